```python
import math
import jax, jax.numpy as jnp
from jax import lax
import numpy as np


D_MODEL = 1024
BATCH = 2
SEQ = 8192
DEPTH = 1

CHUNK = 64
QBLK = 128
EPS = 1e-6
NEG = -1e30

SSM_WIDTH = D_MODEL
SSM_GROUP = 16
SSM_GROUPS = SSM_WIDTH // SSM_GROUP
SSM_STATE = 64
DT_MIN = 1e-3
DT_MAX = 1e-1

N_HEADS = 8
HEAD_DIM = 128
N_KV_HEADS = 2
Q_PER_KV = N_HEADS // N_KV_HEADS
ATT_WIDTH = N_HEADS * HEAD_DIM
KV_WIDTH = N_KV_HEADS * HEAD_DIM
IDX_HEADS = 16
IDX_DIM = 64
TOPK_MAX = 256
ROPE_THETA = 500000.0
ATT_ROT = HEAD_DIM // 4
IDX_ROT = IDX_DIM // 4

FFN_HIDDEN = -(-8 * D_MODEL // (3 * 256)) * 256

SPLIT_SIZES = (SSM_WIDTH, ATT_WIDTH, KV_WIDTH, KV_WIDTH, IDX_HEADS * IDX_DIM, IDX_DIM, IDX_HEADS, D_MODEL, D_MODEL)
IN_WIDTH = SSM_WIDTH + ATT_WIDTH + 2 * KV_WIDTH + IDX_HEADS * IDX_DIM + IDX_DIM + IDX_HEADS + 2 * D_MODEL

kernel_name = 'hybrid_s5_dsa_gated_block'


def _split_points():
    pts, acc = [], 0
    for s in SPLIT_SIZES[:-1]:
        acc += s
        pts.append(acc)
    return pts


def _rmsnorm(x, g):
    xf = x.astype(jnp.float32)
    y = xf * lax.rsqrt(jnp.mean(xf * xf, axis=-1, keepdims=True) + EPS)
    return (y * g.astype(jnp.float32)).astype(x.dtype)


def _rope_partial(x, pos, rot_dim):
    half = rot_dim // 2
    inv_freq = ROPE_THETA ** (-jnp.arange(half, dtype=jnp.float32) / half)
    ang = pos.astype(jnp.float32)[:, None] * inv_freq[None, :]
    cos = jnp.cos(ang)[:, None, :]
    sin = jnp.sin(ang)[:, None, :]
    xr = x[..., :rot_dim].astype(jnp.float32)
    x1, x2 = xr[..., :half], xr[..., half:]
    rot = jnp.concatenate([x1 * cos - x2 * sin, x1 * sin + x2 * cos], axis=-1).astype(x.dtype)
    return jnp.concatenate([rot, x[..., rot_dim:]], axis=-1)


def _s5_branch(u, a_re, a_im, log_dt, b_re, b_im, c_re, c_im, d_skip, w_glu):
    bsz, L, _ = u.shape
    f32 = jnp.float32
    uf = u.astype(f32).reshape(bsz, L, SSM_GROUPS, SSM_GROUP)
    dt = jnp.exp(log_dt.astype(f32))[:, None]
    ar = a_re.astype(f32)
    ai = a_im.astype(f32)
    mag = jnp.exp(ar * dt)
    lb_re = mag * jnp.cos(ai * dt)
    lb_im = mag * jnp.sin(ai * dt)
    num_re = lb_re - 1.0
    num_im = lb_im
    den = ar * ar + ai * ai
    f_re = (num_re * ar + num_im * ai) / den
    f_im = (num_im * ar - num_re * ai) / den
    bu_re = jnp.einsum('blgc,gpc->lbgp', uf, b_re.astype(f32))
    bu_im = jnp.einsum('blgc,gpc->lbgp', uf, b_im.astype(f32))
    x_re = f_re * bu_re - f_im * bu_im
    x_im = f_re * bu_im + f_im * bu_re
    shp = (L, 1, SSM_GROUPS, SSM_STATE)
    a_s_re = jnp.broadcast_to(lb_re, shp)
    a_s_im = jnp.broadcast_to(lb_im, shp)

    def combine(e1, e2):
        a1r, a1i, b1r, b1i = e1
        a2r, a2i, b2r, b2i = e2
        return (a2r * a1r - a2i * a1i,
                a2r * a1i + a2i * a1r,
                a2r * b1r - a2i * b1i + b2r,
                a2r * b1i + a2i * b1r + b2i)

    _, _, h_re, h_im = lax.associative_scan(combine, (a_s_re, a_s_im, x_re, x_im), axis=0)
    y = (jnp.einsum('lbgp,gcp->blgc', h_re, c_re.astype(f32))
         - jnp.einsum('lbgp,gcp->blgc', h_im, c_im.astype(f32))
         + d_skip.astype(f32) * uf)
    y = jax.nn.gelu(y.reshape(bsz, L, SSM_WIDTH))
    y = y * jax.nn.sigmoid(y @ w_glu.astype(f32))
    return y.astype(u.dtype)


def _dsa_branch(q, k, v, qi, ki, wi, pos):
    bsz, L, _ = q.shape
    f32 = jnp.float32
    k_sel = min(TOPK_MAX, L // 4)
    nblk = L // QBLK
    q = _rope_partial(q.reshape(bsz, L, N_HEADS, HEAD_DIM), pos, ATT_ROT)
    k = _rope_partial(k.reshape(bsz, L, N_KV_HEADS, HEAD_DIM), pos, ATT_ROT)
    v = v.reshape(bsz, L, N_KV_HEADS, HEAD_DIM)
    qi = _rope_partial(qi.reshape(bsz, L, IDX_HEADS, IDX_DIM), pos, IDX_ROT)
    ki = _rope_partial(ki.reshape(bsz, L, 1, IDX_DIM), pos, IDX_ROT)[:, :, 0].astype(f32)
    key_chunk = pos // CHUNK

    def to_blocks(a):
        return jnp.moveaxis(a.reshape((bsz, nblk, QBLK) + a.shape[2:]), 1, 0)

    def block_fn(args):
        qb, qib, wb, qpos = args
        rel = jnp.einsum('bqhd,bsd->bqhs', qib.astype(f32), ki) * (IDX_DIM ** -0.5)
        score = jnp.einsum('bqhs,bqh->bqs', jax.nn.relu(rel), wb.astype(f32) * (IDX_HEADS ** -0.5))
        allowed = key_chunk[None, :] <= (qpos // CHUNK)[:, None]
        score = jnp.where(allowed[None], score, NEG)
        top_val, top_idx = lax.top_k(score, k_sel)
        valid = top_val > 0.5 * NEG
        kg = jax.vmap(lambda kk, ii: kk[ii])(k, top_idx)
        vg = jax.vmap(lambda vv, ii: vv[ii])(v, top_idx)
        qg = qb.reshape(bsz, QBLK, N_KV_HEADS, Q_PER_KV, HEAD_DIM)
        s = jnp.einsum('bqgrd,bqkgd->bqgrk', qg, kg).astype(f32) * (HEAD_DIM ** -0.5)
        s = jnp.where(valid[:, :, None, None, :], s, NEG)
        p = jax.nn.softmax(s, axis=-1).astype(v.dtype)
        o = jnp.einsum('bqgrk,bqkgd->bqgrd', p, vg)
        return o.reshape(bsz, QBLK, ATT_WIDTH)

    out = lax.map(block_fn, (to_blocks(q), to_blocks(qi), to_blocks(wi), pos.reshape(nblk, QBLK)))
    return jnp.moveaxis(out, 0, 1).reshape(bsz, L, ATT_WIDTH)


def setup_inputs(seed: int = 0) -> dict:
    key = jax.random.key(seed)
    ks = jax.random.split(key, 19)
    f32 = jnp.float32
    G, P, C = SSM_GROUPS, SSM_STATE, SSM_GROUP

    def nrm(k, shape, scale):
        return jax.random.normal(k, shape, f32) * scale

    x = nrm(ks[0], (BATCH, SEQ, D_MODEL), 1.0)
    norm1_g = 1.0 + nrm(ks[1], (DEPTH, D_MODEL), 0.02)
    w_in = nrm(ks[2], (DEPTH, D_MODEL, IN_WIDTH), D_MODEL ** -0.5)
    a_re = -0.5 * jnp.exp(nrm(ks[3], (DEPTH, G, P), 0.02))
    a_im = math.pi * jnp.arange(P, dtype=f32) + nrm(ks[4], (DEPTH, G, P), 0.02)
    log_dt = jax.random.uniform(ks[5], (DEPTH, G), f32, math.log(DT_MIN), math.log(DT_MAX))
    b_re = nrm(ks[6], (DEPTH, G, P, C), (2 * C) ** -0.5)
    b_im = nrm(ks[7], (DEPTH, G, P, C), (2 * C) ** -0.5)
    c_re = nrm(ks[8], (DEPTH, G, C, P), P ** -0.5)
    c_im = nrm(ks[9], (DEPTH, G, C, P), P ** -0.5)
    d_skip = nrm(ks[10], (DEPTH, G, C), 1.0)
    w_glu = nrm(ks[11], (DEPTH, SSM_WIDTH, SSM_WIDTH), SSM_WIDTH ** -0.5)
    w_branch_a = nrm(ks[12], (DEPTH, SSM_WIDTH, D_MODEL), SSM_WIDTH ** -0.5)
    w_branch_b = nrm(ks[13], (DEPTH, ATT_WIDTH, D_MODEL), ATT_WIDTH ** -0.5)
    w_out = nrm(ks[14], (DEPTH, D_MODEL, D_MODEL), D_MODEL ** -0.5)
    norm2_g = 1.0 + nrm(ks[15], (DEPTH, D_MODEL), 0.02)
    w_ffn_in = nrm(ks[16], (DEPTH, D_MODEL, 2 * FFN_HIDDEN), D_MODEL ** -0.5)
    w_ffn_out = nrm(ks[17], (DEPTH, FFN_HIDDEN, D_MODEL), FFN_HIDDEN ** -0.5)
    norm_f_g = 1.0 + nrm(ks[18], (D_MODEL,), 0.02)
    return {'x': x, 'norm1_g': norm1_g, 'w_in': w_in, 'a_re': a_re, 'a_im': a_im, 'log_dt': log_dt,
            'b_re': b_re, 'b_im': b_im, 'c_re': c_re, 'c_im': c_im, 'd_skip': d_skip, 'w_glu': w_glu,
            'w_branch_a': w_branch_a, 'w_branch_b': w_branch_b, 'w_out': w_out, 'norm2_g': norm2_g,
            'w_ffn_in': w_ffn_in, 'w_ffn_out': w_ffn_out, 'norm_f_g': norm_f_g}


def reference(x, norm1_g, w_in, a_re, a_im, log_dt, b_re, b_im, c_re, c_im, d_skip, w_glu,
              w_branch_a, w_branch_b, w_out, norm2_g, w_ffn_in, w_ffn_out, norm_f_g):
    L = x.shape[1]
    pos = jnp.arange(L, dtype=jnp.int32)
    split_pts = _split_points()
    for i in range(DEPTH):
        h = _rmsnorm(x, norm1_g[i])
        proj = h @ w_in[i]
        u, q, k, v, qi, ki, wi, ga, gb = jnp.split(proj, split_pts, axis=-1)
        ya = _s5_branch(u, a_re[i], a_im[i], log_dt[i], b_re[i], b_im[i], c_re[i], c_im[i], d_skip[i], w_glu[i])
        yb = _dsa_branch(q, k, v, qi, ki, wi, pos)
        merged = jax.nn.sigmoid(ga) * (ya @ w_branch_a[i]) + jax.nn.sigmoid(gb) * (yb @ w_branch_b[i])
        x = x + merged @ w_out[i]
        h2 = _rmsnorm(x, norm2_g[i])
        g, up = jnp.split(h2 @ w_ffn_in[i], [FFN_HIDDEN], axis=-1)
        x = x + (jax.nn.silu(g) * up) @ w_ffn_out[i]
    return _rmsnorm(x, norm_f_g)
```

```python
import functools
import math

import jax
import jax.numpy as jnp
import numpy as np
from jax import lax
from jax.experimental import pallas as pl
from jax.experimental.pallas import tpu as pltpu

F32 = jnp.float32
BF16 = jnp.bfloat16
I32 = jnp.int32

D_MODEL = 1024
CHUNK = 64
EPS = 1e-6
NEG = -1e30

SSM_GROUP = 16
SSM_GROUPS = 64
SSM_STATE = 64
S5_CHUNK = 16
S5_STREAMS = 8

N_HEADS = 8
HEAD_DIM = 128
N_KV_HEADS = 2
Q_PER_KV = N_HEADS // N_KV_HEADS
ATT_WIDTH = N_HEADS * HEAD_DIM
KV_WIDTH = N_KV_HEADS * HEAD_DIM
IDX_HEADS = 16
IDX_DIM = 64
TOPK_MAX = 256
ROPE_THETA = 500000.0
ATT_ROT = HEAD_DIM // 4
IDX_ROT = IDX_DIM // 4
FFN_HIDDEN = -(-8 * D_MODEL // (3 * 256)) * 256

LANES = 128
INT_MIN = -(2 ** 31)

VMEM_LIMIT = 56 * 1024 * 1024


def _key_of_float(val):
    bits = int(np.float32(val).view(np.int32))
    return bits ^ ((bits >> 31) & 0x7FFFFFFF)


KEY_HALF_NEG = _key_of_float(0.5 * NEG)


def _dot(a, b):
    return jnp.dot(a, b, preferred_element_type=F32)


def _dot_nt(a, b):
    return lax.dot_general(a, b, (((1,), (1,)), ((), ())), preferred_element_type=F32)


def _split_hi_lo(a):
    hi = a.astype(BF16)
    lo = (a - hi.astype(F32)).astype(BF16)
    return hi, lo


def _rope(x, c, s1, s2, half):
    n = x.shape[-1]
    return x * c + pltpu.roll(x, half, 1) * s1 + pltpu.roll(x, n - half, 1) * s2


def _proj_kernel(x_ref, g1_ref, wu_ref, wq_ref, wk_ref, wv_ref, wqi_ref, wsm_ref, wg_ref,
                 ca_ref, sa1_ref, sa2_ref, ci_ref, si1_ref, si2_ref, cs_ref, ss1_ref, ss2_ref,
                 u_ref, q_ref, k_ref, v_ref, qi_ref, ki2_ref, wi_ref, gate_ref):
    x = x_ref[...]
    h = x * lax.rsqrt(jnp.mean(x * x, axis=-1, keepdims=True) + EPS) * g1_ref[...]
    hb = h.astype(BF16)

    u_ref[...] = _dot(hb, wu_ref[...]).astype(BF16)
    v_ref[...] = _dot(hb, wv_ref[...]).astype(BF16)
    gate_ref[...] = _dot(hb, wg_ref[...])

    ca, sa1, sa2 = ca_ref[...], sa1_ref[...], sa2_ref[...]
    q = _dot(hb, wq_ref[...])
    scale = HEAD_DIM ** -0.5
    for hd in range(N_HEADS):
        sl = slice(hd * HEAD_DIM, (hd + 1) * HEAD_DIM)
        q_ref[:, sl] = (_rope(q[:, sl], ca, sa1, sa2, ATT_ROT // 2) * scale).astype(BF16)
    k = _dot(hb, wk_ref[...])
    for hd in range(N_KV_HEADS):
        sl = slice(hd * HEAD_DIM, (hd + 1) * HEAD_DIM)
        k_ref[:, sl] = _rope(k[:, sl], ca, sa1, sa2, ATT_ROT // 2).astype(BF16)

    ci, si1, si2 = ci_ref[...], si1_ref[...], si2_ref[...]
    qi = _dot(hb, wqi_ref[...])
    for pr in range(IDX_HEADS * IDX_DIM // LANES):
        sl = slice(pr * LANES, (pr + 1) * LANES)
        qi_ref[:, sl] = _rope(qi[:, sl], ci, si1, si2, IDX_ROT // 2).astype(BF16)

    sm = _dot(hb, wsm_ref[...])
    sm = _rope(sm, cs_ref[...], ss1_ref[...], ss2_ref[...], IDX_ROT // 2)
    lane = lax.broadcasted_iota(I32, sm.shape, 1)
    ki2_ref[...] = jnp.where(lane < IDX_DIM, sm, pltpu.roll(sm, IDX_DIM, 1)).astype(BF16)
    wi_ref[...] = sm * (IDX_DIM ** -0.5 * IDX_HEADS ** -0.5)


def _rope_tables(L, rot, period, active_lanes):
    half = rot // 2
    pos = jnp.arange(L, dtype=jnp.int32)
    inv_freq = ROPE_THETA ** (-jnp.arange(half, dtype=F32) / half)
    ang = pos.astype(F32)[:, None] * inv_freq[None, :]
    cos, sin = jnp.cos(ang), jnp.sin(ang)
    lane = np.arange(LANES)
    within = lane % period
    fidx = np.where(within < rot, within % half, 0)
    is_x1 = (within < half) & (lane < active_lanes)
    is_x2 = (within >= half) & (within < rot) & (lane < active_lanes)
    rot_lane = is_x1 | is_x2
    c = jnp.where(rot_lane[None, :], cos[:, fidx], 1.0)
    s1 = jnp.where(is_x2[None, :], sin[:, fidx], 0.0)
    s2 = jnp.where(is_x1[None, :], -sin[:, fidx], 0.0)
    return c.astype(F32), s1.astype(F32), s2.astype(F32)


def _project(x2, g1, w_in, L, tile):
    T = x2.shape[0]
    o = 0
    parts = []
    for s in (D_MODEL, ATT_WIDTH, KV_WIDTH, KV_WIDTH, IDX_HEADS * IDX_DIM, IDX_DIM, IDX_HEADS, D_MODEL, D_MODEL):
        parts.append(w_in[:, o:o + s])
        o += s
    wu, wq, wk, wv, wqi, wki, wwi, wga, wgb = parts
    wsm = jnp.concatenate([wki, wwi, jnp.zeros((D_MODEL, LANES - IDX_DIM - IDX_HEADS), w_in.dtype)], axis=1)
    wg = jnp.concatenate([wga, wgb], axis=1)
    ws = [w.astype(BF16) for w in (wu, wq, wk, wv, wqi, wsm, wg)]
    tabs = (_rope_tables(L, ATT_ROT, HEAD_DIM, LANES)
            + _rope_tables(L, IDX_ROT, IDX_DIM, LANES)
            + _rope_tables(L, IDX_ROT, IDX_DIM, IDX_DIM))
    nl = L // tile

    def row(w):
        return pl.BlockSpec((tile, w), lambda i: (i, 0))

    def full(a):
        return pl.BlockSpec(a.shape, lambda i: (0, 0))

    tab_spec = pl.BlockSpec((tile, LANES), lambda i: (i % nl, 0))
    out_shape = (
        jax.ShapeDtypeStruct((T, D_MODEL), BF16),
        jax.ShapeDtypeStruct((T, ATT_WIDTH), BF16),
        jax.ShapeDtypeStruct((T, KV_WIDTH), BF16),
        jax.ShapeDtypeStruct((T, KV_WIDTH), BF16),
        jax.ShapeDtypeStruct((T, IDX_HEADS * IDX_DIM), BF16),
        jax.ShapeDtypeStruct((T, LANES), BF16),
        jax.ShapeDtypeStruct((T, LANES), F32),
        jax.ShapeDtypeStruct((T, 2 * D_MODEL), F32),
    )
    return pl.pallas_call(
        _proj_kernel,
        grid=(T // tile,),
        in_specs=[row(D_MODEL), full(g1)] + [full(w) for w in ws] + [tab_spec] * 9,
        out_specs=tuple(row(s.shape[1]) for s in out_shape),
        out_shape=out_shape,
        compiler_params=pltpu.CompilerParams(dimension_semantics=("arbitrary",), vmem_limit_bytes=VMEM_LIMIT),
        name="proj",
    )(x2, g1, *ws, *tabs)


def _s5_kernel(up_ref, are_ref, aim_ref, ldt_ref, btr_ref, bti_ref, cre_ref, cim_ref, dsk_ref,
               y_ref, s_ref, hprev_ref, *, n_steps, n_seg):
    lane = lax.broadcasted_iota(I32, (1, LANES), 1)
    lo = lane < SSM_STATE
    dt = jnp.exp(ldt_ref[...])
    ar, ai = are_ref[...], aim_ref[...]
    rho, th = ar * dt, ai * dt
    npow = S5_CHUNK + 1
    nn = lax.broadcasted_iota(I32, (24, LANES), 0).astype(F32)
    mag = jnp.exp(nn * rho)
    lc = mag * jnp.cos(nn * th)
    ls = mag * jnp.sin(nn * th)
    lam_pk = jnp.where(lo, lc, ls)
    lam_sw = jnp.where(lo, -ls, lc)
    num_re, num_im = lc[1:2] - 1.0, ls[1:2]
    den = ar * ar + ai * ai
    f_re = (num_re * ar + num_im * ai) / den
    f_im = (num_im * ar - num_re * ai) / den
    btr, bti = btr_ref[...], bti_ref[...]
    g_re = btr * f_re - bti * f_im
    g_im = btr * f_im + bti * f_re
    g_neg = jnp.where(lo, g_re, -g_im)
    cre, cim = cre_ref[...], cim_ref[...]
    w_pk = [cre * lam_pk[n:n + 1] + cim * lam_sw[n:n + 1] for n in range(npow)]

    w0 = jnp.concatenate(w_pk[0:S5_CHUNK], axis=0)
    kt_row = lax.dot_general(g_neg, w0, (((1,), (1,)), ((), ())), preferred_element_type=F32,
                             precision=lax.Precision.HIGHEST)
    lane_m = lax.broadcasted_iota(I32, (SSM_GROUP, S5_CHUNK * SSM_GROUP), 1)
    blocks = []
    for s in range(S5_CHUNK):
        r = kt_row if s == 0 else pltpu.roll(kt_row, SSM_GROUP * s, 1)
        blocks.append(jnp.where(lane_m >= SSM_GROUP * s, r, 0.0))
    m_toe = jnp.concatenate(blocks, axis=0)

    w_state = jnp.concatenate(
        [g_re * lam_pk[S5_CHUNK - 1 - ti:S5_CHUNK - ti] + g_im * lam_sw[S5_CHUNK - 1 - ti:S5_CHUNK - ti]
         for ti in range(S5_CHUNK)], axis=0)
    w_out_t = jnp.concatenate([jnp.where(lo, w_pk[ti + 1], -w_pk[ti + 1]) for ti in range(S5_CHUNK)], axis=0)

    u = up_ref[...]
    ws_hi, ws_lo = _split_hi_lo(w_state)
    s_ref[...] = _dot(u, ws_hi) + _dot(u, ws_lo)

    a_p = jnp.where(lo, lam_pk[S5_CHUNK:S5_CHUNK + 1], lam_sw[S5_CHUNK:S5_CHUNK + 1])
    a_q = jnp.where(lo, lam_sw[S5_CHUNK:S5_CHUNK + 1], lam_pk[S5_CHUNK:S5_CHUNK + 1])

    def cmul(h, p, q):
        return h * p + pltpu.roll(h, SSM_STATE, 1) * q

    lane8 = lax.broadcasted_iota(I32, (S5_STREAMS, LANES), 1)
    one_pk = jnp.where(lane8 < SSM_STATE, 1.0, 0.0).astype(F32)

    def pass1(i, carry):
        h, p = carry
        off = pl.multiple_of(i * S5_STREAMS, S5_STREAMS)
        return cmul(h, a_p, a_q) + s_ref[pl.ds(off, S5_STREAMS), :], cmul(p, a_p, a_q)

    z, a_seg = lax.fori_loop(0, n_steps, pass1, (jnp.zeros((S5_STREAMS, LANES), F32), one_pk))
    a_seg = a_seg[0:1]
    a_seg_sw = pltpu.roll(a_seg, SSM_STATE, 1)
    seg_p = jnp.where(lo, a_seg, a_seg_sw)
    seg_q = jnp.where(lo, -a_seg_sw, a_seg)

    row8 = lax.broadcasted_iota(I32, (S5_STREAMS, LANES), 0)
    init = jnp.zeros((S5_STREAMS, LANES), F32)
    prev = None
    for s in range(S5_STREAMS):
        if s % n_seg == 0:
            cur = jnp.zeros((1, LANES), F32)
        else:
            cur = cmul(prev, seg_p, seg_q) + z[s - 1:s]
            init = jnp.where(row8 == s, cur, init)
        prev = cur

    def pass2(i, h):
        off = pl.multiple_of(i * S5_STREAMS, S5_STREAMS)
        hprev_ref[pl.ds(off, S5_STREAMS), :] = h
        return cmul(h, a_p, a_q) + s_ref[pl.ds(off, S5_STREAMS), :]

    lax.fori_loop(0, n_steps, pass2, init)

    m_hi, m_lo = _split_hi_lo(m_toe)
    wo_hi, wo_lo = _split_hi_lo(w_out_t)
    hp_hi, hp_lo = _split_hi_lo(hprev_ref[...])
    y = _dot(u, m_hi) + _dot(u, m_lo)
    y = y + _dot_nt(hp_hi, wo_hi) + _dot_nt(hp_hi, wo_lo) + _dot_nt(hp_lo, wo_hi)
    y_ref[...] = y + u.astype(F32) * dsk_ref[...]


def _s5_branch(u, B, L, a_re, a_im, log_dt, b_re, b_im, c_re, c_im, d_skip):
    G, C, P = SSM_GROUPS, SSM_GROUP, SSM_STATE
    n_seg = S5_STREAMS // B
    n_steps = L // (S5_CHUNK * n_seg)
    R = n_steps * S5_STREAMS
    up = u.reshape(B, n_seg, n_steps, S5_CHUNK, G, C).transpose(4, 2, 0, 1, 3, 5).reshape(G, R, S5_CHUNK * C)

    def dup(a):
        return jnp.concatenate([a, a], axis=-1).astype(F32)

    are2 = dup(a_re)[:, None, :]
    aim2 = dup(a_im)[:, None, :]
    ldt = log_dt.astype(F32)[:, None, None]
    btr2 = dup(jnp.swapaxes(b_re, 1, 2))
    bti2 = dup(jnp.swapaxes(b_im, 1, 2))
    cre2 = dup(c_re)
    cim2 = dup(c_im)
    dsk = jnp.tile(d_skip.astype(F32), (1, S5_CHUNK))[:, None, :]

    def per_group(shape):
        return pl.BlockSpec((None,) + shape, lambda g: (g, 0, 0))

    yp = pl.pallas_call(
        functools.partial(_s5_kernel, n_steps=n_steps, n_seg=n_seg),
        grid=(G,),
        in_specs=[per_group((R, S5_CHUNK * C)), per_group((1, LANES)), per_group((1, LANES)), per_group((1, 1)),
                  per_group((C, LANES)), per_group((C, LANES)), per_group((C, LANES)), per_group((C, LANES)),
                  per_group((1, S5_CHUNK * C))],
        out_specs=per_group((R, S5_CHUNK * C)),
        out_shape=jax.ShapeDtypeStruct((G, R, S5_CHUNK * C), F32),
        scratch_shapes=[pltpu.VMEM((R, LANES), F32), pltpu.VMEM((R, LANES), F32)],
        compiler_params=pltpu.CompilerParams(dimension_semantics=("arbitrary",), vmem_limit_bytes=VMEM_LIMIT),
        name="s5",
    )(up, are2, aim2, ldt, btr2, bti2, cre2, cim2, dsk)
    y = yp.reshape(G, n_steps, B, n_seg, S5_CHUNK, C).transpose(2, 3, 1, 4, 0, 5)
    return y.reshape(B * L, G * C)


def _dsa_kernel(qi_ref, wi_ref, q_ref, ki2_ref, k_ref, v_ref, o_ref,
                key_ref, qm_ref, m_ref, l_ref, acc_ref, *, k_sel, qblk, ktile):
    qb = pl.program_id(1)
    n_tiles = lax.div(qb * qblk + qblk + ktile - 1, ktile)
    lane = lax.broadcasted_iota(I32, (qblk, LANES), 1)

    for hd in range(IDX_HEADS):
        pair = qi_ref[:, (hd // 2) * LANES:(hd // 2 + 1) * LANES]
        msk = (lane < IDX_DIM) if hd % 2 == 0 else (lane >= IDX_DIM)
        qm_ref[hd] = jnp.where(msk, pair, jnp.zeros_like(pair))

    wi = wi_ref[...]
    q_chunk = lax.shift_right_logical(qb * qblk + lax.broadcasted_iota(I32, (qblk, 1), 0), 6)

    def score_tile(t, carry):
        off = pl.multiple_of(t * ktile, ktile)
        kt = ki2_ref[pl.ds(off, ktile), :]
        sc = jnp.zeros((qblk, ktile), F32)
        for hd in range(IDX_HEADS):
            rel = _dot_nt(qm_ref[hd], kt)
            sc = sc + jnp.maximum(rel, 0.0) * wi[:, IDX_DIM + hd:IDX_DIM + hd + 1]
        k_chunk = lax.shift_right_logical(off + lax.broadcasted_iota(I32, (qblk, ktile), 1), 6)
        sc = jnp.where(k_chunk <= q_chunk, sc, NEG)
        bits = lax.bitcast_convert_type(sc, I32)
        key_ref[:, pl.ds(off, ktile)] = bits ^ (lax.shift_right_arithmetic(bits, 31) & 0x7FFFFFFF)
        return carry

    lax.fori_loop(0, n_tiles, score_tile, 0)

    def bit_step(bi, prefix):
        trial_u = prefix | lax.shift_left(jnp.int32(1), 31 - bi)
        trial_s = trial_u ^ INT_MIN

        def count_tile(t, acc):
            off = pl.multiple_of(t * ktile, ktile)
            c = (key_ref[:, pl.ds(off, ktile)] >= trial_s).astype(I32)
            for j in range(ktile // LANES):
                acc = acc + c[:, j * LANES:(j + 1) * LANES]
            return acc

        acc = lax.fori_loop(0, n_tiles, count_tile, jnp.zeros((qblk, LANES), I32))
        cnt = jnp.sum(acc, axis=1, keepdims=True)
        return jnp.where(cnt >= k_sel, trial_u, prefix)

    prefix = lax.fori_loop(0, 32, bit_step, jnp.zeros((qblk, 1), I32))
    thr = jnp.maximum(prefix ^ INT_MIN, KEY_HALF_NEG + 1)

    m_ref[...] = jnp.full(m_ref.shape, -jnp.inf, F32)
    l_ref[...] = jnp.zeros(l_ref.shape, F32)
    acc_ref[...] = jnp.zeros(acc_ref.shape, F32)

    def attend_tile(t, carry):
        off = pl.multiple_of(t * ktile, ktile)
        bias = jnp.where(key_ref[:, pl.ds(off, ktile)] >= thr, 0.0, NEG)
        for g in range(N_KV_HEADS):
            kt = k_ref[pl.ds(off, ktile), g * HEAD_DIM:(g + 1) * HEAD_DIM]
            vt = v_ref[pl.ds(off, ktile), g * HEAD_DIM:(g + 1) * HEAD_DIM]
            for r in range(Q_PER_KV):
                hd = g * Q_PER_KV + r
                s = _dot_nt(q_ref[:, hd * HEAD_DIM:(hd + 1) * HEAD_DIM], kt) + bias
                m_old = m_ref[hd]
                m_new = jnp.maximum(m_old, jnp.max(s, axis=1, keepdims=True))
                alpha = jnp.exp(m_old - m_new)
                p = jnp.exp(s - m_new[:, 0:1])
                l_ref[hd] = alpha * l_ref[hd] + jnp.sum(p, axis=1, keepdims=True)
                acc_ref[hd] = alpha * acc_ref[hd] + _dot(p.astype(BF16), vt)
                m_ref[hd] = m_new
        return carry

    lax.fori_loop(0, n_tiles, attend_tile, 0)
    for hd in range(N_HEADS):
        o_ref[:, hd * HEAD_DIM:(hd + 1) * HEAD_DIM] = (acc_ref[hd] / l_ref[hd]).astype(o_ref.dtype)


def _dsa_branch(q, k, v, qi, ki2, wi, B, L, qblk=128, ktile=256):
    k_sel = min(TOPK_MAX, L // 4)
    nq = L // qblk

    def qrow(w):
        return pl.BlockSpec((None, qblk, w), lambda b, i: (b, i, 0))

    def whole(w):
        return pl.BlockSpec((None, L, w), lambda b, i: (b, 0, 0))

    r3 = lambda a: a.reshape(B, L, a.shape[-1])
    out = pl.pallas_call(
        functools.partial(_dsa_kernel, k_sel=k_sel, qblk=qblk, ktile=ktile),
        grid=(B, nq),
        in_specs=[qrow(IDX_HEADS * IDX_DIM), qrow(LANES), qrow(ATT_WIDTH), whole(LANES), whole(KV_WIDTH), whole(KV_WIDTH)],
        out_specs=qrow(ATT_WIDTH),
        out_shape=jax.ShapeDtypeStruct((B, L, ATT_WIDTH), BF16),
        scratch_shapes=[pltpu.VMEM((qblk, L), I32),
                        pltpu.VMEM((IDX_HEADS, qblk, LANES), BF16),
                        pltpu.VMEM((N_HEADS, qblk, LANES), F32),
                        pltpu.VMEM((N_HEADS, qblk, LANES), F32),
                        pltpu.VMEM((N_HEADS, qblk, HEAD_DIM), F32)],
        compiler_params=pltpu.CompilerParams(dimension_semantics=("arbitrary", "arbitrary"),
                                             vmem_limit_bytes=VMEM_LIMIT),
        name="dsa",
    )(r3(qi), r3(wi), r3(q), r3(ki2), r3(k), r3(v))
    return out.reshape(B * L, ATT_WIDTH)


def _gelu_tanh(x):
    return 0.5 * x * (1.0 + jnp.tanh(math.sqrt(2.0 / math.pi) * (x + 0.044715 * (x * x * x))))


def _merge_kernel(x_ref, y_ref, yb_ref, gate_ref, wglu_ref, wa_ref, wb_ref, wo_ref, o_ref):
    y = _gelu_tanh(y_ref[...])
    ya = y * jax.nn.sigmoid(_dot(y.astype(BF16), wglu_ref[...]))
    gate = gate_ref[...]
    ga, gb = gate[:, :D_MODEL], gate[:, D_MODEL:]
    merged = (jax.nn.sigmoid(ga) * _dot(ya.astype(BF16), wa_ref[...])
              + jax.nn.sigmoid(gb) * _dot(yb_ref[...], wb_ref[...]))
    o_ref[...] = x_ref[...] + _dot(merged.astype(BF16), wo_ref[...])


def _merge(x2, y, yb, gate, w_glu, w_a, w_b, w_out, tile):
    T = x2.shape[0]
    ws = [w.astype(BF16) for w in (w_glu, w_a, w_b, w_out)]

    def row(w):
        return pl.BlockSpec((tile, w), lambda i: (i, 0))

    return pl.pallas_call(
        _merge_kernel,
        grid=(T // tile,),
        in_specs=[row(D_MODEL), row(D_MODEL), row(ATT_WIDTH), row(2 * D_MODEL)]
                 + [pl.BlockSpec(w.shape, lambda i: (0, 0)) for w in ws],
        out_specs=row(D_MODEL),
        out_shape=jax.ShapeDtypeStruct((T, D_MODEL), F32),
        compiler_params=pltpu.CompilerParams(dimension_semantics=("arbitrary",), vmem_limit_bytes=VMEM_LIMIT),
        name="merge",
    )(x2, y, yb, gate, *ws)


def _rms(x, g):
    return x * lax.rsqrt(jnp.mean(x * x, axis=-1, keepdims=True) + EPS) * g


def _ffn_kernel(x_ref, g2_ref, win_ref, wout_ref, gf_ref, o_ref, *, final_norm):
    x = x_ref[...]
    h = _rms(x, g2_ref[...]).astype(BF16)
    gu = _dot(h, win_ref[...])
    g, up = gu[:, :FFN_HIDDEN], gu[:, FFN_HIDDEN:]
    act = (g * jax.nn.sigmoid(g)) * up
    x = x + _dot(act.astype(BF16), wout_ref[...])
    o_ref[...] = _rms(x, gf_ref[...]) if final_norm else x


def _ffn(x1, g2, w_ffn_in, w_ffn_out, gf, final_norm, tile):
    T = x1.shape[0]
    win, wout = w_ffn_in.astype(BF16), w_ffn_out.astype(BF16)

    def row(w):
        return pl.BlockSpec((tile, w), lambda i: (i, 0))

    def full(a):
        return pl.BlockSpec(a.shape, lambda i: (0, 0))

    return pl.pallas_call(
        functools.partial(_ffn_kernel, final_norm=final_norm),
        grid=(T // tile,),
        in_specs=[row(D_MODEL), full(g2), full(win), full(wout), full(gf)],
        out_specs=row(D_MODEL),
        out_shape=jax.ShapeDtypeStruct((T, D_MODEL), F32),
        compiler_params=pltpu.CompilerParams(dimension_semantics=("arbitrary",), vmem_limit_bytes=VMEM_LIMIT),
        name="ffn",
    )(x1, g2, win, wout, gf)


def kernel(x, norm1_g, w_in, a_re, a_im, log_dt, b_re, b_im, c_re, c_im, d_skip, w_glu,
           w_branch_a, w_branch_b, w_out, norm2_g, w_ffn_in, w_ffn_out, norm_f_g):
    B, L, D = x.shape
    depth = norm1_g.shape[0]
    assert D == D_MODEL and S5_STREAMS % B == 0 and L % 256 == 0
    tile = 256
    x2 = x.reshape(B * L, D).astype(F32)
    for i in range(depth):
        u, q, k, v, qi, ki2, wi, gate = _project(x2, norm1_g[i][None, :].astype(F32), w_in[i], L, tile)
        y = _s5_branch(u, B, L, a_re[i], a_im[i], log_dt[i], b_re[i], b_im[i], c_re[i], c_im[i], d_skip[i])
        yb = _dsa_branch(q, k, v, qi, ki2, wi, B, L)
        x1 = _merge(x2, y, yb, gate, w_glu[i], w_branch_a[i], w_branch_b[i], w_out[i], tile)
        x2 = _ffn(x1, norm2_g[i][None, :].astype(F32), w_ffn_in[i], w_ffn_out[i],
                  norm_f_g[None, :].astype(F32), i == depth - 1, tile)
    return x2.reshape(B, L, D)
```

```python
import functools
import math

import jax
import jax.numpy as jnp
import numpy as np
from jax import lax
from jax.experimental import pallas as pl
from jax.experimental.pallas import tpu as pltpu

F32 = jnp.float32
BF16 = jnp.bfloat16
I32 = jnp.int32

D_MODEL = 1024
CHUNK = 64
EPS = 1e-6
NEG = -1e30

SSM_GROUP = 16
SSM_GROUPS = 64
SSM_STATE = 64
S5_CHUNK = 16
S5_STREAMS = 8

N_HEADS = 8
HEAD_DIM = 128
N_KV_HEADS = 2
Q_PER_KV = N_HEADS // N_KV_HEADS
ATT_WIDTH = N_HEADS * HEAD_DIM
KV_WIDTH = N_KV_HEADS * HEAD_DIM
IDX_HEADS = 16
IDX_DIM = 64
TOPK_MAX = 256
ROPE_THETA = 500000.0
ATT_ROT = HEAD_DIM // 4
IDX_ROT = IDX_DIM // 4
FFN_HIDDEN = -(-8 * D_MODEL // (3 * 256)) * 256

SCORE_ROWS = 16
ATT_ROWS = 32

LANES = 128
INT_MIN = -(2 ** 31)

VMEM_LIMIT = 56 * 1024 * 1024


def _key_of_float(val):
    bits = int(np.float32(val).view(np.int32))
    return bits ^ ((bits >> 31) & 0x7FFFFFFF)


KEY_HALF_NEG = _key_of_float(0.5 * NEG)


def _dot(a, b):
    return jnp.dot(a, b, preferred_element_type=F32)


def _dot_nt(a, b):
    return lax.dot_general(a, b, (((1,), (1,)), ((), ())), preferred_element_type=F32)


def _split_hi_lo(a):
    hi = a.astype(BF16)
    lo = (a - hi.astype(F32)).astype(BF16)
    return hi, lo


def _rope(x, c, s1, s2, half):
    n = x.shape[-1]
    return x * c + pltpu.roll(x, half, 1) * s1 + pltpu.roll(x, n - half, 1) * s2


def _proj_kernel(x_ref, g1_ref, wu_ref, wq_ref, wk_ref, wv_ref, wqi_ref, wsm_ref, wg_ref,
                 ca_ref, sa1_ref, sa2_ref, ci_ref, si1_ref, si2_ref, cs_ref, ss1_ref, ss2_ref,
                 u_ref, q_ref, k_ref, v_ref, qi_ref, ki2_ref, wi_ref, gate_ref):
    x = x_ref[...]
    h = x * lax.rsqrt(jnp.mean(x * x, axis=-1, keepdims=True) + EPS) * g1_ref[...]
    hb = h.astype(BF16)

    u_ref[...] = _dot(hb, wu_ref[...]).astype(BF16)
    v = _dot(hb, wv_ref[...]).astype(BF16)
    ones = jnp.ones((v.shape[0], HEAD_DIM), BF16)
    for hd in range(N_KV_HEADS):
        v_ref[:, 2 * hd * HEAD_DIM:(2 * hd + 1) * HEAD_DIM] = v[:, hd * HEAD_DIM:(hd + 1) * HEAD_DIM]
        v_ref[:, (2 * hd + 1) * HEAD_DIM:(2 * hd + 2) * HEAD_DIM] = ones
    gate_ref[...] = _dot(hb, wg_ref[...])

    ca, sa1, sa2 = ca_ref[...], sa1_ref[...], sa2_ref[...]
    q = _dot(hb, wq_ref[...])
    scale = HEAD_DIM ** -0.5
    for hd in range(N_HEADS):
        sl = slice(hd * HEAD_DIM, (hd + 1) * HEAD_DIM)
        q_ref[:, sl] = (_rope(q[:, sl], ca, sa1, sa2, ATT_ROT // 2) * scale).astype(BF16)
    k = _dot(hb, wk_ref[...])
    for hd in range(N_KV_HEADS):
        sl = slice(hd * HEAD_DIM, (hd + 1) * HEAD_DIM)
        k_ref[:, sl] = _rope(k[:, sl], ca, sa1, sa2, ATT_ROT // 2).astype(BF16)

    ci, si1, si2 = ci_ref[...], si1_ref[...], si2_ref[...]
    qi = _dot(hb, wqi_ref[...])
    for pr in range(IDX_HEADS * IDX_DIM // LANES):
        sl = slice(pr * LANES, (pr + 1) * LANES)
        qi_ref[:, sl] = _rope(qi[:, sl], ci, si1, si2, IDX_ROT // 2).astype(BF16)

    sm = _dot(hb, wsm_ref[...])
    sm = _rope(sm, cs_ref[...], ss1_ref[...], ss2_ref[...], IDX_ROT // 2)
    lane = lax.broadcasted_iota(I32, sm.shape, 1)
    ki2_ref[...] = jnp.where(lane < IDX_DIM, sm, pltpu.roll(sm, IDX_DIM, 1)).astype(BF16)
    wi_ref[...] = sm * (IDX_DIM ** -0.5 * IDX_HEADS ** -0.5)


def _rope_tables(L, rot, period, active_lanes):
    half = rot // 2
    pos = jnp.arange(L, dtype=jnp.int32)
    inv_freq = ROPE_THETA ** (-jnp.arange(half, dtype=F32) / half)
    ang = pos.astype(F32)[:, None] * inv_freq[None, :]
    cos, sin = jnp.cos(ang), jnp.sin(ang)
    lane = np.arange(LANES)
    within = lane % period
    fidx = np.where(within < rot, within % half, 0)
    is_x1 = (within < half) & (lane < active_lanes)
    is_x2 = (within >= half) & (within < rot) & (lane < active_lanes)
    rot_lane = is_x1 | is_x2
    c = jnp.where(rot_lane[None, :], cos[:, fidx], 1.0)
    s1 = jnp.where(is_x2[None, :], sin[:, fidx], 0.0)
    s2 = jnp.where(is_x1[None, :], -sin[:, fidx], 0.0)
    return c.astype(F32), s1.astype(F32), s2.astype(F32)


def _project(x2, g1, w_in, L, tile):
    T = x2.shape[0]
    o = 0
    parts = []
    for s in (D_MODEL, ATT_WIDTH, KV_WIDTH, KV_WIDTH, IDX_HEADS * IDX_DIM, IDX_DIM, IDX_HEADS, D_MODEL, D_MODEL):
        parts.append(w_in[:, o:o + s])
        o += s
    wu, wq, wk, wv, wqi, wki, wwi, wga, wgb = parts
    wsm = jnp.concatenate([wki, wwi, jnp.zeros((D_MODEL, LANES - IDX_DIM - IDX_HEADS), w_in.dtype)], axis=1)
    wg = jnp.concatenate([wga, wgb], axis=1)
    ws = [w.astype(BF16) for w in (wu, wq, wk, wv, wqi, wsm, wg)]
    tabs = (_rope_tables(L, ATT_ROT, HEAD_DIM, LANES)
            + _rope_tables(L, IDX_ROT, IDX_DIM, LANES)
            + _rope_tables(L, IDX_ROT, IDX_DIM, IDX_DIM))
    nl = L // tile

    def row(w):
        return pl.BlockSpec((tile, w), lambda i: (i, 0))

    def full(a):
        return pl.BlockSpec(a.shape, lambda i: (0, 0))

    tab_spec = pl.BlockSpec((tile, LANES), lambda i: (i % nl, 0))
    out_shape = (
        jax.ShapeDtypeStruct((T, D_MODEL), BF16),
        jax.ShapeDtypeStruct((T, ATT_WIDTH), BF16),
        jax.ShapeDtypeStruct((T, KV_WIDTH), BF16),
        jax.ShapeDtypeStruct((T, 2 * KV_WIDTH), BF16),
        jax.ShapeDtypeStruct((T, IDX_HEADS * IDX_DIM), BF16),
        jax.ShapeDtypeStruct((T, LANES), BF16),
        jax.ShapeDtypeStruct((T, LANES), F32),
        jax.ShapeDtypeStruct((T, 2 * D_MODEL), F32),
    )
    return pl.pallas_call(
        _proj_kernel,
        grid=(T // tile,),
        in_specs=[row(D_MODEL), full(g1)] + [full(w) for w in ws] + [tab_spec] * 9,
        out_specs=tuple(row(s.shape[1]) for s in out_shape),
        out_shape=out_shape,
        compiler_params=pltpu.CompilerParams(dimension_semantics=("arbitrary",), vmem_limit_bytes=VMEM_LIMIT),
        name="proj",
    )(x2, g1, *ws, *tabs)


def _s5_kernel(up_ref, are_ref, aim_ref, ldt_ref, btr_ref, bti_ref, cre_ref, cim_ref, dsk_ref,
               y_ref, s_ref, ssw_ref, hprev_ref, *, n_steps, n_seg):
    lane = lax.broadcasted_iota(I32, (1, LANES), 1)
    lo = lane < SSM_STATE
    dt = jnp.exp(ldt_ref[...])
    ar, ai = are_ref[...], aim_ref[...]
    rho, th = ar * dt, ai * dt
    npow = S5_CHUNK + 1
    nn = lax.broadcasted_iota(I32, (24, LANES), 0).astype(F32)
    mag = jnp.exp(nn * rho)
    lc = mag * jnp.cos(nn * th)
    ls = mag * jnp.sin(nn * th)
    lam_pk = jnp.where(lo, lc, ls)
    lam_sw = jnp.where(lo, -ls, lc)
    num_re, num_im = lc[1:2] - 1.0, ls[1:2]
    den = ar * ar + ai * ai
    f_re = (num_re * ar + num_im * ai) / den
    f_im = (num_im * ar - num_re * ai) / den
    btr, bti = btr_ref[...], bti_ref[...]
    g_re = btr * f_re - bti * f_im
    g_im = btr * f_im + bti * f_re
    g_neg = jnp.where(lo, g_re, -g_im)
    cre, cim = cre_ref[...], cim_ref[...]
    w_pk = [cre * lam_pk[n:n + 1] + cim * lam_sw[n:n + 1] for n in range(npow)]

    w0 = jnp.concatenate(w_pk[0:S5_CHUNK], axis=0)
    kt_row = lax.dot_general(g_neg, w0, (((1,), (1,)), ((), ())), preferred_element_type=F32,
                             precision=lax.Precision.HIGHEST)
    lane_m = lax.broadcasted_iota(I32, (SSM_GROUP, S5_CHUNK * SSM_GROUP), 1)
    blocks = []
    for s in range(S5_CHUNK):
        r = kt_row if s == 0 else pltpu.roll(kt_row, SSM_GROUP * s, 1)
        blocks.append(jnp.where(lane_m >= SSM_GROUP * s, r, 0.0))
    m_toe = jnp.concatenate(blocks, axis=0)

    w_state = jnp.concatenate(
        [g_re * lam_pk[S5_CHUNK - 1 - ti:S5_CHUNK - ti] + g_im * lam_sw[S5_CHUNK - 1 - ti:S5_CHUNK - ti]
         for ti in range(S5_CHUNK)], axis=0)
    w_out_t = jnp.concatenate([jnp.where(lo, w_pk[ti + 1], -w_pk[ti + 1]) for ti in range(S5_CHUNK)], axis=0)

    u = up_ref[...]
    ws_hi, ws_lo = _split_hi_lo(w_state)
    s_all = _dot(u, ws_hi) + _dot(u, ws_lo)
    s_ref[...] = s_all
    ssw_ref[...] = pltpu.roll(s_all, SSM_STATE, 1)

    a_p = jnp.where(lo, lam_pk[S5_CHUNK:S5_CHUNK + 1], lam_sw[S5_CHUNK:S5_CHUNK + 1])
    a_q = jnp.where(lo, lam_sw[S5_CHUNK:S5_CHUNK + 1], lam_pk[S5_CHUNK:S5_CHUNK + 1])

    def cmul(h, p, q):
        return h * p + pltpu.roll(h, SSM_STATE, 1) * q

    def step(h, hs, s, ssw):
        return h * a_p + hs * a_q + s, hs * a_p - h * a_q + ssw

    lane8 = lax.broadcasted_iota(I32, (S5_STREAMS, LANES), 1)
    one_pk = jnp.where(lane8 < SSM_STATE, 1.0, 0.0).astype(F32)
    zero8 = jnp.zeros((S5_STREAMS, LANES), F32)

    def pass1(i, carry):
        h, hs, p, ps = carry
        off = pl.multiple_of(i * S5_STREAMS, S5_STREAMS)
        h, hs = step(h, hs, s_ref[pl.ds(off, S5_STREAMS), :], ssw_ref[pl.ds(off, S5_STREAMS), :])
        p, ps = step(p, ps, 0.0, 0.0)
        return h, hs, p, ps

    z, _, a_seg, _ = lax.fori_loop(0, n_steps, pass1, (zero8, zero8, one_pk, 1.0 - one_pk))
    a_seg = a_seg[0:1]
    a_seg_sw = pltpu.roll(a_seg, SSM_STATE, 1)
    seg_p = jnp.where(lo, a_seg, a_seg_sw)
    seg_q = jnp.where(lo, -a_seg_sw, a_seg)

    row8 = lax.broadcasted_iota(I32, (S5_STREAMS, LANES), 0)
    init = jnp.zeros((S5_STREAMS, LANES), F32)
    prev = None
    for s in range(S5_STREAMS):
        if s % n_seg == 0:
            cur = jnp.zeros((1, LANES), F32)
        else:
            cur = cmul(prev, seg_p, seg_q) + z[s - 1:s]
            init = jnp.where(row8 == s, cur, init)
        prev = cur

    def pass2(i, carry):
        h, hs = carry
        off = pl.multiple_of(i * S5_STREAMS, S5_STREAMS)
        hprev_ref[pl.ds(off, S5_STREAMS), :] = h
        return step(h, hs, s_ref[pl.ds(off, S5_STREAMS), :], ssw_ref[pl.ds(off, S5_STREAMS), :])

    lax.fori_loop(0, n_steps, pass2, (init, pltpu.roll(init, SSM_STATE, 1)))

    m_hi, m_lo = _split_hi_lo(m_toe)
    wo_hi, wo_lo = _split_hi_lo(w_out_t)
    hp_hi, hp_lo = _split_hi_lo(hprev_ref[...])
    y = _dot(u, m_hi) + _dot(u, m_lo)
    y = y + _dot_nt(hp_hi, wo_hi) + _dot_nt(hp_hi, wo_lo) + _dot_nt(hp_lo, wo_hi)
    y_ref[...] = (y + u.astype(F32) * dsk_ref[...]).astype(y_ref.dtype)


def _s5_branch(u, B, L, a_re, a_im, log_dt, b_re, b_im, c_re, c_im, d_skip):
    G, C, P = SSM_GROUPS, SSM_GROUP, SSM_STATE
    n_seg = S5_STREAMS // B
    n_steps = L // (S5_CHUNK * n_seg)
    R = n_steps * S5_STREAMS
    up = u.reshape(B, n_seg, n_steps, S5_CHUNK, G, C).transpose(4, 2, 0, 1, 3, 5).reshape(G, R, S5_CHUNK * C)

    def dup(a):
        return jnp.concatenate([a, a], axis=-1).astype(F32)

    are2 = dup(a_re)[:, None, :]
    aim2 = dup(a_im)[:, None, :]
    ldt = log_dt.astype(F32)[:, None, None]
    btr2 = dup(jnp.swapaxes(b_re, 1, 2))
    bti2 = dup(jnp.swapaxes(b_im, 1, 2))
    cre2 = dup(c_re)
    cim2 = dup(c_im)
    dsk = jnp.tile(d_skip.astype(F32), (1, S5_CHUNK))[:, None, :]

    def per_group(shape):
        return pl.BlockSpec((None,) + shape, lambda g: (g, 0, 0))

    yp = pl.pallas_call(
        functools.partial(_s5_kernel, n_steps=n_steps, n_seg=n_seg),
        grid=(G,),
        in_specs=[per_group((R, S5_CHUNK * C)), per_group((1, LANES)), per_group((1, LANES)), per_group((1, 1)),
                  per_group((C, LANES)), per_group((C, LANES)), per_group((C, LANES)), per_group((C, LANES)),
                  per_group((1, S5_CHUNK * C))],
        out_specs=per_group((R, S5_CHUNK * C)),
        out_shape=jax.ShapeDtypeStruct((G, R, S5_CHUNK * C), BF16),
        scratch_shapes=[pltpu.VMEM((R, LANES), F32), pltpu.VMEM((R, LANES), F32), pltpu.VMEM((R, LANES), F32)],
        compiler_params=pltpu.CompilerParams(dimension_semantics=("arbitrary",), vmem_limit_bytes=VMEM_LIMIT),
        name="s5",
    )(up, are2, aim2, ldt, btr2, bti2, cre2, cim2, dsk)
    y = yp.reshape(G, n_steps, B, n_seg, S5_CHUNK, C).transpose(2, 3, 1, 4, 0, 5)
    return y.reshape(B * L, G * C)


def _dsa_kernel(qi_ref, wi_ref, q_ref, ki2_ref, k_ref, v_ref, o_ref,
                key_ref, qm_ref, wrep_ref, rel_ref, qg_ref, bias_ref, s_ref, p_ref, alpha_ref, m_ref, acc_ref,
                *, k_sel, qblk, ktile):
    qb = pl.program_id(1)
    n_tiles = lax.div(qb * qblk + qblk + ktile - 1, ktile)
    lane = lax.broadcasted_iota(I32, (qblk, LANES), 1)

    for hd in range(IDX_HEADS):
        pair = qi_ref[:, (hd // 2) * LANES:(hd // 2 + 1) * LANES]
        msk = (lane < IDX_DIM) if hd % 2 == 0 else (lane >= IDX_DIM)
        qm_ref[hd * qblk:(hd + 1) * qblk, :] = jnp.where(msk, pair, jnp.zeros_like(pair))
    for hd in range(N_HEADS):
        g, r = divmod(hd, Q_PER_KV)
        qg_ref[g, r * qblk:(r + 1) * qblk, :] = q_ref[:, hd * HEAD_DIM:(hd + 1) * HEAD_DIM]

    wi = wi_ref[...]
    for hd in range(IDX_HEADS):
        wrep_ref[hd] = jnp.broadcast_to(wi[:, IDX_DIM + hd:IDX_DIM + hd + 1], (qblk, LANES))
    n_lt = ktile // LANES

    def score_tile(t, carry):
        off = pl.multiple_of(t * ktile, ktile)
        rel_ref[...] = _dot_nt(qm_ref[...], ki2_ref[pl.ds(off, ktile), :])
        k_chunk = lax.shift_right_logical(off + lax.broadcasted_iota(I32, (SCORE_ROWS, ktile), 1), 6)
        for c in range(qblk // SCORE_ROWS):
            r0 = c * SCORE_ROWS
            sc = jnp.zeros((SCORE_ROWS, ktile), F32)
            for hd in range(IDX_HEADS):
                w = wrep_ref[hd, r0:r0 + SCORE_ROWS, :]
                rel = rel_ref[hd * qblk + r0:hd * qblk + r0 + SCORE_ROWS, :]
                sc = sc + jnp.maximum(rel, 0.0) * jnp.concatenate([w] * n_lt, axis=1)
            q_chunk = lax.shift_right_logical(
                qb * qblk + r0 + lax.broadcasted_iota(I32, (SCORE_ROWS, 1), 0), 6)
            sc = jnp.where(k_chunk <= q_chunk, sc, NEG)
            bits = lax.bitcast_convert_type(sc, I32)
            key_ref[r0:r0 + SCORE_ROWS, pl.ds(off, ktile)] = bits ^ (lax.shift_right_arithmetic(bits, 31) & 0x7FFFFFFF)
        return carry

    lax.fori_loop(0, n_tiles, score_tile, 0)

    def bit_step(bi, prefix):
        trial_u = prefix | lax.shift_left(jnp.int32(1), 31 - bi)
        trial_s = trial_u ^ INT_MIN

        def count_tile(t, acc):
            off = pl.multiple_of(t * ktile, ktile)
            c = (key_ref[:, pl.ds(off, ktile)] >= trial_s).astype(I32)
            for j in range(ktile // LANES):
                acc = acc + c[:, j * LANES:(j + 1) * LANES]
            return acc

        acc = lax.fori_loop(0, n_tiles, count_tile, jnp.zeros((qblk, LANES), I32))
        cnt = jnp.sum(acc, axis=1, keepdims=True)
        return jnp.where(cnt >= k_sel, trial_u, prefix)

    prefix = lax.fori_loop(0, 32, bit_step, jnp.zeros((qblk, 1), I32))
    thr = jnp.maximum(prefix ^ INT_MIN, KEY_HALF_NEG + 1)

    m_ref[...] = jnp.full(m_ref.shape, -jnp.inf, F32)
    acc_ref[...] = jnp.zeros(acc_ref.shape, F32)
    g_rows = Q_PER_KV * qblk

    def attend_tile(t, carry):
        off = pl.multiple_of(t * ktile, ktile)
        bias_ref[...] = jnp.where(key_ref[:, pl.ds(off, ktile)] >= thr, 0.0, NEG)
        for g in range(N_KV_HEADS):
            s_ref[...] = _dot_nt(qg_ref[g], k_ref[pl.ds(off, ktile), g * HEAD_DIM:(g + 1) * HEAD_DIM])
            for c in range(g_rows // ATT_ROWS):
                r0 = c * ATT_ROWS
                b0 = r0 % qblk
                s = s_ref[r0:r0 + ATT_ROWS, :] + bias_ref[b0:b0 + ATT_ROWS, :]
                m_cur = s[:, 0:LANES]
                for j in range(1, n_lt):
                    m_cur = jnp.maximum(m_cur, s[:, j * LANES:(j + 1) * LANES])
                m_old = m_ref[g, r0:r0 + ATT_ROWS, :]
                m_new = jnp.maximum(m_old, jnp.max(m_cur, axis=1, keepdims=True))
                alpha_ref[r0:r0 + ATT_ROWS, :] = jnp.exp(m_old - m_new)
                m_ref[g, r0:r0 + ATT_ROWS, :] = m_new
                p_ref[r0:r0 + ATT_ROWS, :] = jnp.exp(s - jnp.concatenate([m_new] * n_lt, axis=1)).astype(BF16)
            alpha = alpha_ref[...]
            pv = _dot(p_ref[...], v_ref[pl.ds(off, ktile), 2 * g * HEAD_DIM:(2 * g + 2) * HEAD_DIM])
            acc_ref[g] = acc_ref[g] * jnp.concatenate([alpha, alpha], axis=1) + pv
        return carry

    lax.fori_loop(0, n_tiles, attend_tile, 0)
    for hd in range(N_HEADS):
        g, r = divmod(hd, Q_PER_KV)
        a = acc_ref[g, r * qblk:(r + 1) * qblk, :]
        o_ref[:, hd * HEAD_DIM:(hd + 1) * HEAD_DIM] = (a[:, :HEAD_DIM] / a[:, HEAD_DIM:]).astype(o_ref.dtype)


def _dsa_branch(q, k, v, qi, ki2, wi, B, L, qblk=128, ktile=512):
    k_sel = min(TOPK_MAX, L // 4)
    nq = L // qblk
    g_rows = Q_PER_KV * qblk
    assert L % ktile == 0

    def qrow(w):
        return pl.BlockSpec((None, qblk, w), lambda b, i: (b, i, 0))

    def whole(w):
        return pl.BlockSpec((None, L, w), lambda b, i: (b, 0, 0))

    r3 = lambda a: a.reshape(B, L, a.shape[-1])
    out = pl.pallas_call(
        functools.partial(_dsa_kernel, k_sel=k_sel, qblk=qblk, ktile=ktile),
        grid=(B, nq),
        in_specs=[qrow(IDX_HEADS * IDX_DIM), qrow(LANES), qrow(ATT_WIDTH), whole(LANES), whole(KV_WIDTH),
                  whole(2 * KV_WIDTH)],
        out_specs=qrow(ATT_WIDTH),
        out_shape=jax.ShapeDtypeStruct((B, L, ATT_WIDTH), BF16),
        scratch_shapes=[pltpu.VMEM((qblk, L), I32),
                        pltpu.VMEM((IDX_HEADS * qblk, LANES), BF16),
                        pltpu.VMEM((IDX_HEADS, qblk, LANES), F32),
                        pltpu.VMEM((IDX_HEADS * qblk, ktile), F32),
                        pltpu.VMEM((N_KV_HEADS, g_rows, HEAD_DIM), BF16),
                        pltpu.VMEM((qblk, ktile), F32),
                        pltpu.VMEM((g_rows, ktile), F32),
                        pltpu.VMEM((g_rows, ktile), BF16),
                        pltpu.VMEM((g_rows, LANES), F32),
                        pltpu.VMEM((N_KV_HEADS, g_rows, LANES), F32),
                        pltpu.VMEM((N_KV_HEADS, g_rows, 2 * HEAD_DIM), F32)],
        compiler_params=pltpu.CompilerParams(dimension_semantics=("arbitrary", "arbitrary"),
                                             vmem_limit_bytes=VMEM_LIMIT),
        name="dsa",
    )(r3(qi), r3(wi), r3(q), r3(ki2), r3(k), r3(v))
    return out.reshape(B * L, ATT_WIDTH)


def _gelu_tanh(x):
    return 0.5 * x * (1.0 + jnp.tanh(math.sqrt(2.0 / math.pi) * (x + 0.044715 * (x * x * x))))


def _merge_kernel(x_ref, y_ref, yb_ref, gate_ref, wglu_ref, wa_ref, wb_ref, wo_ref, o_ref):
    y = _gelu_tanh(y_ref[...].astype(F32))
    ya = y * jax.nn.sigmoid(_dot(y.astype(BF16), wglu_ref[...]))
    gate = gate_ref[...]
    ga, gb = gate[:, :D_MODEL], gate[:, D_MODEL:]
    merged = (jax.nn.sigmoid(ga) * _dot(ya.astype(BF16), wa_ref[...])
              + jax.nn.sigmoid(gb) * _dot(yb_ref[...], wb_ref[...]))
    o_ref[...] = x_ref[...] + _dot(merged.astype(BF16), wo_ref[...])


def _merge(x2, y, yb, gate, w_glu, w_a, w_b, w_out, tile):
    T = x2.shape[0]
    ws = [w.astype(BF16) for w in (w_glu, w_a, w_b, w_out)]

    def row(w):
        return pl.BlockSpec((tile, w), lambda i: (i, 0))

    return pl.pallas_call(
        _merge_kernel,
        grid=(T // tile,),
        in_specs=[row(D_MODEL), row(D_MODEL), row(ATT_WIDTH), row(2 * D_MODEL)]
                 + [pl.BlockSpec(w.shape, lambda i: (0, 0)) for w in ws],
        out_specs=row(D_MODEL),
        out_shape=jax.ShapeDtypeStruct((T, D_MODEL), F32),
        compiler_params=pltpu.CompilerParams(dimension_semantics=("arbitrary",), vmem_limit_bytes=VMEM_LIMIT),
        name="merge",
    )(x2, y, yb, gate, *ws)


def _rms(x, g):
    return x * lax.rsqrt(jnp.mean(x * x, axis=-1, keepdims=True) + EPS) * g


def _ffn_kernel(x_ref, g2_ref, win_ref, wout_ref, gf_ref, o_ref, *, final_norm):
    x = x_ref[...]
    h = _rms(x, g2_ref[...]).astype(BF16)
    gu = _dot(h, win_ref[...])
    g, up = gu[:, :FFN_HIDDEN], gu[:, FFN_HIDDEN:]
    act = (g * jax.nn.sigmoid(g)) * up
    x = x + _dot(act.astype(BF16), wout_ref[...])
    o_ref[...] = _rms(x, gf_ref[...]) if final_norm else x


def _ffn(x1, g2, w_ffn_in, w_ffn_out, gf, final_norm, tile):
    T = x1.shape[0]
    win, wout = w_ffn_in.astype(BF16), w_ffn_out.astype(BF16)

    def row(w):
        return pl.BlockSpec((tile, w), lambda i: (i, 0))

    def full(a):
        return pl.BlockSpec(a.shape, lambda i: (0, 0))

    return pl.pallas_call(
        functools.partial(_ffn_kernel, final_norm=final_norm),
        grid=(T // tile,),
        in_specs=[row(D_MODEL), full(g2), full(win), full(wout), full(gf)],
        out_specs=row(D_MODEL),
        out_shape=jax.ShapeDtypeStruct((T, D_MODEL), F32),
        compiler_params=pltpu.CompilerParams(dimension_semantics=("arbitrary",), vmem_limit_bytes=VMEM_LIMIT),
        name="ffn",
    )(x1, g2, win, wout, gf)


def kernel(x, norm1_g, w_in, a_re, a_im, log_dt, b_re, b_im, c_re, c_im, d_skip, w_glu,
           w_branch_a, w_branch_b, w_out, norm2_g, w_ffn_in, w_ffn_out, norm_f_g):
    B, L, D = x.shape
    depth = norm1_g.shape[0]
    assert D == D_MODEL and S5_STREAMS % B == 0 and L % 512 == 0
    tile = 256
    x2 = x.reshape(B * L, D).astype(F32)
    for i in range(depth):
        u, q, k, v, qi, ki2, wi, gate = _project(x2, norm1_g[i][None, :].astype(F32), w_in[i], L, tile)
        y = _s5_branch(u, B, L, a_re[i], a_im[i], log_dt[i], b_re[i], b_im[i], c_re[i], c_im[i], d_skip[i])
        yb = _dsa_branch(q, k, v, qi, ki2, wi, B, L)
        x1 = _merge(x2, y, yb, gate, w_glu[i], w_branch_a[i], w_branch_b[i], w_out[i], tile)
        x2 = _ffn(x1, norm2_g[i][None, :].astype(F32), w_ffn_in[i], w_ffn_out[i],
                  norm_f_g[None, :].astype(F32), i == depth - 1, tile)
    return x2.reshape(B, L, D)
```

```python
import functools
import math

import jax
import jax.numpy as jnp
import numpy as np
from jax import lax
from jax.experimental import pallas as pl
from jax.experimental.pallas import tpu as pltpu

F32 = jnp.float32
BF16 = jnp.bfloat16
I32 = jnp.int32

D_MODEL = 1024
CHUNK = 64
EPS = 1e-6
NEG = -1e30

SSM_GROUP = 16
SSM_GROUPS = 64
SSM_STATE = 64
S5_CHUNK = 16
S5_STREAMS = 8

N_HEADS = 8
HEAD_DIM = 128
N_KV_HEADS = 2
Q_PER_KV = N_HEADS // N_KV_HEADS
ATT_WIDTH = N_HEADS * HEAD_DIM
KV_WIDTH = N_KV_HEADS * HEAD_DIM
IDX_HEADS = 16
IDX_DIM = 64
TOPK_MAX = 256
ROPE_THETA = 500000.0
ATT_ROT = HEAD_DIM // 4
IDX_ROT = IDX_DIM // 4
FFN_HIDDEN = -(-8 * D_MODEL // (3 * 256)) * 256

SCORE_ROWS = 16
COUNT_ROWS = 32
ATT_ROWS = 32

LANES = 128
INT_MIN = -(2 ** 31)

VMEM_LIMIT = 56 * 1024 * 1024


def _key_of_float(val):
    bits = int(np.float32(val).view(np.int32))
    return bits ^ ((bits >> 31) & 0x7FFFFFFF)


KEY_HALF_NEG = _key_of_float(0.5 * NEG)


def _dot(a, b):
    return jnp.dot(a, b, preferred_element_type=F32)


def _dot_nt(a, b):
    return lax.dot_general(a, b, (((1,), (1,)), ((), ())), preferred_element_type=F32)


def _split_hi_lo(a):
    hi = a.astype(BF16)
    lo = (a - hi.astype(F32)).astype(BF16)
    return hi, lo


def _rope(x, c, s1, s2, half):
    n = x.shape[-1]
    return x * c + pltpu.roll(x, half, 1) * s1 + pltpu.roll(x, n - half, 1) * s2


def _proj_kernel(x_ref, g1_ref, wu_ref, wq_ref, wk_ref, wv_ref, wqi_ref, wsm_ref, wg_ref,
                 ca_ref, sa1_ref, sa2_ref, ci_ref, si1_ref, si2_ref, cs_ref, ss1_ref, ss2_ref,
                 u_ref, q_ref, k_ref, v_ref, qi_ref, ki2_ref, wi_ref, gate_ref):
    x = x_ref[...]
    h = x * lax.rsqrt(jnp.mean(x * x, axis=-1, keepdims=True) + EPS) * g1_ref[...]
    hb = h.astype(BF16)

    u_ref[...] = _dot(hb, wu_ref[...]).astype(BF16)
    v = _dot(hb, wv_ref[...]).astype(BF16)
    ones = jnp.ones((v.shape[0], HEAD_DIM), BF16)
    for hd in range(N_KV_HEADS):
        v_ref[:, 2 * hd * HEAD_DIM:(2 * hd + 1) * HEAD_DIM] = v[:, hd * HEAD_DIM:(hd + 1) * HEAD_DIM]
        v_ref[:, (2 * hd + 1) * HEAD_DIM:(2 * hd + 2) * HEAD_DIM] = ones
    gate_ref[...] = _dot(hb, wg_ref[...])

    ca, sa1, sa2 = ca_ref[...], sa1_ref[...], sa2_ref[...]
    q = _dot(hb, wq_ref[...])
    scale = HEAD_DIM ** -0.5
    for hd in range(N_HEADS):
        sl = slice(hd * HEAD_DIM, (hd + 1) * HEAD_DIM)
        q_ref[:, sl] = (_rope(q[:, sl], ca, sa1, sa2, ATT_ROT // 2) * scale).astype(BF16)
    k = _dot(hb, wk_ref[...])
    for hd in range(N_KV_HEADS):
        sl = slice(hd * HEAD_DIM, (hd + 1) * HEAD_DIM)
        k_ref[:, sl] = _rope(k[:, sl], ca, sa1, sa2, ATT_ROT // 2).astype(BF16)

    ci, si1, si2 = ci_ref[...], si1_ref[...], si2_ref[...]
    qi = _dot(hb, wqi_ref[...])
    for pr in range(IDX_HEADS * IDX_DIM // LANES):
        sl = slice(pr * LANES, (pr + 1) * LANES)
        qi_ref[:, sl] = _rope(qi[:, sl], ci, si1, si2, IDX_ROT // 2).astype(BF16)

    sm = _dot(hb, wsm_ref[...])
    sm = _rope(sm, cs_ref[...], ss1_ref[...], ss2_ref[...], IDX_ROT // 2)
    lane = lax.broadcasted_iota(I32, sm.shape, 1)
    ki2_ref[...] = jnp.where(lane < IDX_DIM, sm, pltpu.roll(sm, IDX_DIM, 1)).astype(BF16)
    wi_ref[...] = sm * (IDX_DIM ** -0.5 * IDX_HEADS ** -0.5)


def _rope_tables(L, rot, period, active_lanes):
    half = rot // 2
    pos = jnp.arange(L, dtype=jnp.int32)
    inv_freq = ROPE_THETA ** (-jnp.arange(half, dtype=F32) / half)
    ang = pos.astype(F32)[:, None] * inv_freq[None, :]
    cos, sin = jnp.cos(ang), jnp.sin(ang)
    lane = np.arange(LANES)
    within = lane % period
    fidx = np.where(within < rot, within % half, 0)
    is_x1 = (within < half) & (lane < active_lanes)
    is_x2 = (within >= half) & (within < rot) & (lane < active_lanes)
    rot_lane = is_x1 | is_x2
    c = jnp.where(rot_lane[None, :], cos[:, fidx], 1.0)
    s1 = jnp.where(is_x2[None, :], sin[:, fidx], 0.0)
    s2 = jnp.where(is_x1[None, :], -sin[:, fidx], 0.0)
    return c.astype(F32), s1.astype(F32), s2.astype(F32)


def _project(x2, g1, w_in, L, tile):
    T = x2.shape[0]
    o = 0
    parts = []
    for s in (D_MODEL, ATT_WIDTH, KV_WIDTH, KV_WIDTH, IDX_HEADS * IDX_DIM, IDX_DIM, IDX_HEADS, D_MODEL, D_MODEL):
        parts.append(w_in[:, o:o + s])
        o += s
    wu, wq, wk, wv, wqi, wki, wwi, wga, wgb = parts
    wsm = jnp.concatenate([wki, wwi, jnp.zeros((D_MODEL, LANES - IDX_DIM - IDX_HEADS), w_in.dtype)], axis=1)
    wg = jnp.concatenate([wga, wgb], axis=1)
    ws = [w.astype(BF16) for w in (wu, wq, wk, wv, wqi, wsm, wg)]
    tabs = (_rope_tables(L, ATT_ROT, HEAD_DIM, LANES)
            + _rope_tables(L, IDX_ROT, IDX_DIM, LANES)
            + _rope_tables(L, IDX_ROT, IDX_DIM, IDX_DIM))
    nl = L // tile

    def row(w):
        return pl.BlockSpec((tile, w), lambda i: (i, 0))

    def full(a):
        return pl.BlockSpec(a.shape, lambda i: (0, 0))

    tab_spec = pl.BlockSpec((tile, LANES), lambda i: (i % nl, 0))
    out_shape = (
        jax.ShapeDtypeStruct((T, D_MODEL), BF16),
        jax.ShapeDtypeStruct((T, ATT_WIDTH), BF16),
        jax.ShapeDtypeStruct((T, KV_WIDTH), BF16),
        jax.ShapeDtypeStruct((T, 2 * KV_WIDTH), BF16),
        jax.ShapeDtypeStruct((T, IDX_HEADS * IDX_DIM), BF16),
        jax.ShapeDtypeStruct((T, LANES), BF16),
        jax.ShapeDtypeStruct((T, LANES), F32),
        jax.ShapeDtypeStruct((T, 2 * D_MODEL), F32),
    )
    return pl.pallas_call(
        _proj_kernel,
        grid=(T // tile,),
        in_specs=[row(D_MODEL), full(g1)] + [full(w) for w in ws] + [tab_spec] * 9,
        out_specs=tuple(row(s.shape[1]) for s in out_shape),
        out_shape=out_shape,
        compiler_params=pltpu.CompilerParams(dimension_semantics=("arbitrary",), vmem_limit_bytes=VMEM_LIMIT),
        name="proj",
    )(x2, g1, *ws, *tabs)


def _s5_kernel(up_ref, are_ref, aim_ref, ldt_ref, btr_ref, bti_ref, cre_ref, cim_ref, dsk_ref,
               y_ref, s_ref, ssw_ref, hprev_ref, *, n_steps, n_seg):
    lane = lax.broadcasted_iota(I32, (1, LANES), 1)
    lo = lane < SSM_STATE
    dt = jnp.exp(ldt_ref[...])
    ar, ai = are_ref[...], aim_ref[...]
    rho, th = ar * dt, ai * dt
    npow = S5_CHUNK + 1
    nn = lax.broadcasted_iota(I32, (24, LANES), 0).astype(F32)
    mag = jnp.exp(nn * rho)
    lc = mag * jnp.cos(nn * th)
    ls = mag * jnp.sin(nn * th)
    lam_pk = jnp.where(lo, lc, ls)
    lam_sw = jnp.where(lo, -ls, lc)
    num_re, num_im = lc[1:2] - 1.0, ls[1:2]
    den = ar * ar + ai * ai
    f_re = (num_re * ar + num_im * ai) / den
    f_im = (num_im * ar - num_re * ai) / den
    btr, bti = btr_ref[...], bti_ref[...]
    g_re = btr * f_re - bti * f_im
    g_im = btr * f_im + bti * f_re
    g_neg = jnp.where(lo, g_re, -g_im)
    cre, cim = cre_ref[...], cim_ref[...]
    w_pk = [cre * lam_pk[n:n + 1] + cim * lam_sw[n:n + 1] for n in range(npow)]

    w0 = jnp.concatenate(w_pk[0:S5_CHUNK], axis=0)
    kt_row = lax.dot_general(g_neg, w0, (((1,), (1,)), ((), ())), preferred_element_type=F32,
                             precision=lax.Precision.HIGHEST)
    lane_m = lax.broadcasted_iota(I32, (SSM_GROUP, S5_CHUNK * SSM_GROUP), 1)
    blocks = []
    for s in range(S5_CHUNK):
        r = kt_row if s == 0 else pltpu.roll(kt_row, SSM_GROUP * s, 1)
        blocks.append(jnp.where(lane_m >= SSM_GROUP * s, r, 0.0))
    m_toe = jnp.concatenate(blocks, axis=0)

    w_state = jnp.concatenate(
        [g_re * lam_pk[S5_CHUNK - 1 - ti:S5_CHUNK - ti] + g_im * lam_sw[S5_CHUNK - 1 - ti:S5_CHUNK - ti]
         for ti in range(S5_CHUNK)], axis=0)
    w_out_t = jnp.concatenate([jnp.where(lo, w_pk[ti + 1], -w_pk[ti + 1]) for ti in range(S5_CHUNK)], axis=0)

    u = up_ref[...]
    ws_hi, ws_lo = _split_hi_lo(w_state)
    s_all = _dot(u, ws_hi) + _dot(u, ws_lo)
    s_ref[...] = s_all
    ssw_ref[...] = pltpu.roll(s_all, SSM_STATE, 1)

    a_p = jnp.where(lo, lam_pk[S5_CHUNK:S5_CHUNK + 1], lam_sw[S5_CHUNK:S5_CHUNK + 1])
    a_q = jnp.where(lo, lam_sw[S5_CHUNK:S5_CHUNK + 1], lam_pk[S5_CHUNK:S5_CHUNK + 1])

    def cmul(h, p, q):
        return h * p + pltpu.roll(h, SSM_STATE, 1) * q

    def step(h, hs, s, ssw):
        return h * a_p + hs * a_q + s, hs * a_p - h * a_q + ssw

    lane8 = lax.broadcasted_iota(I32, (S5_STREAMS, LANES), 1)
    one_pk = jnp.where(lane8 < SSM_STATE, 1.0, 0.0).astype(F32)
    zero8 = jnp.zeros((S5_STREAMS, LANES), F32)

    def pass1(i, carry):
        h, hs, p, ps = carry
        off = pl.multiple_of(i * S5_STREAMS, S5_STREAMS)
        h, hs = step(h, hs, s_ref[pl.ds(off, S5_STREAMS), :], ssw_ref[pl.ds(off, S5_STREAMS), :])
        p, ps = step(p, ps, 0.0, 0.0)
        return h, hs, p, ps

    z, _, a_seg, _ = lax.fori_loop(0, n_steps, pass1, (zero8, zero8, one_pk, 1.0 - one_pk))
    a_seg = a_seg[0:1]
    a_seg_sw = pltpu.roll(a_seg, SSM_STATE, 1)
    seg_p = jnp.where(lo, a_seg, a_seg_sw)
    seg_q = jnp.where(lo, -a_seg_sw, a_seg)

    row8 = lax.broadcasted_iota(I32, (S5_STREAMS, LANES), 0)
    init = jnp.zeros((S5_STREAMS, LANES), F32)
    prev = None
    for s in range(S5_STREAMS):
        if s % n_seg == 0:
            cur = jnp.zeros((1, LANES), F32)
        else:
            cur = cmul(prev, seg_p, seg_q) + z[s - 1:s]
            init = jnp.where(row8 == s, cur, init)
        prev = cur

    def pass2(i, carry):
        h, hs = carry
        off = pl.multiple_of(i * S5_STREAMS, S5_STREAMS)
        hprev_ref[pl.ds(off, S5_STREAMS), :] = h
        return step(h, hs, s_ref[pl.ds(off, S5_STREAMS), :], ssw_ref[pl.ds(off, S5_STREAMS), :])

    lax.fori_loop(0, n_steps, pass2, (init, pltpu.roll(init, SSM_STATE, 1)))

    m_hi, m_lo = _split_hi_lo(m_toe)
    wo_hi, wo_lo = _split_hi_lo(w_out_t)
    hp_hi, hp_lo = _split_hi_lo(hprev_ref[...])
    y = _dot(u, m_hi) + _dot(u, m_lo)
    y = y + _dot_nt(hp_hi, wo_hi) + _dot_nt(hp_hi, wo_lo) + _dot_nt(hp_lo, wo_hi)
    y_ref[...] = (y + u.astype(F32) * dsk_ref[...]).astype(y_ref.dtype)


def _s5_branch(u, B, L, a_re, a_im, log_dt, b_re, b_im, c_re, c_im, d_skip):
    G, C, P = SSM_GROUPS, SSM_GROUP, SSM_STATE
    n_seg = S5_STREAMS // B
    n_steps = L // (S5_CHUNK * n_seg)
    R = n_steps * S5_STREAMS
    up = u.reshape(B, n_seg, n_steps, S5_CHUNK, G, C).transpose(4, 2, 0, 1, 3, 5).reshape(G, R, S5_CHUNK * C)

    def dup(a):
        return jnp.concatenate([a, a], axis=-1).astype(F32)

    are2 = dup(a_re)[:, None, :]
    aim2 = dup(a_im)[:, None, :]
    ldt = log_dt.astype(F32)[:, None, None]
    btr2 = dup(jnp.swapaxes(b_re, 1, 2))
    bti2 = dup(jnp.swapaxes(b_im, 1, 2))
    cre2 = dup(c_re)
    cim2 = dup(c_im)
    dsk = jnp.tile(d_skip.astype(F32), (1, S5_CHUNK))[:, None, :]

    def per_group(shape):
        return pl.BlockSpec((None,) + shape, lambda g: (g, 0, 0))

    yp = pl.pallas_call(
        functools.partial(_s5_kernel, n_steps=n_steps, n_seg=n_seg),
        grid=(G,),
        in_specs=[per_group((R, S5_CHUNK * C)), per_group((1, LANES)), per_group((1, LANES)), per_group((1, 1)),
                  per_group((C, LANES)), per_group((C, LANES)), per_group((C, LANES)), per_group((C, LANES)),
                  per_group((1, S5_CHUNK * C))],
        out_specs=per_group((R, S5_CHUNK * C)),
        out_shape=jax.ShapeDtypeStruct((G, R, S5_CHUNK * C), BF16),
        scratch_shapes=[pltpu.VMEM((R, LANES), F32), pltpu.VMEM((R, LANES), F32), pltpu.VMEM((R, LANES), F32)],
        compiler_params=pltpu.CompilerParams(dimension_semantics=("arbitrary",), vmem_limit_bytes=VMEM_LIMIT),
        name="s5",
    )(up, are2, aim2, ldt, btr2, bti2, cre2, cim2, dsk)
    y = yp.reshape(G, n_steps, B, n_seg, S5_CHUNK, C).transpose(2, 3, 1, 4, 0, 5)
    return y.reshape(B * L, G * C)


def _float_to_key(x):
    bits = lax.bitcast_convert_type(x, I32)
    return bits ^ (lax.shift_right_arithmetic(bits, 31) & 0x7FFFFFFF)


def _key_to_float(key):
    return lax.bitcast_convert_type(key ^ (lax.shift_right_arithmetic(key, 31) & 0x7FFFFFFF), F32)


def _probit(p):
    t = jnp.sqrt(-2.0 * jnp.log(jnp.minimum(p, 1.0 - p)))
    z = t - ((0.010328 * t + 0.802853) * t + 2.515517) / (((0.001308 * t + 0.189269) * t + 1.432788) * t + 1.0)
    return jnp.where(p < 0.5, -z, z)


def _count_ge(sc_ref, trial, n_tiles, ktile):
    def count_tile(t, acc):
        off = pl.multiple_of(t * ktile, ktile)
        c = (sc_ref[pl.ds(off, ktile), :] >= trial).astype(I32)
        return acc + jnp.sum(c.reshape(ktile // COUNT_ROWS, COUNT_ROWS, LANES), axis=0)

    acc = lax.fori_loop(0, n_tiles, count_tile, jnp.zeros((COUNT_ROWS, LANES), I32))
    return jnp.sum(acc, axis=0, keepdims=True).astype(F32)


def _select_threshold(sc_ref, s_min, s_max, n_allowed, n_tiles, k_sel, ktile):
    kf = float(k_sel)
    z_t = _probit(1.0 - (kf - 0.5) / n_allowed)

    def z_of(c):
        return _probit(jnp.clip(1.0 - c / n_allowed, 0.5 / n_allowed, 1.0 - 0.5 / n_allowed))

    def body(st):
        it, t_lo, t_hi, c_lo, c_hi, z_lo, z_hi, w_lo, w_hi, side, done, thr = st
        g_lo = (z_lo - z_t) * w_lo
        g_hi = (z_hi - z_t) * w_hi
        t = t_lo + (t_hi - t_lo) * jnp.clip(g_lo / (g_lo - g_hi), 0.0, 1.0)
        k_lo, k_hi, k_t = _float_to_key(t_lo), _float_to_key(t_hi), _float_to_key(t)
        mid = lax.shift_right_arithmetic(k_lo, 1) + lax.shift_right_arithmetic(k_hi, 1) + (k_lo & k_hi & 1)
        use_mid = (k_t <= k_lo) | (k_t >= k_hi) | (c_lo - c_hi <= 4.0) | (it >= 20)
        t = _key_to_float(jnp.where(use_mid, mid, k_t))
        c = _count_ge(sc_ref, t, n_tiles, ktile)
        active = done == 0
        hit = active & (c == kf)
        new_lo = active & (c >= kf)
        new_hi = active & (c < kf)
        z_c = z_of(c)
        w_hi = jnp.where(new_lo, jnp.where(side > 0, 0.5 * w_hi, w_hi), 1.0)
        w_lo = jnp.where(new_hi, jnp.where(side < 0, 0.5 * w_lo, w_lo), 1.0)
        side = jnp.where(new_lo, 1.0, jnp.where(new_hi, -1.0, side))
        t_lo = jnp.where(new_lo, t, t_lo)
        c_lo = jnp.where(new_lo, c, c_lo)
        z_lo = jnp.where(new_lo, z_c, z_lo)
        t_hi = jnp.where(new_hi, t, t_hi)
        c_hi = jnp.where(new_hi, c, c_hi)
        z_hi = jnp.where(new_hi, z_c, z_hi)
        adjacent = active & (_float_to_key(t_hi) - 1 <= _float_to_key(t_lo))
        thr = jnp.where(hit, t, jnp.where(adjacent, t_lo, thr))
        done = jnp.where(hit | adjacent, 1, done)
        return it + 1, t_lo, t_hi, c_lo, c_hi, z_lo, z_hi, w_lo, w_hi, side, done, thr

    def cond(st):
        return jnp.logical_and(st[0] < 64, jnp.min(st[10]) < 1)

    few = n_allowed <= kf
    ones = jnp.ones((1, LANES), F32)
    zeros = jnp.zeros((1, LANES), F32)
    init = (jnp.int32(0), s_min, _key_to_float(_float_to_key(s_max) + 1), n_allowed, zeros,
            z_of(n_allowed), z_of(zeros), ones, ones, zeros,
            few.astype(I32), jnp.where(few, -jnp.inf, s_min))
    return lax.while_loop(cond, body, init)[11]


def _dsa_kernel(qi_ref, wi_ref, q_ref, ki2_ref, k_ref, v_ref, o_ref,
                sc_ref, qm_ref, qg_ref, bias_ref, m_ref, acc_ref, *, k_sel, qblk, ktile):
    assert qblk == LANES
    qb = pl.program_id(1)
    n_tiles = lax.div(qb * qblk + qblk + ktile - 1, ktile)
    lane = lax.broadcasted_iota(I32, (qblk, LANES), 1)
    n_lt = ktile // LANES

    for hd in range(IDX_HEADS):
        pair = qi_ref[:, (hd // 2) * LANES:(hd // 2 + 1) * LANES]
        msk = (lane < IDX_DIM) if hd % 2 == 0 else (lane >= IDX_DIM)
        qm_ref[hd * qblk:(hd + 1) * qblk, :] = jnp.where(msk, pair, jnp.zeros_like(pair))
    for hd in range(N_HEADS):
        g, r = divmod(hd, Q_PER_KV)
        qg_ref[g, r * qblk:(r + 1) * qblk, :] = q_ref[:, hd * HEAD_DIM:(hd + 1) * HEAD_DIM]

    w_t = wi_ref[...].T
    q_chunk = lax.shift_right_logical(qb * qblk + lax.broadcasted_iota(I32, (1, LANES), 1), 6)
    n_allowed = ((q_chunk + 1) * CHUNK).astype(F32)

    def score_tile(t, carry):
        mx, mn = carry
        off = pl.multiple_of(t * ktile, ktile)
        rel_all = _dot_nt(ki2_ref[pl.ds(off, ktile), :], qm_ref[...])
        for c in range(ktile // SCORE_ROWS):
            r0 = c * SCORE_ROWS
            sc = jnp.zeros((SCORE_ROWS, LANES), F32)
            for hd in range(IDX_HEADS):
                rel = rel_all[r0:r0 + SCORE_ROWS, hd * qblk:(hd + 1) * qblk]
                sc = sc + jnp.maximum(rel, 0.0) * w_t[IDX_DIM + hd:IDX_DIM + hd + 1, :]
            k_chunk = lax.shift_right_logical(off + r0 + lax.broadcasted_iota(I32, (SCORE_ROWS, 1), 0), 6)
            ok = k_chunk <= q_chunk
            sc_ref[pl.ds(off + r0, SCORE_ROWS), :] = jnp.where(ok, sc, NEG)
            mx = jnp.maximum(mx, jnp.where(ok, sc, -jnp.inf))
            mn = jnp.minimum(mn, jnp.where(ok, sc, jnp.inf))
        return mx, mn

    mx, mn = lax.fori_loop(0, n_tiles, score_tile,
                           (jnp.full((SCORE_ROWS, LANES), -jnp.inf, F32), jnp.full((SCORE_ROWS, LANES), jnp.inf, F32)))
    s_max = jnp.max(mx, axis=0, keepdims=True)
    s_min = jnp.min(mn, axis=0, keepdims=True)

    thr = _select_threshold(sc_ref, s_min, s_max, n_allowed, n_tiles, k_sel, ktile)
    thr = jnp.maximum(thr, _key_to_float(jnp.full((1, LANES), KEY_HALF_NEG + 1, I32)))

    m_ref[...] = jnp.full(m_ref.shape, -jnp.inf, F32)
    acc_ref[...] = jnp.zeros(acc_ref.shape, F32)
    g_rows = Q_PER_KV * qblk

    def attend_tile(t, carry):
        off = pl.multiple_of(t * ktile, ktile)
        bias_ref[...] = jnp.where(sc_ref[pl.ds(off, ktile), :] >= thr, 0.0, NEG).T
        for g in range(N_KV_HEADS):
            s_all = _dot_nt(qg_ref[g], k_ref[pl.ds(off, ktile), g * HEAD_DIM:(g + 1) * HEAD_DIM])
            ps, alphas = [], []
            for c in range(g_rows // ATT_ROWS):
                r0 = c * ATT_ROWS
                b0 = r0 % qblk
                s = s_all[r0:r0 + ATT_ROWS, :] + bias_ref[b0:b0 + ATT_ROWS, :]
                m_cur = s[:, 0:LANES]
                for j in range(1, n_lt):
                    m_cur = jnp.maximum(m_cur, s[:, j * LANES:(j + 1) * LANES])
                m_old = m_ref[g, r0:r0 + ATT_ROWS, :]
                m_new = jnp.maximum(m_old, jnp.max(m_cur, axis=1, keepdims=True))
                alphas.append(jnp.exp(m_old - m_new))
                m_ref[g, r0:r0 + ATT_ROWS, :] = m_new
                ps.append(jnp.exp(s - jnp.concatenate([m_new] * n_lt, axis=1)).astype(BF16))
            alpha = jnp.concatenate(alphas, axis=0)
            pv = _dot(jnp.concatenate(ps, axis=0), v_ref[pl.ds(off, ktile), 2 * g * HEAD_DIM:(2 * g + 2) * HEAD_DIM])
            acc_ref[g] = acc_ref[g] * jnp.concatenate([alpha, alpha], axis=1) + pv
        return carry

    lax.fori_loop(0, n_tiles, attend_tile, 0)
    for hd in range(N_HEADS):
        g, r = divmod(hd, Q_PER_KV)
        a = acc_ref[g, r * qblk:(r + 1) * qblk, :]
        o_ref[:, hd * HEAD_DIM:(hd + 1) * HEAD_DIM] = (a[:, :HEAD_DIM] / a[:, HEAD_DIM:]).astype(o_ref.dtype)


def _dsa_branch(q, k, v, qi, ki2, wi, B, L, qblk=128, ktile=512):
    k_sel = min(TOPK_MAX, L // 4)
    nq = L // qblk
    g_rows = Q_PER_KV * qblk
    assert L % ktile == 0

    def qrow(w):
        return pl.BlockSpec((None, qblk, w), lambda b, i: (b, i, 0))

    def whole(w):
        return pl.BlockSpec((None, L, w), lambda b, i: (b, 0, 0))

    r3 = lambda a: a.reshape(B, L, a.shape[-1])
    out = pl.pallas_call(
        functools.partial(_dsa_kernel, k_sel=k_sel, qblk=qblk, ktile=ktile),
        grid=(B, nq),
        in_specs=[qrow(IDX_HEADS * IDX_DIM), qrow(LANES), qrow(ATT_WIDTH), whole(LANES), whole(KV_WIDTH),
                  whole(2 * KV_WIDTH)],
        out_specs=qrow(ATT_WIDTH),
        out_shape=jax.ShapeDtypeStruct((B, L, ATT_WIDTH), BF16),
        scratch_shapes=[pltpu.VMEM((L, qblk), F32),
                        pltpu.VMEM((IDX_HEADS * qblk, LANES), BF16),
                        pltpu.VMEM((N_KV_HEADS, g_rows, HEAD_DIM), BF16),
                        pltpu.VMEM((qblk, ktile), F32),
                        pltpu.VMEM((N_KV_HEADS, g_rows, LANES), F32),
                        pltpu.VMEM((N_KV_HEADS, g_rows, 2 * HEAD_DIM), F32)],
        compiler_params=pltpu.CompilerParams(dimension_semantics=("arbitrary", "arbitrary"),
                                             vmem_limit_bytes=VMEM_LIMIT),
        name="dsa",
    )(r3(qi), r3(wi), r3(q), r3(ki2), r3(k), r3(v))
    return out.reshape(B * L, ATT_WIDTH)


def _gelu_tanh(x):
    return 0.5 * x * (1.0 + jnp.tanh(math.sqrt(2.0 / math.pi) * (x + 0.044715 * (x * x * x))))


def _merge_kernel(x_ref, y_ref, yb_ref, gate_ref, wglu_ref, wa_ref, wb_ref, wo_ref, o_ref):
    y = _gelu_tanh(y_ref[...].astype(F32))
    ya = y * jax.nn.sigmoid(_dot(y.astype(BF16), wglu_ref[...]))
    gate = gate_ref[...]
    ga, gb = gate[:, :D_MODEL], gate[:, D_MODEL:]
    merged = (jax.nn.sigmoid(ga) * _dot(ya.astype(BF16), wa_ref[...])
              + jax.nn.sigmoid(gb) * _dot(yb_ref[...], wb_ref[...]))
    o_ref[...] = x_ref[...] + _dot(merged.astype(BF16), wo_ref[...])


def _merge(x2, y, yb, gate, w_glu, w_a, w_b, w_out, tile):
    T = x2.shape[0]
    ws = [w.astype(BF16) for w in (w_glu, w_a, w_b, w_out)]

    def row(w):
        return pl.BlockSpec((tile, w), lambda i: (i, 0))

    return pl.pallas_call(
        _merge_kernel,
        grid=(T // tile,),
        in_specs=[row(D_MODEL), row(D_MODEL), row(ATT_WIDTH), row(2 * D_MODEL)]
                 + [pl.BlockSpec(w.shape, lambda i: (0, 0)) for w in ws],
        out_specs=row(D_MODEL),
        out_shape=jax.ShapeDtypeStruct((T, D_MODEL), F32),
        compiler_params=pltpu.CompilerParams(dimension_semantics=("arbitrary",), vmem_limit_bytes=VMEM_LIMIT),
        name="merge",
    )(x2, y, yb, gate, *ws)


def _rms(x, g):
    return x * lax.rsqrt(jnp.mean(x * x, axis=-1, keepdims=True) + EPS) * g


def _ffn_kernel(x_ref, g2_ref, win_ref, wout_ref, gf_ref, o_ref, *, final_norm):
    x = x_ref[...]
    h = _rms(x, g2_ref[...]).astype(BF16)
    gu = _dot(h, win_ref[...])
    g, up = gu[:, :FFN_HIDDEN], gu[:, FFN_HIDDEN:]
    act = (g * jax.nn.sigmoid(g)) * up
    x = x + _dot(act.astype(BF16), wout_ref[...])
    o_ref[...] = _rms(x, gf_ref[...]) if final_norm else x


def _ffn(x1, g2, w_ffn_in, w_ffn_out, gf, final_norm, tile):
    T = x1.shape[0]
    win, wout = w_ffn_in.astype(BF16), w_ffn_out.astype(BF16)

    def row(w):
        return pl.BlockSpec((tile, w), lambda i: (i, 0))

    def full(a):
        return pl.BlockSpec(a.shape, lambda i: (0, 0))

    return pl.pallas_call(
        functools.partial(_ffn_kernel, final_norm=final_norm),
        grid=(T // tile,),
        in_specs=[row(D_MODEL), full(g2), full(win), full(wout), full(gf)],
        out_specs=row(D_MODEL),
        out_shape=jax.ShapeDtypeStruct((T, D_MODEL), F32),
        compiler_params=pltpu.CompilerParams(dimension_semantics=("arbitrary",), vmem_limit_bytes=VMEM_LIMIT),
        name="ffn",
    )(x1, g2, win, wout, gf)


def kernel(x, norm1_g, w_in, a_re, a_im, log_dt, b_re, b_im, c_re, c_im, d_skip, w_glu,
           w_branch_a, w_branch_b, w_out, norm2_g, w_ffn_in, w_ffn_out, norm_f_g):
    B, L, D = x.shape
    depth = norm1_g.shape[0]
    assert D == D_MODEL and S5_STREAMS % B == 0 and L % 512 == 0
    tile = 256
    x2 = x.reshape(B * L, D).astype(F32)
    for i in range(depth):
        u, q, k, v, qi, ki2, wi, gate = _project(x2, norm1_g[i][None, :].astype(F32), w_in[i], L, tile)
        y = _s5_branch(u, B, L, a_re[i], a_im[i], log_dt[i], b_re[i], b_im[i], c_re[i], c_im[i], d_skip[i])
        yb = _dsa_branch(q, k, v, qi, ki2, wi, B, L)
        x1 = _merge(x2, y, yb, gate, w_glu[i], w_branch_a[i], w_branch_b[i], w_out[i], tile)
        x2 = _ffn(x1, norm2_g[i][None, :].astype(F32), w_ffn_in[i], w_ffn_out[i],
                  norm_f_g[None, :].astype(F32), i == depth - 1, tile)
    return x2.reshape(B, L, D)
```

```python
import functools
import math

import jax
import jax.numpy as jnp
import numpy as np
from jax import lax
from jax.experimental import pallas as pl
from jax.experimental.pallas import tpu as pltpu

F32 = jnp.float32
BF16 = jnp.bfloat16
I32 = jnp.int32

D_MODEL = 1024
CHUNK = 64
EPS = 1e-6
NEG = -1e30

SSM_GROUP = 16
SSM_GROUPS = 64
SSM_STATE = 64
S5_CHUNK = 16
S5_STREAMS = 16
S5_LANE_GROUPS = 8
ROW_TILE = S5_CHUNK

N_HEADS = 8
HEAD_DIM = 128
N_KV_HEADS = 2
Q_PER_KV = N_HEADS // N_KV_HEADS
ATT_WIDTH = N_HEADS * HEAD_DIM
KV_WIDTH = N_KV_HEADS * HEAD_DIM
IDX_HEADS = 16
IDX_DIM = 64
TOPK_MAX = 256
ROPE_THETA = 500000.0
ATT_ROT = HEAD_DIM // 4
IDX_ROT = IDX_DIM // 4
FFN_HIDDEN = -(-8 * D_MODEL // (3 * 256)) * 256

SCORE_ROWS = 16
COUNT_ROWS = 32
ATT_ROWS = 32

LANES = 128
INT_MIN = -(2 ** 31)

VMEM_LIMIT = 56 * 1024 * 1024


def _key_of_float(val):
    bits = int(np.float32(val).view(np.int32))
    return bits ^ ((bits >> 31) & 0x7FFFFFFF)


KEY_HALF_NEG = _key_of_float(0.5 * NEG)


def _dot(a, b):
    return jnp.dot(a, b, preferred_element_type=F32)


def _dot_nt(a, b):
    return lax.dot_general(a, b, (((1,), (1,)), ((), ())), preferred_element_type=F32)


def _split_hi_lo(a):
    hi = a.astype(BF16)
    lo = (a - hi.astype(F32)).astype(BF16)
    return hi, lo


def _rope(x, c, s1, s2, half):
    n = x.shape[-1]
    return x * c + pltpu.roll(x, half, 1) * s1 + pltpu.roll(x, n - half, 1) * s2


def _store_lane_blocks(scr, val):
    for j in range(scr.shape[0]):
        scr[j] = val[:, j * LANES:(j + 1) * LANES]


def _strided_rows(scr, start, size, stride):
    return jnp.concatenate([scr[j, pl.ds(start, size, stride=stride), :] for j in range(scr.shape[0])], axis=1)


def _proj_kernel(x_ref, g1_ref, wu_ref, wq_ref, wk_ref, wv_ref, wqi_ref, wsm_ref, wg_ref,
                 ca_ref, sa1_ref, sa2_ref, ci_ref, si1_ref, si2_ref, cs_ref, ss1_ref, ss2_ref,
                 u_ref, q_ref, k_ref, v_ref, qi_ref, ki2_ref, wi_ref, gate_ref, u_scr, *, batch):
    n_st, rt, d = x_ref.shape
    rows = n_st * rt
    x = x_ref[...].reshape(rows, d)
    h = x * lax.rsqrt(jnp.mean(x * x, axis=-1, keepdims=True) + EPS) * g1_ref[...]
    hb = h.astype(BF16)

    def put(ref, val, sl=slice(None)):
        ref[:, :, sl] = val.reshape(n_st, rt, val.shape[-1])

    def table(ref):
        t = ref[...].reshape(rows // batch, LANES)
        return jnp.concatenate([t] * batch, axis=0)

    _store_lane_blocks(u_scr, _dot(hb, wu_ref[...]))
    for t in range(rt):
        u_ref[t * n_st:(t + 1) * n_st, :] = _strided_rows(u_scr, t, n_st, rt).astype(BF16)

    v = _dot(hb, wv_ref[...]).astype(BF16)
    ones = jnp.ones((rows, HEAD_DIM), BF16)
    for hd in range(N_KV_HEADS):
        put(v_ref, v[:, hd * HEAD_DIM:(hd + 1) * HEAD_DIM], slice(2 * hd * HEAD_DIM, (2 * hd + 1) * HEAD_DIM))
        put(v_ref, ones, slice((2 * hd + 1) * HEAD_DIM, (2 * hd + 2) * HEAD_DIM))
    put(gate_ref, _dot(hb, wg_ref[...]))

    ca, sa1, sa2 = table(ca_ref), table(sa1_ref), table(sa2_ref)
    q = _dot(hb, wq_ref[...])
    scale = HEAD_DIM ** -0.5 * math.log2(math.e)
    for hd in range(N_HEADS):
        sl = slice(hd * HEAD_DIM, (hd + 1) * HEAD_DIM)
        put(q_ref, (_rope(q[:, sl], ca, sa1, sa2, ATT_ROT // 2) * scale).astype(BF16), sl)
    k = _dot(hb, wk_ref[...])
    for hd in range(N_KV_HEADS):
        sl = slice(hd * HEAD_DIM, (hd + 1) * HEAD_DIM)
        put(k_ref, _rope(k[:, sl], ca, sa1, sa2, ATT_ROT // 2).astype(BF16), sl)

    ci, si1, si2 = table(ci_ref), table(si1_ref), table(si2_ref)
    qi = _dot(hb, wqi_ref[...])
    for pr in range(IDX_HEADS * IDX_DIM // LANES):
        sl = slice(pr * LANES, (pr + 1) * LANES)
        put(qi_ref, _rope(qi[:, sl], ci, si1, si2, IDX_ROT // 2).astype(BF16), sl)

    sm = _dot(hb, wsm_ref[...])
    sm = _rope(sm, table(cs_ref), table(ss1_ref), table(ss2_ref), IDX_ROT // 2)
    lane = lax.broadcasted_iota(I32, sm.shape, 1)
    put(ki2_ref, jnp.where(lane < IDX_DIM, sm, pltpu.roll(sm, IDX_DIM, 1)).astype(BF16))
    put(wi_ref, sm * (IDX_DIM ** -0.5 * IDX_HEADS ** -0.5))


def _rope_tables(L, rot, period, active_lanes):
    half = rot // 2
    pos = jnp.arange(L, dtype=jnp.int32)
    inv_freq = ROPE_THETA ** (-jnp.arange(half, dtype=F32) / half)
    ang = pos.astype(F32)[:, None] * inv_freq[None, :]
    cos, sin = jnp.cos(ang), jnp.sin(ang)
    lane = np.arange(LANES)
    within = lane % period
    fidx = np.where(within < rot, within % half, 0)
    is_x1 = (within < half) & (lane < active_lanes)
    is_x2 = (within >= half) & (within < rot) & (lane < active_lanes)
    rot_lane = is_x1 | is_x2
    c = jnp.where(rot_lane[None, :], cos[:, fidx], 1.0)
    s1 = jnp.where(is_x2[None, :], sin[:, fidx], 0.0)
    s2 = jnp.where(is_x1[None, :], -sin[:, fidx], 0.0)
    return c.astype(F32), s1.astype(F32), s2.astype(F32)


def _project(x, g1, w_in):
    B, L, _ = x.shape
    n_seg = S5_STREAMS // B
    l_seg = L // n_seg
    rt = ROW_TILE
    o = 0
    parts = []
    for s in (D_MODEL, ATT_WIDTH, KV_WIDTH, KV_WIDTH, IDX_HEADS * IDX_DIM, IDX_DIM, IDX_HEADS, D_MODEL, D_MODEL):
        parts.append(w_in[:, o:o + s])
        o += s
    wu, wq, wk, wv, wqi, wki, wwi, wga, wgb = parts
    wsm = jnp.concatenate([wki, wwi, jnp.zeros((D_MODEL, LANES - IDX_DIM - IDX_HEADS), w_in.dtype)], axis=1)
    wg = jnp.concatenate([wga, wgb], axis=1)
    ws = [w.astype(BF16) for w in (wu, wq, wk, wv, wqi, wsm, wg)]
    tabs = (_rope_tables(L, ATT_ROT, HEAD_DIM, LANES)
            + _rope_tables(L, IDX_ROT, IDX_DIM, LANES)
            + _rope_tables(L, IDX_ROT, IDX_DIM, IDX_DIM))
    tabs = [t.reshape(n_seg, l_seg, LANES) for t in tabs]

    def streams(w):
        return pl.BlockSpec((S5_STREAMS, rt, w), lambda i: (0, i, 0))

    def full(a):
        return pl.BlockSpec(a.shape, lambda i: (0, 0))

    tab_spec = pl.BlockSpec((n_seg, rt, LANES), lambda i: (0, i, 0))
    widths = (ATT_WIDTH,
              KV_WIDTH,
              2 * KV_WIDTH,
              IDX_HEADS * IDX_DIM,
              LANES,
              LANES,
              2 * D_MODEL)
    dtypes = (BF16, BF16, BF16, BF16, BF16, F32, F32)
    out_shape = ((jax.ShapeDtypeStruct((l_seg // rt, rt * S5_STREAMS, D_MODEL), BF16),)
                 + tuple(jax.ShapeDtypeStruct((S5_STREAMS, l_seg, w), dt) for w, dt in zip(widths, dtypes)))
    out_specs = ((pl.BlockSpec((None, rt * S5_STREAMS, D_MODEL), lambda i: (i, 0, 0)),)
                 + tuple(streams(w) for w in widths))
    return pl.pallas_call(
        functools.partial(_proj_kernel, batch=B),
        grid=(l_seg // rt,),
        in_specs=[streams(D_MODEL), full(g1)] + [full(w) for w in ws] + [tab_spec] * 9,
        out_specs=out_specs,
        out_shape=out_shape,
        scratch_shapes=[pltpu.VMEM((D_MODEL // LANES, rt * S5_STREAMS, LANES), F32)],
        compiler_params=pltpu.CompilerParams(dimension_semantics=("arbitrary",), vmem_limit_bytes=VMEM_LIMIT),
        name="proj",
    )(x.reshape(S5_STREAMS, l_seg, D_MODEL), g1, *ws, *tabs)


def _s5_group_weights(ldt, ar, ai, btr, bti, cre, cim):
    lane = lax.broadcasted_iota(I32, (1, LANES), 1)
    lo = lane < SSM_STATE
    dt = jnp.exp(ldt)
    rho, th = ar * dt, ai * dt
    npow = S5_CHUNK + 1
    nn = lax.broadcasted_iota(I32, (24, LANES), 0).astype(F32)
    mag = jnp.exp(nn * rho)
    lc = mag * jnp.cos(nn * th)
    ls = mag * jnp.sin(nn * th)
    lam_pk = jnp.where(lo, lc, ls)
    lam_sw = jnp.where(lo, -ls, lc)
    num_re, num_im = lc[1:2] - 1.0, ls[1:2]
    den = ar * ar + ai * ai
    f_re = (num_re * ar + num_im * ai) / den
    f_im = (num_im * ar - num_re * ai) / den
    g_re = btr * f_re - bti * f_im
    g_im = btr * f_im + bti * f_re
    g_neg = jnp.where(lo, g_re, -g_im)
    w_pk = [cre * lam_pk[n:n + 1] + cim * lam_sw[n:n + 1] for n in range(npow)]
    w_state = [g_re * lam_pk[S5_CHUNK - 1 - ti:S5_CHUNK - ti] + g_im * lam_sw[S5_CHUNK - 1 - ti:S5_CHUNK - ti]
               for ti in range(S5_CHUNK)]
    a_p = jnp.where(lo, lam_pk[S5_CHUNK:S5_CHUNK + 1], lam_sw[S5_CHUNK:S5_CHUNK + 1])
    a_q = jnp.where(lo, lam_sw[S5_CHUNK:S5_CHUNK + 1], lam_pk[S5_CHUNK:S5_CHUNK + 1])
    return g_neg, w_pk, w_state, a_p, a_q


def _place(block, g):
    z = jnp.zeros_like(block)
    return jnp.concatenate([block if j == g else z for j in range(S5_LANE_GROUPS)], axis=1)


def _swap_halves(x):
    return jnp.concatenate([pltpu.roll(x[:, g * LANES:(g + 1) * LANES], SSM_STATE, 1)
                            for g in range(x.shape[1] // LANES)], axis=1)


def _s5_kernel(up_ref, are_ref, aim_ref, ldt_ref, btr_ref, bti_ref, cre_ref, cim_ref, dsk_ref,
               y_ref, s_ref, hprev_ref, kbd_ref, wst_ref, wot_ref, *, n_steps, n_seg):
    ng = S5_LANE_GROUPS
    lane = lax.broadcasted_iota(I32, (1, LANES), 1)
    lo = lane < SSM_STATE
    weights = [_s5_group_weights(ldt_ref[g], are_ref[g], aim_ref[g], btr_ref[g], bti_ref[g], cre_ref[g], cim_ref[g])
               for g in range(ng)]
    gneg_blk = jnp.concatenate([_place(weights[g][0], g) for g in range(ng)], axis=0)
    for tau in range(S5_CHUNK):
        wt = jnp.concatenate([_place(weights[g][1][tau], g) for g in range(ng)], axis=0)
        kbd_ref[:, tau * LANES:(tau + 1) * LANES] = lax.dot_general(
            gneg_blk, wt, (((1,), (1,)), ((), ())), preferred_element_type=F32, precision=lax.Precision.HIGHEST)
    for ti in range(S5_CHUNK):
        wst_ref[ti * LANES:(ti + 1) * LANES, :] = jnp.concatenate(
            [_place(weights[g][2][ti], g) for g in range(ng)], axis=0).astype(BF16)
        wot_ref[ti * LANES:(ti + 1) * LANES, :] = jnp.concatenate(
            [_place(jnp.where(lo, weights[g][1][ti + 1], -weights[g][1][ti + 1]), g) for g in range(ng)],
            axis=0).astype(BF16)
    a_p = jnp.concatenate([weights[g][3] for g in range(ng)], axis=1)
    a_q = jnp.concatenate([weights[g][4] for g in range(ng)], axis=1)

    rows = n_steps * S5_STREAMS

    def chunk_rows(ti):
        return up_ref[:, ti * S5_STREAMS:(ti + 1) * S5_STREAMS, :].reshape(rows, LANES)

    lhs = jnp.concatenate([chunk_rows(ti) for ti in range(S5_CHUNK)], axis=1)
    s_ref[...] = _dot(lhs, wst_ref[...])

    def cmul(x, p, q):
        return x * p + _swap_halves(x) * q

    def step(h, hs, i):
        s = s_ref[pl.ds(pl.multiple_of(i * S5_STREAMS, S5_STREAMS), S5_STREAMS), :]
        return h * a_p + hs * a_q + s, hs * a_p - h * a_q + _swap_halves(s)

    zero = jnp.zeros((S5_STREAMS, ng * LANES), F32)
    z, _ = lax.fori_loop(0, n_steps, lambda i, c: step(c[0], c[1], i), (zero, zero))

    lo_all = jnp.concatenate([lo] * ng, axis=1)

    def factors(zpk):
        sw = _swap_halves(zpk)
        return jnp.where(lo_all, zpk, sw), jnp.where(lo_all, -sw, zpk)

    base, seg, n = jnp.where(lo_all, a_p, a_q), None, n_steps
    while n:
        if n & 1:
            seg = base if seg is None else cmul(seg, *factors(base))
        n >>= 1
        if n:
            base = cmul(base, *factors(base))
    seg_p, seg_q = factors(seg)

    row_id = lax.broadcasted_iota(I32, (S5_STREAMS, ng * LANES), 0)
    init = zero
    prev = None
    for s in range(S5_STREAMS):
        if s % n_seg == 0:
            cur = jnp.zeros((1, ng * LANES), F32)
        else:
            cur = cmul(prev, seg_p, seg_q) + z[s - 1:s]
            init = jnp.where(row_id == s, cur, init)
        prev = cur

    def pass2(i, c):
        hprev_ref[pl.ds(pl.multiple_of(i * S5_STREAMS, S5_STREAMS), S5_STREAMS), :] = c[0]
        return step(c[0], c[1], i)

    lax.fori_loop(0, n_steps, pass2, (init, _swap_halves(init)))

    hp = hprev_ref[...].astype(BF16)
    dsk = dsk_ref[...]
    zblk = jnp.zeros((LANES, LANES), F32)
    for tp in range(S5_CHUNK // 2):
        kdim = (2 * tp + 2) * LANES
        cols = []
        for t2 in (2 * tp, 2 * tp + 1):
            cols.append(jnp.concatenate(
                [kbd_ref[:, (t2 - ti) * LANES:(t2 - ti + 1) * LANES] if ti <= t2 else zblk
                 for ti in range(2 * tp + 2)], axis=0))
        slab = jnp.concatenate(cols, axis=1).astype(BF16)
        y = _dot(lhs[:, :kdim], slab) + _dot_nt(hp, wot_ref[2 * tp * LANES:(2 * tp + 2) * LANES, :])
        for half in range(2):
            t2 = 2 * tp + half
            u_t = up_ref[:, t2 * S5_STREAMS:(t2 + 1) * S5_STREAMS, :].astype(F32)
            y_t = y[:, half * LANES:(half + 1) * LANES].reshape(n_steps, S5_STREAMS, LANES) + u_t * dsk
            y_ref[:, t2 * S5_STREAMS:(t2 + 1) * S5_STREAMS, :] = y_t.astype(y_ref.dtype)


def _s5_branch(up, B, a_re, a_im, log_dt, b_re, b_im, c_re, c_im, d_skip):
    G, C = SSM_GROUPS, SSM_GROUP
    n_steps = up.shape[0]
    n_seg = S5_STREAMS // B
    rows = n_steps * S5_STREAMS
    ng = S5_LANE_GROUPS

    def dup(a):
        return jnp.concatenate([a, a], axis=-1).astype(F32)

    are2 = dup(a_re)[:, None, :]
    aim2 = dup(a_im)[:, None, :]
    ldt = log_dt.astype(F32)[:, None, None]
    btr2 = dup(jnp.swapaxes(b_re, 1, 2))
    bti2 = dup(jnp.swapaxes(b_im, 1, 2))
    cre2 = dup(c_re)
    cim2 = dup(c_im)
    dsk = d_skip.astype(F32).reshape(G // ng, 1, ng * C)

    def per_block(shape):
        return pl.BlockSpec((ng,) + shape, lambda i: (i, 0, 0))

    act = pl.BlockSpec((n_steps, S5_CHUNK * S5_STREAMS, LANES), lambda i: (0, 0, i))
    return pl.pallas_call(
        functools.partial(_s5_kernel, n_steps=n_steps, n_seg=n_seg),
        grid=(G // ng,),
        in_specs=[act, per_block((1, LANES)), per_block((1, LANES)), per_block((1, 1)),
                  per_block((C, LANES)), per_block((C, LANES)), per_block((C, LANES)), per_block((C, LANES)),
                  pl.BlockSpec((None, 1, LANES), lambda i: (i, 0, 0))],
        out_specs=act,
        out_shape=jax.ShapeDtypeStruct(up.shape, BF16),
        scratch_shapes=[pltpu.VMEM((rows, ng * LANES), F32),
                        pltpu.VMEM((rows, ng * LANES), F32),
                        pltpu.VMEM((LANES, S5_CHUNK * LANES), F32),
                        pltpu.VMEM((S5_CHUNK * LANES, ng * LANES), BF16),
                        pltpu.VMEM((S5_CHUNK * LANES, ng * LANES), BF16)],
        compiler_params=pltpu.CompilerParams(dimension_semantics=("arbitrary",), vmem_limit_bytes=VMEM_LIMIT),
        name="s5",
    )(up, are2, aim2, ldt, btr2, bti2, cre2, cim2, dsk)


def _float_to_key(x):
    bits = lax.bitcast_convert_type(x, I32)
    return bits ^ (lax.shift_right_arithmetic(bits, 31) & 0x7FFFFFFF)


def _key_to_float(key):
    return lax.bitcast_convert_type(key ^ (lax.shift_right_arithmetic(key, 31) & 0x7FFFFFFF), F32)


def _probit(p):
    t = jnp.sqrt(-2.0 * jnp.log(jnp.minimum(p, 1.0 - p)))
    z = t - ((0.010328 * t + 0.802853) * t + 2.515517) / (((0.001308 * t + 0.189269) * t + 1.432788) * t + 1.0)
    return jnp.where(p < 0.5, -z, z)


def _count_ge(sc_ref, trial, n_tiles, ktile):
    def count_tile(t, acc):
        off = pl.multiple_of(t * ktile, ktile)
        c = (sc_ref[pl.ds(off, ktile), :] >= trial).astype(I32)
        return acc + jnp.sum(c.reshape(ktile // COUNT_ROWS, COUNT_ROWS, LANES), axis=0)

    acc = lax.fori_loop(0, n_tiles, count_tile, jnp.zeros((COUNT_ROWS, LANES), I32))
    return jnp.sum(acc, axis=0, keepdims=True).astype(F32)


def _select_threshold(sc_ref, s_min, s_max, n_allowed, n_tiles, k_sel, ktile):
    kf = float(k_sel)
    z_t = _probit(1.0 - (kf - 0.5) / n_allowed)

    def z_of(c):
        return _probit(jnp.clip(1.0 - c / n_allowed, 0.5 / n_allowed, 1.0 - 0.5 / n_allowed))

    def body(st):
        it, t_lo, t_hi, c_lo, c_hi, z_lo, z_hi, w_lo, w_hi, side, done, thr = st
        g_lo = (z_lo - z_t) * w_lo
        g_hi = (z_hi - z_t) * w_hi
        t = t_lo + (t_hi - t_lo) * jnp.clip(g_lo / (g_lo - g_hi), 0.0, 1.0)
        k_lo, k_hi, k_t = _float_to_key(t_lo), _float_to_key(t_hi), _float_to_key(t)
        mid = lax.shift_right_arithmetic(k_lo, 1) + lax.shift_right_arithmetic(k_hi, 1) + (k_lo & k_hi & 1)
        use_mid = (k_t <= k_lo) | (k_t >= k_hi) | (c_lo - c_hi <= 4.0) | (it >= 20)
        t = _key_to_float(jnp.where(use_mid, mid, k_t))
        c = _count_ge(sc_ref, t, n_tiles, ktile)
        active = done == 0
        hit = active & (c == kf)
        new_lo = active & (c >= kf)
        new_hi = active & (c < kf)
        z_c = z_of(c)
        w_hi = jnp.where(new_lo, jnp.where(side > 0, 0.5 * w_hi, w_hi), 1.0)
        w_lo = jnp.where(new_hi, jnp.where(side < 0, 0.5 * w_lo, w_lo), 1.0)
        side = jnp.where(new_lo, 1.0, jnp.where(new_hi, -1.0, side))
        t_lo = jnp.where(new_lo, t, t_lo)
        c_lo = jnp.where(new_lo, c, c_lo)
        z_lo = jnp.where(new_lo, z_c, z_lo)
        t_hi = jnp.where(new_hi, t, t_hi)
        c_hi = jnp.where(new_hi, c, c_hi)
        z_hi = jnp.where(new_hi, z_c, z_hi)
        adjacent = active & (_float_to_key(t_hi) - 1 <= _float_to_key(t_lo))
        thr = jnp.where(hit, t, jnp.where(adjacent, t_lo, thr))
        done = jnp.where(hit | adjacent, 1, done)
        return it + 1, t_lo, t_hi, c_lo, c_hi, z_lo, z_hi, w_lo, w_hi, side, done, thr

    def cond(st):
        return jnp.logical_and(st[0] < 64, jnp.min(st[10]) < 1)

    few = n_allowed <= kf
    ones = jnp.ones((1, LANES), F32)
    zeros = jnp.zeros((1, LANES), F32)
    init = (jnp.int32(0), s_min, _key_to_float(_float_to_key(s_max) + 1), n_allowed, zeros,
            z_of(n_allowed), z_of(zeros), ones, ones, zeros,
            few.astype(I32), jnp.where(few, -jnp.inf, s_min))
    return lax.while_loop(cond, body, init)[11]


def _dsa_kernel(qi_ref, wi_ref, q_ref, ki2_ref, k_ref, v_ref, o_ref,
                sc_ref, qm_ref, qg_ref, m_ref, acc_ref, *, k_sel, qblk, ktile):
    assert qblk == LANES
    qb = pl.program_id(1)
    n_tiles = lax.div(qb * qblk + qblk + ktile - 1, ktile)
    lane = lax.broadcasted_iota(I32, (qblk, LANES), 1)
    n_lt = ktile // LANES

    for hd in range(IDX_HEADS):
        pair = qi_ref[:, (hd // 2) * LANES:(hd // 2 + 1) * LANES]
        msk = (lane < IDX_DIM) if hd % 2 == 0 else (lane >= IDX_DIM)
        qm_ref[hd * qblk:(hd + 1) * qblk, :] = jnp.where(msk, pair, jnp.zeros_like(pair))
    eye = (lax.broadcasted_iota(I32, (qblk, LANES), 0) == lane).astype(BF16)
    for hd in range(N_HEADS):
        g, r = divmod(hd, Q_PER_KV)
        qg_ref[g, r * qblk:(r + 1) * qblk, 0:HEAD_DIM] = q_ref[:, hd * HEAD_DIM:(hd + 1) * HEAD_DIM]
        qg_ref[g, r * qblk:(r + 1) * qblk, HEAD_DIM:2 * HEAD_DIM] = eye

    w_t = wi_ref[...].T
    q_chunk = lax.shift_right_logical(qb * qblk + lax.broadcasted_iota(I32, (1, LANES), 1), 6)
    n_allowed = ((q_chunk + 1) * CHUNK).astype(F32)

    def score_tile(t, carry):
        mx, mn = carry
        off = pl.multiple_of(t * ktile, ktile)
        rel_all = _dot_nt(ki2_ref[pl.ds(off, ktile), :], qm_ref[...])
        for c in range(ktile // SCORE_ROWS):
            r0 = c * SCORE_ROWS
            sc = jnp.zeros((SCORE_ROWS, LANES), F32)
            for hd in range(IDX_HEADS):
                rel = rel_all[r0:r0 + SCORE_ROWS, hd * qblk:(hd + 1) * qblk]
                sc = sc + jnp.maximum(rel, 0.0) * w_t[IDX_DIM + hd:IDX_DIM + hd + 1, :]
            k_chunk = lax.shift_right_logical(off + r0 + lax.broadcasted_iota(I32, (SCORE_ROWS, 1), 0), 6)
            ok = k_chunk <= q_chunk
            sc_ref[pl.ds(off + r0, SCORE_ROWS), :] = jnp.where(ok, sc, NEG)
            mx = jnp.maximum(mx, jnp.where(ok, sc, -jnp.inf))
            mn = jnp.minimum(mn, jnp.where(ok, sc, jnp.inf))
        return mx, mn

    mx, mn = lax.fori_loop(0, n_tiles, score_tile,
                           (jnp.full((SCORE_ROWS, LANES), -jnp.inf, F32), jnp.full((SCORE_ROWS, LANES), jnp.inf, F32)))
    s_max = jnp.max(mx, axis=0, keepdims=True)
    s_min = jnp.min(mn, axis=0, keepdims=True)

    thr = _select_threshold(sc_ref, s_min, s_max, n_allowed, n_tiles, k_sel, ktile)
    thr = jnp.maximum(thr, _key_to_float(jnp.full((1, LANES), KEY_HALF_NEG + 1, I32)))

    m_ref[...] = jnp.full(m_ref.shape, -jnp.inf, F32)
    acc_ref[...] = jnp.zeros(acc_ref.shape, F32)
    g_rows = Q_PER_KV * qblk

    def attend_tile(t, carry):
        off = pl.multiple_of(t * ktile, ktile)
        bias_t = jnp.where(sc_ref[pl.ds(off, ktile), :] >= thr, 0.0, NEG).astype(BF16)
        logits = []
        for g in range(N_KV_HEADS):
            kb = jnp.concatenate([k_ref[pl.ds(off, ktile), g * HEAD_DIM:(g + 1) * HEAD_DIM], bias_t], axis=1)
            logits.append(_dot_nt(qg_ref[g], kb))
        for g in range(N_KV_HEADS):
            s_all = logits[g]
            ps, alphas = [], []
            for c in range(g_rows // ATT_ROWS):
                r0 = c * ATT_ROWS
                s = s_all[r0:r0 + ATT_ROWS, :]
                m_cur = s[:, 0:LANES]
                for j in range(1, n_lt):
                    m_cur = jnp.maximum(m_cur, s[:, j * LANES:(j + 1) * LANES])
                m_old = m_ref[g, r0:r0 + ATT_ROWS, :]
                m_new = jnp.maximum(m_old, jnp.max(m_cur, axis=1, keepdims=True))
                alphas.append(jnp.exp2(m_old - m_new))
                m_ref[g, r0:r0 + ATT_ROWS, :] = m_new
                ps.append(jnp.exp2(s - jnp.concatenate([m_new] * n_lt, axis=1)).astype(BF16))
            alpha = jnp.concatenate(alphas, axis=0)
            pv = _dot(jnp.concatenate(ps, axis=0), v_ref[pl.ds(off, ktile), 2 * g * HEAD_DIM:(2 * g + 2) * HEAD_DIM])
            acc_ref[g] = acc_ref[g] * jnp.concatenate([alpha, alpha], axis=1) + pv
        return carry

    lax.fori_loop(0, n_tiles, attend_tile, 0)
    for hd in range(N_HEADS):
        g, r = divmod(hd, Q_PER_KV)
        a = acc_ref[g, r * qblk:(r + 1) * qblk, :]
        o_ref[:, hd * HEAD_DIM:(hd + 1) * HEAD_DIM] = (a[:, :HEAD_DIM] / a[:, HEAD_DIM:]).astype(o_ref.dtype)


def _dsa_branch(q, k, v, qi, ki2, wi, B, L, qblk=128, ktile=512):
    k_sel = min(TOPK_MAX, L // 4)
    nq = L // qblk
    g_rows = Q_PER_KV * qblk
    assert L % ktile == 0

    def qrow(w):
        return pl.BlockSpec((None, qblk, w), lambda b, i: (b, i, 0))

    def whole(w):
        return pl.BlockSpec((None, L, w), lambda b, i: (b, 0, 0))

    r3 = lambda a: a.reshape(B, L, a.shape[-1])
    out = pl.pallas_call(
        functools.partial(_dsa_kernel, k_sel=k_sel, qblk=qblk, ktile=ktile),
        grid=(B, nq),
        in_specs=[qrow(IDX_HEADS * IDX_DIM), qrow(LANES), qrow(ATT_WIDTH), whole(LANES), whole(KV_WIDTH),
                  whole(2 * KV_WIDTH)],
        out_specs=qrow(ATT_WIDTH),
        out_shape=jax.ShapeDtypeStruct((B, L, ATT_WIDTH), BF16),
        scratch_shapes=[pltpu.VMEM((L, qblk), F32),
                        pltpu.VMEM((IDX_HEADS * qblk, LANES), BF16),
                        pltpu.VMEM((N_KV_HEADS, g_rows, 2 * HEAD_DIM), BF16),
                        pltpu.VMEM((N_KV_HEADS, g_rows, LANES), F32),
                        pltpu.VMEM((N_KV_HEADS, g_rows, 2 * HEAD_DIM), F32)],
        compiler_params=pltpu.CompilerParams(dimension_semantics=("arbitrary", "arbitrary"),
                                             vmem_limit_bytes=VMEM_LIMIT),
        name="dsa",
    )(r3(qi), r3(wi), r3(q), r3(ki2), r3(k), r3(v))
    return out.reshape(B * L, ATT_WIDTH)


def _gelu_tanh(x):
    return 0.5 * x * (1.0 + jnp.tanh(math.sqrt(2.0 / math.pi) * (x + 0.044715 * (x * x * x))))


def _merge_kernel(x_ref, y_ref, yb_ref, gate_ref, wglu_ref, wa_ref, wb_ref, wo_ref, o_ref, y_scr):
    n_st, rt, d = x_ref.shape
    rows = n_st * rt
    _store_lane_blocks(y_scr, y_ref[...].astype(F32))
    y = jnp.concatenate([_strided_rows(y_scr, s, rt, n_st) for s in range(n_st)], axis=0)
    y = _gelu_tanh(y)
    ya = y * jax.nn.sigmoid(_dot(y.astype(BF16), wglu_ref[...]))
    gate = gate_ref[...].reshape(rows, 2 * d)
    ga, gb = gate[:, :D_MODEL], gate[:, D_MODEL:]
    merged = (jax.nn.sigmoid(ga) * _dot(ya.astype(BF16), wa_ref[...])
              + jax.nn.sigmoid(gb) * _dot(yb_ref[...].reshape(rows, ATT_WIDTH), wb_ref[...]))
    out = x_ref[...].reshape(rows, d) + _dot(merged.astype(BF16), wo_ref[...])
    o_ref[...] = out.reshape(n_st, rt, d)


def _merge(x, y, yb, gate, w_glu, w_a, w_b, w_out):
    n_st, l_seg, _ = x.shape
    rt = ROW_TILE
    ws = [w.astype(BF16) for w in (w_glu, w_a, w_b, w_out)]

    def streams(w):
        return pl.BlockSpec((n_st, rt, w), lambda i: (0, i, 0))

    return pl.pallas_call(
        _merge_kernel,
        grid=(l_seg // rt,),
        in_specs=[streams(D_MODEL), pl.BlockSpec((None, rt * n_st, D_MODEL), lambda i: (i, 0, 0)),
                  streams(ATT_WIDTH), streams(2 * D_MODEL)]
                 + [pl.BlockSpec(w.shape, lambda i: (0, 0)) for w in ws],
        out_specs=streams(D_MODEL),
        out_shape=jax.ShapeDtypeStruct((n_st, l_seg, D_MODEL), F32),
        scratch_shapes=[pltpu.VMEM((D_MODEL // LANES, rt * n_st, LANES), F32)],
        compiler_params=pltpu.CompilerParams(dimension_semantics=("arbitrary",), vmem_limit_bytes=VMEM_LIMIT),
        name="merge",
    )(x, y, yb, gate, *ws)


def _rms(x, g):
    return x * lax.rsqrt(jnp.mean(x * x, axis=-1, keepdims=True) + EPS) * g


def _ffn_kernel(x_ref, g2_ref, win_ref, wout_ref, gf_ref, o_ref, *, final_norm):
    x = x_ref[...]
    h = _rms(x, g2_ref[...]).astype(BF16)
    gu = _dot(h, win_ref[...])
    g, up = gu[:, :FFN_HIDDEN], gu[:, FFN_HIDDEN:]
    act = (g * jax.nn.sigmoid(g)) * up
    x = x + _dot(act.astype(BF16), wout_ref[...])
    o_ref[...] = _rms(x, gf_ref[...]) if final_norm else x


def _ffn(x1, g2, w_ffn_in, w_ffn_out, gf, final_norm, tile):
    T = x1.shape[0]
    win, wout = w_ffn_in.astype(BF16), w_ffn_out.astype(BF16)

    def row(w):
        return pl.BlockSpec((tile, w), lambda i: (i, 0))

    def full(a):
        return pl.BlockSpec(a.shape, lambda i: (0, 0))

    return pl.pallas_call(
        functools.partial(_ffn_kernel, final_norm=final_norm),
        grid=(T // tile,),
        in_specs=[row(D_MODEL), full(g2), full(win), full(wout), full(gf)],
        out_specs=row(D_MODEL),
        out_shape=jax.ShapeDtypeStruct((T, D_MODEL), F32),
        compiler_params=pltpu.CompilerParams(dimension_semantics=("arbitrary",), vmem_limit_bytes=VMEM_LIMIT),
        name="ffn",
    )(x1, g2, win, wout, gf)


def kernel(x, norm1_g, w_in, a_re, a_im, log_dt, b_re, b_im, c_re, c_im, d_skip, w_glu,
           w_branch_a, w_branch_b, w_out, norm2_g, w_ffn_in, w_ffn_out, norm_f_g):
    B, L, D = x.shape
    depth = norm1_g.shape[0]
    n_seg = S5_STREAMS // B
    assert D == D_MODEL and S5_STREAMS % B == 0 and L % 512 == 0 and L % (n_seg * ROW_TILE) == 0
    x = x.astype(F32)
    for i in range(depth):
        up, q, k, v, qi, ki2, wi, gate = _project(x, norm1_g[i][None, :].astype(F32), w_in[i])
        y = _s5_branch(up, B, a_re[i], a_im[i], log_dt[i], b_re[i], b_im[i], c_re[i], c_im[i], d_skip[i])
        yb = _dsa_branch(q, k, v, qi, ki2, wi, B, L)
        x1 = _merge(x.reshape(S5_STREAMS, L // n_seg, D), y, yb.reshape(S5_STREAMS, L // n_seg, ATT_WIDTH), gate,
                    w_glu[i], w_branch_a[i], w_branch_b[i], w_out[i])
        x = _ffn(x1.reshape(B * L, D), norm2_g[i][None, :].astype(F32), w_ffn_in[i], w_ffn_out[i],
                 norm_f_g[None, :].astype(F32), i == depth - 1, 256).reshape(B, L, D)
    return x
```

```python
import functools
import math

import jax
import jax.numpy as jnp
import numpy as np
from jax import lax
from jax.experimental import pallas as pl
from jax.experimental.pallas import tpu as pltpu

F32 = jnp.float32
BF16 = jnp.bfloat16
I32 = jnp.int32

D_MODEL = 1024
CHUNK = 64
EPS = 1e-6
NEG = -1e30

SSM_GROUP = 16
SSM_GROUPS = 64
SSM_STATE = 64
S5_CHUNK = 16
S5_STREAMS = 16
S5_LANE_GROUPS = 8
ROW_TILE = S5_CHUNK

N_HEADS = 8
HEAD_DIM = 128
N_KV_HEADS = 2
Q_PER_KV = N_HEADS // N_KV_HEADS
ATT_WIDTH = N_HEADS * HEAD_DIM
KV_WIDTH = N_KV_HEADS * HEAD_DIM
IDX_HEADS = 16
IDX_DIM = 64
TOPK_MAX = 256
ROPE_THETA = 500000.0
ATT_ROT = HEAD_DIM // 4
IDX_ROT = IDX_DIM // 4
FFN_HIDDEN = -(-8 * D_MODEL // (3 * 256)) * 256

SCORE_ROWS = 16
COUNT_ROWS = 32
SELECT_STEPS = 12
ATT_ROWS = 32

LANES = 128
INT_MIN = -(2 ** 31)

VMEM_LIMIT = 56 * 1024 * 1024


def _key_of_float(val):
    bits = int(np.float32(val).view(np.int32))
    return bits ^ ((bits >> 31) & 0x7FFFFFFF)


KEY_HALF_NEG = _key_of_float(0.5 * NEG)


def _dot(a, b):
    return jnp.dot(a, b, preferred_element_type=F32)


def _dot_nt(a, b):
    return lax.dot_general(a, b, (((1,), (1,)), ((), ())), preferred_element_type=F32)


def _split_hi_lo(a):
    hi = a.astype(BF16)
    lo = (a - hi.astype(F32)).astype(BF16)
    return hi, lo


def _rope(x, c, s1, s2, half):
    n = x.shape[-1]
    return x * c + pltpu.roll(x, half, 1) * s1 + pltpu.roll(x, n - half, 1) * s2


def _store_lane_blocks(scr, val):
    for j in range(scr.shape[0]):
        scr[j] = val[:, j * LANES:(j + 1) * LANES]


def _strided_rows(scr, start, size, stride):
    return jnp.concatenate([scr[j, pl.ds(start, size, stride=stride), :] for j in range(scr.shape[0])], axis=1)


def _proj_kernel(x_ref, g1_ref, wu_ref, wq_ref, wk_ref, wv_ref, wqi_ref, wsm_ref, wg_ref,
                 ca_ref, sa1_ref, sa2_ref, ci_ref, si1_ref, si2_ref, cs_ref, ss1_ref, ss2_ref,
                 u_ref, q_ref, k_ref, v_ref, qi_ref, ki2_ref, wi_ref, gate_ref, u_scr, *, batch):
    n_st, rt, d = x_ref.shape
    rows = n_st * rt
    x = x_ref[...].reshape(rows, d)
    h = x * lax.rsqrt(jnp.mean(x * x, axis=-1, keepdims=True) + EPS) * g1_ref[...]
    hb = h.astype(BF16)

    def put(ref, val, sl=slice(None)):
        ref[:, :, sl] = val.reshape(n_st, rt, val.shape[-1])

    def table(ref):
        t = ref[...].reshape(rows // batch, LANES)
        return jnp.concatenate([t] * batch, axis=0)

    _store_lane_blocks(u_scr, _dot(hb, wu_ref[...]))
    for t in range(rt):
        u_ref[t * n_st:(t + 1) * n_st, :] = _strided_rows(u_scr, t, n_st, rt).astype(BF16)

    v = _dot(hb, wv_ref[...]).astype(BF16)
    ones = jnp.ones((rows, HEAD_DIM), BF16)
    for hd in range(N_KV_HEADS):
        put(v_ref, v[:, hd * HEAD_DIM:(hd + 1) * HEAD_DIM], slice(2 * hd * HEAD_DIM, (2 * hd + 1) * HEAD_DIM))
        put(v_ref, ones, slice((2 * hd + 1) * HEAD_DIM, (2 * hd + 2) * HEAD_DIM))
    put(gate_ref, _dot(hb, wg_ref[...]))

    ca, sa1, sa2 = table(ca_ref), table(sa1_ref), table(sa2_ref)
    q = _dot(hb, wq_ref[...])
    scale = HEAD_DIM ** -0.5 * math.log2(math.e)
    for hd in range(N_HEADS):
        sl = slice(hd * HEAD_DIM, (hd + 1) * HEAD_DIM)
        put(q_ref, (_rope(q[:, sl], ca, sa1, sa2, ATT_ROT // 2) * scale).astype(BF16), sl)
    k = _dot(hb, wk_ref[...])
    for hd in range(N_KV_HEADS):
        sl = slice(hd * HEAD_DIM, (hd + 1) * HEAD_DIM)
        put(k_ref, _rope(k[:, sl], ca, sa1, sa2, ATT_ROT // 2).astype(BF16), sl)

    ci, si1, si2 = table(ci_ref), table(si1_ref), table(si2_ref)
    qi = _dot(hb, wqi_ref[...])
    for pr in range(IDX_HEADS * IDX_DIM // LANES):
        sl = slice(pr * LANES, (pr + 1) * LANES)
        put(qi_ref, _rope(qi[:, sl], ci, si1, si2, IDX_ROT // 2).astype(BF16), sl)

    sm = _dot(hb, wsm_ref[...])
    sm = _rope(sm, table(cs_ref), table(ss1_ref), table(ss2_ref), IDX_ROT // 2)
    lane = lax.broadcasted_iota(I32, sm.shape, 1)
    put(ki2_ref, jnp.where(lane < IDX_DIM, sm, pltpu.roll(sm, IDX_DIM, 1)).astype(BF16))
    put(wi_ref, sm * (IDX_DIM ** -0.5 * IDX_HEADS ** -0.5))


def _rope_tables(L, rot, period, active_lanes):
    half = rot // 2
    pos = jnp.arange(L, dtype=jnp.int32)
    inv_freq = ROPE_THETA ** (-jnp.arange(half, dtype=F32) / half)
    ang = pos.astype(F32)[:, None] * inv_freq[None, :]
    cos, sin = jnp.cos(ang), jnp.sin(ang)
    lane = np.arange(LANES)
    within = lane % period
    fidx = np.where(within < rot, within % half, 0)
    is_x1 = (within < half) & (lane < active_lanes)
    is_x2 = (within >= half) & (within < rot) & (lane < active_lanes)
    rot_lane = is_x1 | is_x2
    c = jnp.where(rot_lane[None, :], cos[:, fidx], 1.0)
    s1 = jnp.where(is_x2[None, :], sin[:, fidx], 0.0)
    s2 = jnp.where(is_x1[None, :], -sin[:, fidx], 0.0)
    return c.astype(F32), s1.astype(F32), s2.astype(F32)


def _project(x, g1, w_in):
    B, L, _ = x.shape
    n_seg = S5_STREAMS // B
    l_seg = L // n_seg
    rt = ROW_TILE
    o = 0
    parts = []
    for s in (D_MODEL, ATT_WIDTH, KV_WIDTH, KV_WIDTH, IDX_HEADS * IDX_DIM, IDX_DIM, IDX_HEADS, D_MODEL, D_MODEL):
        parts.append(w_in[:, o:o + s])
        o += s
    wu, wq, wk, wv, wqi, wki, wwi, wga, wgb = parts
    wsm = jnp.concatenate([wki, wwi, jnp.zeros((D_MODEL, LANES - IDX_DIM - IDX_HEADS), w_in.dtype)], axis=1)
    wg = jnp.concatenate([wga, wgb], axis=1)
    ws = [w.astype(BF16) for w in (wu, wq, wk, wv, wqi, wsm, wg)]
    tabs = (_rope_tables(L, ATT_ROT, HEAD_DIM, LANES)
            + _rope_tables(L, IDX_ROT, IDX_DIM, LANES)
            + _rope_tables(L, IDX_ROT, IDX_DIM, IDX_DIM))
    tabs = [t.reshape(n_seg, l_seg, LANES) for t in tabs]

    def streams(w):
        return pl.BlockSpec((S5_STREAMS, rt, w), lambda i: (0, i, 0))

    def full(a):
        return pl.BlockSpec(a.shape, lambda i: (0, 0))

    tab_spec = pl.BlockSpec((n_seg, rt, LANES), lambda i: (0, i, 0))
    widths = (ATT_WIDTH,
              KV_WIDTH,
              2 * KV_WIDTH,
              IDX_HEADS * IDX_DIM,
              LANES,
              LANES,
              2 * D_MODEL)
    dtypes = (BF16, BF16, BF16, BF16, BF16, F32, F32)
    out_shape = ((jax.ShapeDtypeStruct((l_seg // rt, rt * S5_STREAMS, D_MODEL), BF16),)
                 + tuple(jax.ShapeDtypeStruct((S5_STREAMS, l_seg, w), dt) for w, dt in zip(widths, dtypes)))
    out_specs = ((pl.BlockSpec((None, rt * S5_STREAMS, D_MODEL), lambda i: (i, 0, 0)),)
                 + tuple(streams(w) for w in widths))
    return pl.pallas_call(
        functools.partial(_proj_kernel, batch=B),
        grid=(l_seg // rt,),
        in_specs=[streams(D_MODEL), full(g1)] + [full(w) for w in ws] + [tab_spec] * 9,
        out_specs=out_specs,
        out_shape=out_shape,
        scratch_shapes=[pltpu.VMEM((D_MODEL // LANES, rt * S5_STREAMS, LANES), F32)],
        compiler_params=pltpu.CompilerParams(dimension_semantics=("arbitrary",), vmem_limit_bytes=VMEM_LIMIT),
        name="proj",
    )(x.reshape(S5_STREAMS, l_seg, D_MODEL), g1, *ws, *tabs)


def _s5_group_weights(ldt, ar, ai, btr, bti, cre, cim):
    lane = lax.broadcasted_iota(I32, (1, LANES), 1)
    lo = lane < SSM_STATE
    dt = jnp.exp(ldt)
    rho, th = ar * dt, ai * dt
    npow = S5_CHUNK + 1
    nn = lax.broadcasted_iota(I32, (24, LANES), 0).astype(F32)
    mag = jnp.exp(nn * rho)
    lc = mag * jnp.cos(nn * th)
    ls = mag * jnp.sin(nn * th)
    lam_pk = jnp.where(lo, lc, ls)
    lam_sw = jnp.where(lo, -ls, lc)
    num_re, num_im = lc[1:2] - 1.0, ls[1:2]
    den = ar * ar + ai * ai
    f_re = (num_re * ar + num_im * ai) / den
    f_im = (num_im * ar - num_re * ai) / den
    g_re = btr * f_re - bti * f_im
    g_im = btr * f_im + bti * f_re
    g_neg = jnp.where(lo, g_re, -g_im)
    w_pk = [cre * lam_pk[n:n + 1] + cim * lam_sw[n:n + 1] for n in range(npow)]
    w_state = [g_re * lam_pk[S5_CHUNK - 1 - ti:S5_CHUNK - ti] + g_im * lam_sw[S5_CHUNK - 1 - ti:S5_CHUNK - ti]
               for ti in range(S5_CHUNK)]
    a_p = jnp.where(lo, lam_pk[S5_CHUNK:S5_CHUNK + 1], lam_sw[S5_CHUNK:S5_CHUNK + 1])
    a_q = jnp.where(lo, lam_sw[S5_CHUNK:S5_CHUNK + 1], lam_pk[S5_CHUNK:S5_CHUNK + 1])
    return g_neg, w_pk, w_state, a_p, a_q


def _place(block, g):
    z = jnp.zeros_like(block)
    return jnp.concatenate([block if j == g else z for j in range(S5_LANE_GROUPS)], axis=1)


def _swap_halves(x):
    return jnp.concatenate([pltpu.roll(x[:, g * LANES:(g + 1) * LANES], SSM_STATE, 1)
                            for g in range(x.shape[1] // LANES)], axis=1)


def _s5_kernel(up_ref, are_ref, aim_ref, ldt_ref, btr_ref, bti_ref, cre_ref, cim_ref, dsk_ref,
               y_ref, s_ref, ssw_ref, hprev_ref, kbd_ref, wst_ref, wot_ref, *, n_steps, n_seg):
    ng = S5_LANE_GROUPS
    lane = lax.broadcasted_iota(I32, (1, LANES), 1)
    lo = lane < SSM_STATE
    weights = [_s5_group_weights(ldt_ref[g], are_ref[g], aim_ref[g], btr_ref[g], bti_ref[g], cre_ref[g], cim_ref[g])
               for g in range(ng)]
    gneg_blk = jnp.concatenate([_place(weights[g][0], g) for g in range(ng)], axis=0)
    for tau in range(S5_CHUNK):
        wt = jnp.concatenate([_place(weights[g][1][tau], g) for g in range(ng)], axis=0)
        kbd_ref[:, tau * LANES:(tau + 1) * LANES] = lax.dot_general(
            gneg_blk, wt, (((1,), (1,)), ((), ())), preferred_element_type=F32, precision=lax.Precision.HIGHEST)
    for ti in range(S5_CHUNK):
        wst_ref[ti * LANES:(ti + 1) * LANES, :] = jnp.concatenate(
            [_place(weights[g][2][ti], g) for g in range(ng)], axis=0).astype(BF16)
        wot_ref[ti * LANES:(ti + 1) * LANES, :] = jnp.concatenate(
            [_place(jnp.where(lo, weights[g][1][ti + 1], -weights[g][1][ti + 1]), g) for g in range(ng)],
            axis=0).astype(BF16)
    a_p = jnp.concatenate([weights[g][3] for g in range(ng)], axis=1)
    a_q = jnp.concatenate([weights[g][4] for g in range(ng)], axis=1)

    rows = n_steps * S5_STREAMS

    def chunk_rows(ti):
        return up_ref[:, ti * S5_STREAMS:(ti + 1) * S5_STREAMS, :].reshape(rows, LANES)

    lhs = jnp.concatenate([chunk_rows(ti) for ti in range(S5_CHUNK)], axis=1)
    s_all = _dot(lhs, wst_ref[...])
    s_ref[...] = s_all
    ssw_ref[...] = _swap_halves(s_all)

    def cmul(x, p, q):
        return x * p + _swap_halves(x) * q

    def step(h, hs, i):
        rows_i = pl.ds(pl.multiple_of(i * S5_STREAMS, S5_STREAMS), S5_STREAMS)
        return h * a_p + hs * a_q + s_ref[rows_i, :], hs * a_p - h * a_q + ssw_ref[rows_i, :]

    zero = jnp.zeros((S5_STREAMS, ng * LANES), F32)
    z, _ = lax.fori_loop(0, n_steps, lambda i, c: step(c[0], c[1], i), (zero, zero))

    lo_all = jnp.concatenate([lo] * ng, axis=1)

    def factors(zpk):
        sw = _swap_halves(zpk)
        return jnp.where(lo_all, zpk, sw), jnp.where(lo_all, -sw, zpk)

    base, seg, n = jnp.where(lo_all, a_p, a_q), None, n_steps
    while n:
        if n & 1:
            seg = base if seg is None else cmul(seg, *factors(base))
        n >>= 1
        if n:
            base = cmul(base, *factors(base))
    seg_p, seg_q = factors(seg)

    row_id = lax.broadcasted_iota(I32, (S5_STREAMS, ng * LANES), 0)
    init = zero
    prev = None
    for s in range(S5_STREAMS):
        if s % n_seg == 0:
            cur = jnp.zeros((1, ng * LANES), F32)
        else:
            cur = cmul(prev, seg_p, seg_q) + z[s - 1:s]
            init = jnp.where(row_id == s, cur, init)
        prev = cur

    def pass2(i, c):
        hprev_ref[pl.ds(pl.multiple_of(i * S5_STREAMS, S5_STREAMS), S5_STREAMS), :] = c[0]
        return step(c[0], c[1], i)

    lax.fori_loop(0, n_steps, pass2, (init, _swap_halves(init)))

    hp = hprev_ref[...].astype(BF16)
    dsk = dsk_ref[...]
    zblk = jnp.zeros((LANES, LANES), F32)
    for tp in range(S5_CHUNK // 2):
        kdim = (2 * tp + 2) * LANES
        cols = []
        for t2 in (2 * tp, 2 * tp + 1):
            cols.append(jnp.concatenate(
                [kbd_ref[:, (t2 - ti) * LANES:(t2 - ti + 1) * LANES] if ti <= t2 else zblk
                 for ti in range(2 * tp + 2)], axis=0))
        slab = jnp.concatenate(cols, axis=1).astype(BF16)
        y = _dot(lhs[:, :kdim], slab) + _dot_nt(hp, wot_ref[2 * tp * LANES:(2 * tp + 2) * LANES, :])
        for half in range(2):
            t2 = 2 * tp + half
            u_t = up_ref[:, t2 * S5_STREAMS:(t2 + 1) * S5_STREAMS, :].astype(F32)
            y_t = y[:, half * LANES:(half + 1) * LANES].reshape(n_steps, S5_STREAMS, LANES) + u_t * dsk
            y_ref[:, t2 * S5_STREAMS:(t2 + 1) * S5_STREAMS, :] = y_t.astype(y_ref.dtype)


def _s5_branch(up, B, a_re, a_im, log_dt, b_re, b_im, c_re, c_im, d_skip):
    G, C = SSM_GROUPS, SSM_GROUP
    n_steps = up.shape[0]
    n_seg = S5_STREAMS // B
    rows = n_steps * S5_STREAMS
    ng = S5_LANE_GROUPS

    def dup(a):
        return jnp.concatenate([a, a], axis=-1).astype(F32)

    are2 = dup(a_re)[:, None, :]
    aim2 = dup(a_im)[:, None, :]
    ldt = log_dt.astype(F32)[:, None, None]
    btr2 = dup(jnp.swapaxes(b_re, 1, 2))
    bti2 = dup(jnp.swapaxes(b_im, 1, 2))
    cre2 = dup(c_re)
    cim2 = dup(c_im)
    dsk = d_skip.astype(F32).reshape(G // ng, 1, ng * C)

    def per_block(shape):
        return pl.BlockSpec((ng,) + shape, lambda i: (i, 0, 0))

    act = pl.BlockSpec((n_steps, S5_CHUNK * S5_STREAMS, LANES), lambda i: (0, 0, i))
    return pl.pallas_call(
        functools.partial(_s5_kernel, n_steps=n_steps, n_seg=n_seg),
        grid=(G // ng,),
        in_specs=[act, per_block((1, LANES)), per_block((1, LANES)), per_block((1, 1)),
                  per_block((C, LANES)), per_block((C, LANES)), per_block((C, LANES)), per_block((C, LANES)),
                  pl.BlockSpec((None, 1, LANES), lambda i: (i, 0, 0))],
        out_specs=act,
        out_shape=jax.ShapeDtypeStruct(up.shape, BF16),
        scratch_shapes=[pltpu.VMEM((rows, ng * LANES), F32),
                        pltpu.VMEM((rows, ng * LANES), F32),
                        pltpu.VMEM((rows, ng * LANES), F32),
                        pltpu.VMEM((LANES, S5_CHUNK * LANES), F32),
                        pltpu.VMEM((S5_CHUNK * LANES, ng * LANES), BF16),
                        pltpu.VMEM((S5_CHUNK * LANES, ng * LANES), BF16)],
        compiler_params=pltpu.CompilerParams(dimension_semantics=("arbitrary",), vmem_limit_bytes=VMEM_LIMIT),
        name="s5",
    )(up, are2, aim2, ldt, btr2, bti2, cre2, cim2, dsk)


def _float_to_key(x):
    bits = lax.bitcast_convert_type(x, I32)
    return bits ^ (lax.shift_right_arithmetic(bits, 31) & 0x7FFFFFFF)


def _key_to_float(key):
    return lax.bitcast_convert_type(key ^ (lax.shift_right_arithmetic(key, 31) & 0x7FFFFFFF), F32)


def _probit(p):
    t = jnp.sqrt(-2.0 * jnp.log(jnp.minimum(p, 1.0 - p)))
    z = t - ((0.010328 * t + 0.802853) * t + 2.515517) / (((0.001308 * t + 0.189269) * t + 1.432788) * t + 1.0)
    return jnp.where(p < 0.5, -z, z)


def _count_ge(sc_ref, trial, n_tiles, ktile):
    def count_tile(t, acc):
        off = pl.multiple_of(t * ktile, ktile)
        c = (sc_ref[pl.ds(off, ktile), :] >= trial).astype(I32)
        return acc + jnp.sum(c.reshape(ktile // COUNT_ROWS, COUNT_ROWS, LANES), axis=0)

    acc = lax.fori_loop(0, n_tiles, count_tile, jnp.zeros((COUNT_ROWS, LANES), I32))
    return jnp.sum(acc, axis=0, keepdims=True).astype(F32)


def _select_threshold(sc_ref, s_min, s_max, n_allowed, n_tiles, k_sel, ktile):
    kf = float(k_sel)
    z_t = _probit(1.0 - (kf - 0.5) / n_allowed)

    def z_of(c):
        return _probit(jnp.clip(1.0 - c / n_allowed, 0.5 / n_allowed, 1.0 - 0.5 / n_allowed))

    def body(st):
        it, t_lo, t_hi, c_lo, c_hi, z_lo, z_hi, w_lo, w_hi, side, done, thr = st
        g_lo = (z_lo - z_t) * w_lo
        g_hi = (z_hi - z_t) * w_hi
        t = t_lo + (t_hi - t_lo) * jnp.clip(g_lo / (g_lo - g_hi), 0.0, 1.0)
        k_lo, k_hi, k_t = _float_to_key(t_lo), _float_to_key(t_hi), _float_to_key(t)
        mid = lax.shift_right_arithmetic(k_lo, 1) + lax.shift_right_arithmetic(k_hi, 1) + (k_lo & k_hi & 1)
        use_mid = (k_t <= k_lo) | (k_t >= k_hi) | (c_lo - c_hi <= 4.0) | (it >= 20)
        t = _key_to_float(jnp.where(use_mid, mid, k_t))
        c = _count_ge(sc_ref, t, n_tiles, ktile)
        active = done == 0
        hit = active & (c == kf)
        new_lo = active & (c >= kf)
        new_hi = active & (c < kf)
        z_c = z_of(c)
        w_hi = jnp.where(new_lo, jnp.where(side > 0, 0.5 * w_hi, w_hi), 1.0)
        w_lo = jnp.where(new_hi, jnp.where(side < 0, 0.5 * w_lo, w_lo), 1.0)
        side = jnp.where(new_lo, 1.0, jnp.where(new_hi, -1.0, side))
        t_lo = jnp.where(new_lo, t, t_lo)
        c_lo = jnp.where(new_lo, c, c_lo)
        z_lo = jnp.where(new_lo, z_c, z_lo)
        t_hi = jnp.where(new_hi, t, t_hi)
        c_hi = jnp.where(new_hi, c, c_hi)
        z_hi = jnp.where(new_hi, z_c, z_hi)
        adjacent = active & (_float_to_key(t_hi) - 1 <= _float_to_key(t_lo))
        thr = jnp.where(hit, t, jnp.where(adjacent, t_lo, thr))
        done = jnp.where(hit | adjacent, 1, done)
        return it + 1, t_lo, t_hi, c_lo, c_hi, z_lo, z_hi, w_lo, w_hi, side, done, thr

    def cond(st):
        return jnp.logical_and(st[0] < 64, jnp.min(st[10]) < 1)

    few = n_allowed <= kf
    ones = jnp.ones((1, LANES), F32)
    zeros = jnp.zeros((1, LANES), F32)
    init = (jnp.int32(0), s_min, _key_to_float(_float_to_key(s_max) + 1), n_allowed, zeros,
            z_of(n_allowed), z_of(zeros), ones, ones, zeros,
            few.astype(I32), jnp.where(few, -jnp.inf, s_min))
    st = lax.fori_loop(0, SELECT_STEPS, lambda _, s: body(s), init)
    return lax.while_loop(cond, body, st)[11]


def _dsa_kernel(qi_ref, wi_ref, q_ref, ki2_ref, k_ref, v_ref, o_ref,
                sc_ref, qm_ref, qg_ref, m_ref, acc_ref, rel0_ref, rel1_ref, lg0_ref, lg1_ref, *, k_sel, qblk, ktile):
    assert qblk == LANES
    qb = pl.program_id(1)
    n_tiles = lax.div(qb * qblk + qblk + ktile - 1, ktile)
    lane = lax.broadcasted_iota(I32, (qblk, LANES), 1)
    n_lt = ktile // LANES

    for hd in range(IDX_HEADS):
        pair = qi_ref[:, (hd // 2) * LANES:(hd // 2 + 1) * LANES]
        msk = (lane < IDX_DIM) if hd % 2 == 0 else (lane >= IDX_DIM)
        qm_ref[hd * qblk:(hd + 1) * qblk, :] = jnp.where(msk, pair, jnp.zeros_like(pair))
    eye = (lax.broadcasted_iota(I32, (qblk, LANES), 0) == lane).astype(BF16)
    for hd in range(N_HEADS):
        g, r = divmod(hd, Q_PER_KV)
        qg_ref[g, r * qblk:(r + 1) * qblk, 0:HEAD_DIM] = q_ref[:, hd * HEAD_DIM:(hd + 1) * HEAD_DIM]
        qg_ref[g, r * qblk:(r + 1) * qblk, HEAD_DIM:2 * HEAD_DIM] = eye

    w_t = wi_ref[...].T
    q_chunk = lax.shift_right_logical(qb * qblk + lax.broadcasted_iota(I32, (1, LANES), 1), 6)
    n_allowed = ((q_chunk + 1) * CHUNK).astype(F32)

    rel_refs = (rel0_ref, rel1_ref)
    lg_refs = (lg0_ref, lg1_ref)

    def tile_offset(t):
        return pl.multiple_of(jnp.minimum(t, n_tiles - 1) * ktile, ktile)

    def issue_logits(slot, t):
        rel_refs[slot][...] = _dot_nt(ki2_ref[pl.ds(tile_offset(t), ktile), :], qm_ref[...])

    def score_tile(slot, t, carry):
        mx, mn = carry
        off = tile_offset(t)
        for c in range(ktile // SCORE_ROWS):
            r0 = c * SCORE_ROWS
            sc = jnp.zeros((SCORE_ROWS, LANES), F32)
            for hd in range(IDX_HEADS):
                rel = rel_refs[slot][r0:r0 + SCORE_ROWS, hd * qblk:(hd + 1) * qblk]
                sc = sc + jnp.maximum(rel, 0.0) * w_t[IDX_DIM + hd:IDX_DIM + hd + 1, :]
            k_chunk = lax.shift_right_logical(off + r0 + lax.broadcasted_iota(I32, (SCORE_ROWS, 1), 0), 6)
            ok = k_chunk <= q_chunk
            sc_ref[pl.ds(off + r0, SCORE_ROWS), :] = jnp.where(ok, sc, NEG)
            mx = jnp.maximum(mx, jnp.where(ok, sc, -jnp.inf))
            mn = jnp.minimum(mn, jnp.where(ok, sc, jnp.inf))
        return mx, mn

    def score_pair(p, carry):
        t0 = 2 * p
        issue_logits(1, t0 + 1)
        carry = score_tile(0, t0, carry)
        issue_logits(0, t0 + 2)
        return score_tile(1, t0 + 1, carry)

    issue_logits(0, 0)
    mx, mn = lax.fori_loop(0, lax.div(n_tiles + 1, 2), score_pair,
                           (jnp.full((SCORE_ROWS, LANES), -jnp.inf, F32), jnp.full((SCORE_ROWS, LANES), jnp.inf, F32)))
    s_max = jnp.max(mx, axis=0, keepdims=True)
    s_min = jnp.min(mn, axis=0, keepdims=True)

    thr = _select_threshold(sc_ref, s_min, s_max, n_allowed, n_tiles, k_sel, ktile)
    thr = jnp.maximum(thr, _key_to_float(jnp.full((1, LANES), KEY_HALF_NEG + 1, I32)))

    m_ref[...] = jnp.full(m_ref.shape, -jnp.inf, F32)
    acc_ref[...] = jnp.zeros(acc_ref.shape, F32)
    g_rows = Q_PER_KV * qblk

    def issue_qk(slot, t):
        off = tile_offset(t)
        thr_t = jnp.where(t < n_tiles, thr, jnp.inf)
        bias_t = jnp.where(sc_ref[pl.ds(off, ktile), :] >= thr_t, 0.0, NEG).astype(BF16)
        for g in range(N_KV_HEADS):
            kb = jnp.concatenate([k_ref[pl.ds(off, ktile), g * HEAD_DIM:(g + 1) * HEAD_DIM], bias_t], axis=1)
            lg_refs[slot][g] = _dot_nt(qg_ref[g], kb)

    def attend_tile(slot, t):
        off = tile_offset(t)
        for g in range(N_KV_HEADS):
            ps, alphas = [], []
            for c in range(g_rows // ATT_ROWS):
                r0 = c * ATT_ROWS
                s = lg_refs[slot][g, r0:r0 + ATT_ROWS, :]
                m_cur = s[:, 0:LANES]
                for j in range(1, n_lt):
                    m_cur = jnp.maximum(m_cur, s[:, j * LANES:(j + 1) * LANES])
                m_old = m_ref[g, r0:r0 + ATT_ROWS, :]
                m_new = jnp.maximum(m_old, jnp.max(m_cur, axis=1, keepdims=True))
                alphas.append(jnp.exp2(m_old - m_new))
                m_ref[g, r0:r0 + ATT_ROWS, :] = m_new
                ps.append(jnp.exp2(s - jnp.concatenate([m_new] * n_lt, axis=1)).astype(BF16))
            alpha = jnp.concatenate(alphas, axis=0)
            pv = _dot(jnp.concatenate(ps, axis=0), v_ref[pl.ds(off, ktile), 2 * g * HEAD_DIM:(2 * g + 2) * HEAD_DIM])
            acc_ref[g] = acc_ref[g] * jnp.concatenate([alpha, alpha], axis=1) + pv

    def attend_pair(p, carry):
        t0 = 2 * p
        issue_qk(1, t0 + 1)
        attend_tile(0, t0)
        issue_qk(0, t0 + 2)
        attend_tile(1, t0 + 1)
        return carry

    issue_qk(0, 0)
    lax.fori_loop(0, lax.div(n_tiles + 1, 2), attend_pair, 0)
    for hd in range(N_HEADS):
        g, r = divmod(hd, Q_PER_KV)
        a = acc_ref[g, r * qblk:(r + 1) * qblk, :]
        o_ref[:, hd * HEAD_DIM:(hd + 1) * HEAD_DIM] = (a[:, :HEAD_DIM] / a[:, HEAD_DIM:]).astype(o_ref.dtype)


def _dsa_branch(q, k, v, qi, ki2, wi, B, L, qblk=128, ktile=512):
    k_sel = min(TOPK_MAX, L // 4)
    nq = L // qblk
    g_rows = Q_PER_KV * qblk
    assert L % ktile == 0

    def qrow(w):
        return pl.BlockSpec((None, qblk, w), lambda b, i: (b, i, 0))

    def whole(w):
        return pl.BlockSpec((None, L, w), lambda b, i: (b, 0, 0))

    r3 = lambda a: a.reshape(B, L, a.shape[-1])
    out = pl.pallas_call(
        functools.partial(_dsa_kernel, k_sel=k_sel, qblk=qblk, ktile=ktile),
        grid=(B, nq),
        in_specs=[qrow(IDX_HEADS * IDX_DIM), qrow(LANES), qrow(ATT_WIDTH), whole(LANES), whole(KV_WIDTH),
                  whole(2 * KV_WIDTH)],
        out_specs=qrow(ATT_WIDTH),
        out_shape=jax.ShapeDtypeStruct((B, L, ATT_WIDTH), BF16),
        scratch_shapes=[pltpu.VMEM((L, qblk), F32),
                        pltpu.VMEM((IDX_HEADS * qblk, LANES), BF16),
                        pltpu.VMEM((N_KV_HEADS, g_rows, 2 * HEAD_DIM), BF16),
                        pltpu.VMEM((N_KV_HEADS, g_rows, LANES), F32),
                        pltpu.VMEM((N_KV_HEADS, g_rows, 2 * HEAD_DIM), F32),
                        pltpu.VMEM((ktile, IDX_HEADS * qblk), F32),
                        pltpu.VMEM((ktile, IDX_HEADS * qblk), F32),
                        pltpu.VMEM((N_KV_HEADS, g_rows, ktile), F32),
                        pltpu.VMEM((N_KV_HEADS, g_rows, ktile), F32)],
        compiler_params=pltpu.CompilerParams(dimension_semantics=("arbitrary", "arbitrary"),
                                             vmem_limit_bytes=VMEM_LIMIT),
        name="dsa",
    )(r3(qi), r3(wi), r3(q), r3(ki2), r3(k), r3(v))
    return out.reshape(B * L, ATT_WIDTH)


def _gelu_tanh(x):
    return 0.5 * x * (1.0 + jnp.tanh(math.sqrt(2.0 / math.pi) * (x + 0.044715 * (x * x * x))))


def _merge_kernel(x_ref, y_ref, yb_ref, gate_ref, wglu_ref, wa_ref, wb_ref, wo_ref, o_ref, y_scr):
    n_st, rt, d = x_ref.shape
    rows = n_st * rt
    _store_lane_blocks(y_scr, y_ref[...].astype(F32))
    y = jnp.concatenate([_strided_rows(y_scr, s, rt, n_st) for s in range(n_st)], axis=0)
    y = _gelu_tanh(y)
    ya = y * jax.nn.sigmoid(_dot(y.astype(BF16), wglu_ref[...]))
    gate = gate_ref[...].reshape(rows, 2 * d)
    ga, gb = gate[:, :D_MODEL], gate[:, D_MODEL:]
    merged = (jax.nn.sigmoid(ga) * _dot(ya.astype(BF16), wa_ref[...])
              + jax.nn.sigmoid(gb) * _dot(yb_ref[...].reshape(rows, ATT_WIDTH), wb_ref[...]))
    out = x_ref[...].reshape(rows, d) + _dot(merged.astype(BF16), wo_ref[...])
    o_ref[...] = out.reshape(n_st, rt, d)


def _merge(x, y, yb, gate, w_glu, w_a, w_b, w_out):
    n_st, l_seg, _ = x.shape
    rt = ROW_TILE
    ws = [w.astype(BF16) for w in (w_glu, w_a, w_b, w_out)]

    def streams(w):
        return pl.BlockSpec((n_st, rt, w), lambda i: (0, i, 0))

    return pl.pallas_call(
        _merge_kernel,
        grid=(l_seg // rt,),
        in_specs=[streams(D_MODEL), pl.BlockSpec((None, rt * n_st, D_MODEL), lambda i: (i, 0, 0)),
                  streams(ATT_WIDTH), streams(2 * D_MODEL)]
                 + [pl.BlockSpec(w.shape, lambda i: (0, 0)) for w in ws],
        out_specs=streams(D_MODEL),
        out_shape=jax.ShapeDtypeStruct((n_st, l_seg, D_MODEL), F32),
        scratch_shapes=[pltpu.VMEM((D_MODEL // LANES, rt * n_st, LANES), F32)],
        compiler_params=pltpu.CompilerParams(dimension_semantics=("arbitrary",), vmem_limit_bytes=VMEM_LIMIT),
        name="merge",
    )(x, y, yb, gate, *ws)


def _rms(x, g):
    return x * lax.rsqrt(jnp.mean(x * x, axis=-1, keepdims=True) + EPS) * g


def _ffn_kernel(x_ref, g2_ref, win_ref, wout_ref, gf_ref, o_ref, *, final_norm):
    x = x_ref[...]
    h = _rms(x, g2_ref[...]).astype(BF16)
    gu = _dot(h, win_ref[...])
    g, up = gu[:, :FFN_HIDDEN], gu[:, FFN_HIDDEN:]
    act = (g * jax.nn.sigmoid(g)) * up
    x = x + _dot(act.astype(BF16), wout_ref[...])
    o_ref[...] = _rms(x, gf_ref[...]) if final_norm else x


def _ffn(x1, g2, w_ffn_in, w_ffn_out, gf, final_norm, tile):
    T = x1.shape[0]
    win, wout = w_ffn_in.astype(BF16), w_ffn_out.astype(BF16)

    def row(w):
        return pl.BlockSpec((tile, w), lambda i: (i, 0))

    def full(a):
        return pl.BlockSpec(a.shape, lambda i: (0, 0))

    return pl.pallas_call(
        functools.partial(_ffn_kernel, final_norm=final_norm),
        grid=(T // tile,),
        in_specs=[row(D_MODEL), full(g2), full(win), full(wout), full(gf)],
        out_specs=row(D_MODEL),
        out_shape=jax.ShapeDtypeStruct((T, D_MODEL), F32),
        compiler_params=pltpu.CompilerParams(dimension_semantics=("arbitrary",), vmem_limit_bytes=VMEM_LIMIT),
        name="ffn",
    )(x1, g2, win, wout, gf)


def kernel(x, norm1_g, w_in, a_re, a_im, log_dt, b_re, b_im, c_re, c_im, d_skip, w_glu,
           w_branch_a, w_branch_b, w_out, norm2_g, w_ffn_in, w_ffn_out, norm_f_g):
    B, L, D = x.shape
    depth = norm1_g.shape[0]
    n_seg = S5_STREAMS // B
    assert D == D_MODEL and S5_STREAMS % B == 0 and L % 512 == 0 and L % (n_seg * ROW_TILE) == 0
    x = x.astype(F32)
    for i in range(depth):
        up, q, k, v, qi, ki2, wi, gate = _project(x, norm1_g[i][None, :].astype(F32), w_in[i])
        y = _s5_branch(up, B, a_re[i], a_im[i], log_dt[i], b_re[i], b_im[i], c_re[i], c_im[i], d_skip[i])
        yb = _dsa_branch(q, k, v, qi, ki2, wi, B, L)
        x1 = _merge(x.reshape(S5_STREAMS, L // n_seg, D), y, yb.reshape(S5_STREAMS, L // n_seg, ATT_WIDTH), gate,
                    w_glu[i], w_branch_a[i], w_branch_b[i], w_out[i])
        x = _ffn(x1.reshape(B * L, D), norm2_g[i][None, :].astype(F32), w_ffn_in[i], w_ffn_out[i],
                 norm_f_g[None, :].astype(F32), i == depth - 1, 256).reshape(B, L, D)
    return x
```

```python
import functools
import math

import jax
import jax.numpy as jnp
import numpy as np
from jax import lax
from jax.experimental import pallas as pl
from jax.experimental.pallas import tpu as pltpu

F32 = jnp.float32
BF16 = jnp.bfloat16
I32 = jnp.int32

D_MODEL = 1024
CHUNK = 64
EPS = 1e-6
NEG = -1e30

SSM_GROUP = 16
SSM_GROUPS = 64
SSM_STATE = 64
S5_CHUNK = 16
S5_STREAMS = 16
S5_LANE_GROUPS = 8
ROW_TILE = S5_CHUNK

N_HEADS = 8
HEAD_DIM = 128
N_KV_HEADS = 2
Q_PER_KV = N_HEADS // N_KV_HEADS
ATT_WIDTH = N_HEADS * HEAD_DIM
KV_WIDTH = N_KV_HEADS * HEAD_DIM
IDX_HEADS = 16
IDX_DIM = 64
TOPK_MAX = 256
ROPE_THETA = 500000.0
ATT_ROT = HEAD_DIM // 4
IDX_ROT = IDX_DIM // 4
FFN_HIDDEN = -(-8 * D_MODEL // (3 * 256)) * 256

SCORE_ROWS = 16
COUNT_ROWS = 32
SELECT_STEPS = 12
ATT_ROWS = 32

LANES = 128
INT_MIN = -(2 ** 31)

VMEM_LIMIT = 56 * 1024 * 1024


def _key_of_float(val):
    bits = int(np.float32(val).view(np.int32))
    return bits ^ ((bits >> 31) & 0x7FFFFFFF)


KEY_HALF_NEG = _key_of_float(0.5 * NEG)


def _dot(a, b):
    return jnp.dot(a, b, preferred_element_type=F32)


def _dot_nt(a, b):
    return lax.dot_general(a, b, (((1,), (1,)), ((), ())), preferred_element_type=F32)


def _split_hi_lo(a):
    hi = a.astype(BF16)
    lo = (a - hi.astype(F32)).astype(BF16)
    return hi, lo


def _rope(x, c, s1, s2, half):
    n = x.shape[-1]
    return x * c + pltpu.roll(x, half, 1) * s1 + pltpu.roll(x, n - half, 1) * s2


def _store_lane_blocks(scr, val):
    for j in range(scr.shape[0]):
        scr[j] = val[:, j * LANES:(j + 1) * LANES]


def _strided_rows(scr, start, size, stride):
    return jnp.concatenate([scr[j, pl.ds(start, size, stride=stride), :] for j in range(scr.shape[0])], axis=1)


def _proj_kernel(x_ref, g1_ref, wu_ref, wq_ref, wk_ref, wv_ref, wqi_ref, wsm_ref, wg_ref,
                 ca_ref, sa1_ref, sa2_ref, ci_ref, si1_ref, si2_ref, cs_ref, ss1_ref, ss2_ref,
                 u_ref, q_ref, k_ref, v_ref, qi_ref, ki2_ref, wi_ref, gate_ref, u_scr, *, batch):
    n_st, rt, d = x_ref.shape
    rows = n_st * rt
    x = x_ref[...].reshape(rows, d)
    h = x * lax.rsqrt(jnp.mean(x * x, axis=-1, keepdims=True) + EPS) * g1_ref[...]
    hb = h.astype(BF16)

    def put(ref, val, sl=slice(None)):
        ref[:, :, sl] = val.reshape(n_st, rt, val.shape[-1])

    def table(ref):
        t = ref[...].reshape(rows // batch, LANES)
        return jnp.concatenate([t] * batch, axis=0)

    _store_lane_blocks(u_scr, _dot(hb, wu_ref[...]))
    for t in range(rt):
        u_ref[t * n_st:(t + 1) * n_st, :] = _strided_rows(u_scr, t, n_st, rt).astype(BF16)

    v = _dot(hb, wv_ref[...]).astype(BF16)
    ones = jnp.ones((rows, HEAD_DIM), BF16)
    for hd in range(N_KV_HEADS):
        put(v_ref, v[:, hd * HEAD_DIM:(hd + 1) * HEAD_DIM], slice(2 * hd * HEAD_DIM, (2 * hd + 1) * HEAD_DIM))
        put(v_ref, ones, slice((2 * hd + 1) * HEAD_DIM, (2 * hd + 2) * HEAD_DIM))
    put(gate_ref, _dot(hb, wg_ref[...]))

    ca, sa1, sa2 = table(ca_ref), table(sa1_ref), table(sa2_ref)
    q = _dot(hb, wq_ref[...])
    scale = HEAD_DIM ** -0.5 * math.log2(math.e)
    for hd in range(N_HEADS):
        sl = slice(hd * HEAD_DIM, (hd + 1) * HEAD_DIM)
        put(q_ref, (_rope(q[:, sl], ca, sa1, sa2, ATT_ROT // 2) * scale).astype(BF16), sl)
    k = _dot(hb, wk_ref[...])
    for hd in range(N_KV_HEADS):
        sl = slice(hd * HEAD_DIM, (hd + 1) * HEAD_DIM)
        put(k_ref, _rope(k[:, sl], ca, sa1, sa2, ATT_ROT // 2).astype(BF16), sl)

    ci, si1, si2 = table(ci_ref), table(si1_ref), table(si2_ref)
    qi = _dot(hb, wqi_ref[...])
    for pr in range(IDX_HEADS * IDX_DIM // LANES):
        sl = slice(pr * LANES, (pr + 1) * LANES)
        put(qi_ref, _rope(qi[:, sl], ci, si1, si2, IDX_ROT // 2).astype(BF16), sl)

    sm = _dot(hb, wsm_ref[...])
    sm = _rope(sm, table(cs_ref), table(ss1_ref), table(ss2_ref), IDX_ROT // 2)
    lane = lax.broadcasted_iota(I32, sm.shape, 1)
    put(ki2_ref, jnp.where(lane < IDX_DIM, sm, pltpu.roll(sm, IDX_DIM, 1)).astype(BF16))
    put(wi_ref, sm * (IDX_DIM ** -0.5 * IDX_HEADS ** -0.5))


def _rope_tables(L, rot, period, active_lanes):
    half = rot // 2
    pos = jnp.arange(L, dtype=jnp.int32)
    inv_freq = ROPE_THETA ** (-jnp.arange(half, dtype=F32) / half)
    ang = pos.astype(F32)[:, None] * inv_freq[None, :]
    cos, sin = jnp.cos(ang), jnp.sin(ang)
    lane = np.arange(LANES)
    within = lane % period
    fidx = np.where(within < rot, within % half, 0)
    is_x1 = (within < half) & (lane < active_lanes)
    is_x2 = (within >= half) & (within < rot) & (lane < active_lanes)
    rot_lane = is_x1 | is_x2
    c = jnp.where(rot_lane[None, :], cos[:, fidx], 1.0)
    s1 = jnp.where(is_x2[None, :], sin[:, fidx], 0.0)
    s2 = jnp.where(is_x1[None, :], -sin[:, fidx], 0.0)
    return c.astype(F32), s1.astype(F32), s2.astype(F32)


def _project(x, g1, w_in):
    B, L, _ = x.shape
    n_seg = S5_STREAMS // B
    l_seg = L // n_seg
    rt = ROW_TILE
    o = 0
    parts = []
    for s in (D_MODEL, ATT_WIDTH, KV_WIDTH, KV_WIDTH, IDX_HEADS * IDX_DIM, IDX_DIM, IDX_HEADS, D_MODEL, D_MODEL):
        parts.append(w_in[:, o:o + s])
        o += s
    wu, wq, wk, wv, wqi, wki, wwi, wga, wgb = parts
    wsm = jnp.concatenate([wki, wwi, jnp.zeros((D_MODEL, LANES - IDX_DIM - IDX_HEADS), w_in.dtype)], axis=1)
    wg = jnp.concatenate([wga, wgb], axis=1)
    ws = [w.astype(BF16) for w in (wu, wq, wk, wv, wqi, wsm, wg)]
    tabs = (_rope_tables(L, ATT_ROT, HEAD_DIM, LANES)
            + _rope_tables(L, IDX_ROT, IDX_DIM, LANES)
            + _rope_tables(L, IDX_ROT, IDX_DIM, IDX_DIM))
    tabs = [t.reshape(n_seg, l_seg, LANES) for t in tabs]

    def streams(w):
        return pl.BlockSpec((S5_STREAMS, rt, w), lambda i: (0, i, 0))

    def full(a):
        return pl.BlockSpec(a.shape, lambda i: (0, 0))

    tab_spec = pl.BlockSpec((n_seg, rt, LANES), lambda i: (0, i, 0))
    widths = (ATT_WIDTH,
              KV_WIDTH,
              2 * KV_WIDTH,
              IDX_HEADS * IDX_DIM,
              LANES,
              LANES,
              2 * D_MODEL)
    dtypes = (BF16, BF16, BF16, BF16, BF16, F32, F32)
    out_shape = ((jax.ShapeDtypeStruct((l_seg // rt, rt * S5_STREAMS, D_MODEL), BF16),)
                 + tuple(jax.ShapeDtypeStruct((S5_STREAMS, l_seg, w), dt) for w, dt in zip(widths, dtypes)))
    out_specs = ((pl.BlockSpec((None, rt * S5_STREAMS, D_MODEL), lambda i: (i, 0, 0)),)
                 + tuple(streams(w) for w in widths))
    return pl.pallas_call(
        functools.partial(_proj_kernel, batch=B),
        grid=(l_seg // rt,),
        in_specs=[streams(D_MODEL), full(g1)] + [full(w) for w in ws] + [tab_spec] * 9,
        out_specs=out_specs,
        out_shape=out_shape,
        scratch_shapes=[pltpu.VMEM((D_MODEL // LANES, rt * S5_STREAMS, LANES), F32)],
        compiler_params=pltpu.CompilerParams(dimension_semantics=("arbitrary",), vmem_limit_bytes=VMEM_LIMIT),
        name="proj",
    )(x.reshape(S5_STREAMS, l_seg, D_MODEL), g1, *ws, *tabs)


def _s5_group_weights(ldt, ar, ai, btr, bti, cre, cim):
    lane = lax.broadcasted_iota(I32, (1, LANES), 1)
    lo = lane < SSM_STATE
    dt = jnp.exp(ldt)
    rho, th = ar * dt, ai * dt
    npow = S5_CHUNK + 1
    nn = lax.broadcasted_iota(I32, (24, LANES), 0).astype(F32)
    mag = jnp.exp(nn * rho)
    lc = mag * jnp.cos(nn * th)
    ls = mag * jnp.sin(nn * th)
    lam_pk = jnp.where(lo, lc, ls)
    lam_sw = jnp.where(lo, -ls, lc)
    num_re, num_im = lc[1:2] - 1.0, ls[1:2]
    den = ar * ar + ai * ai
    f_re = (num_re * ar + num_im * ai) / den
    f_im = (num_im * ar - num_re * ai) / den
    g_re = btr * f_re - bti * f_im
    g_im = btr * f_im + bti * f_re
    g_neg = jnp.where(lo, g_re, -g_im)
    w_pk = [cre * lam_pk[n:n + 1] + cim * lam_sw[n:n + 1] for n in range(npow)]
    w_state = [g_re * lam_pk[S5_CHUNK - 1 - ti:S5_CHUNK - ti] + g_im * lam_sw[S5_CHUNK - 1 - ti:S5_CHUNK - ti]
               for ti in range(S5_CHUNK)]
    a_p = jnp.where(lo, lam_pk[S5_CHUNK:S5_CHUNK + 1], lam_sw[S5_CHUNK:S5_CHUNK + 1])
    a_q = jnp.where(lo, lam_sw[S5_CHUNK:S5_CHUNK + 1], lam_pk[S5_CHUNK:S5_CHUNK + 1])
    return g_neg, w_pk, w_state, a_p, a_q


def _place(block, g):
    z = jnp.zeros_like(block)
    return jnp.concatenate([block if j == g else z for j in range(S5_LANE_GROUPS)], axis=1)


def _swap_halves(x):
    return jnp.concatenate([pltpu.roll(x[:, g * LANES:(g + 1) * LANES], SSM_STATE, 1)
                            for g in range(x.shape[1] // LANES)], axis=1)


def _s5_kernel(up_ref, are_ref, aim_ref, ldt_ref, btr_ref, bti_ref, cre_ref, cim_ref, dsk_ref,
               y_ref, s_ref, ssw_ref, hprev_ref, kbd_ref, wst_ref, wot_ref, *, n_steps, n_seg):
    ng = S5_LANE_GROUPS
    lane = lax.broadcasted_iota(I32, (1, LANES), 1)
    lo = lane < SSM_STATE
    weights = [_s5_group_weights(ldt_ref[g], are_ref[g], aim_ref[g], btr_ref[g], bti_ref[g], cre_ref[g], cim_ref[g])
               for g in range(ng)]
    gneg_blk = jnp.concatenate([_place(weights[g][0], g) for g in range(ng)], axis=0)
    for tau in range(S5_CHUNK):
        wt = jnp.concatenate([_place(weights[g][1][tau], g) for g in range(ng)], axis=0)
        kbd_ref[:, tau * LANES:(tau + 1) * LANES] = lax.dot_general(
            gneg_blk, wt, (((1,), (1,)), ((), ())), preferred_element_type=F32, precision=lax.Precision.HIGHEST)
    for ti in range(S5_CHUNK):
        wst_ref[ti * LANES:(ti + 1) * LANES, :] = jnp.concatenate(
            [_place(weights[g][2][ti], g) for g in range(ng)], axis=0).astype(BF16)
        wot_ref[ti * LANES:(ti + 1) * LANES, :] = jnp.concatenate(
            [_place(jnp.where(lo, weights[g][1][ti + 1], -weights[g][1][ti + 1]), g) for g in range(ng)],
            axis=0).astype(BF16)
    a_p = jnp.concatenate([weights[g][3] for g in range(ng)], axis=1)
    a_q = jnp.concatenate([weights[g][4] for g in range(ng)], axis=1)

    rows = n_steps * S5_STREAMS

    def chunk_rows(ti):
        return up_ref[:, ti * S5_STREAMS:(ti + 1) * S5_STREAMS, :].reshape(rows, LANES)

    lhs = jnp.concatenate([chunk_rows(ti) for ti in range(S5_CHUNK)], axis=1)
    s_all = _dot(lhs, wst_ref[...])
    s_ref[...] = s_all
    ssw_ref[...] = _swap_halves(s_all)

    def cmul(x, p, q):
        return x * p + _swap_halves(x) * q

    def step(h, hs, i):
        rows_i = pl.ds(pl.multiple_of(i * S5_STREAMS, S5_STREAMS), S5_STREAMS)
        return h * a_p + hs * a_q + s_ref[rows_i, :], hs * a_p - h * a_q + ssw_ref[rows_i, :]

    zero = jnp.zeros((S5_STREAMS, ng * LANES), F32)
    z, _ = lax.fori_loop(0, n_steps, lambda i, c: step(c[0], c[1], i), (zero, zero))

    lo_all = jnp.concatenate([lo] * ng, axis=1)

    def factors(zpk):
        sw = _swap_halves(zpk)
        return jnp.where(lo_all, zpk, sw), jnp.where(lo_all, -sw, zpk)

    base, seg, n = jnp.where(lo_all, a_p, a_q), None, n_steps
    while n:
        if n & 1:
            seg = base if seg is None else cmul(seg, *factors(base))
        n >>= 1
        if n:
            base = cmul(base, *factors(base))
    seg_p, seg_q = factors(seg)

    row_id = lax.broadcasted_iota(I32, (S5_STREAMS, ng * LANES), 0)
    init = zero
    prev = None
    for s in range(S5_STREAMS):
        if s % n_seg == 0:
            cur = jnp.zeros((1, ng * LANES), F32)
        else:
            cur = cmul(prev, seg_p, seg_q) + z[s - 1:s]
            init = jnp.where(row_id == s, cur, init)
        prev = cur

    def pass2(i, c):
        hprev_ref[pl.ds(pl.multiple_of(i * S5_STREAMS, S5_STREAMS), S5_STREAMS), :] = c[0]
        return step(c[0], c[1], i)

    lax.fori_loop(0, n_steps, pass2, (init, _swap_halves(init)))

    hp = hprev_ref[...].astype(BF16)
    dsk = dsk_ref[...]
    zblk = jnp.zeros((LANES, LANES), F32)
    for tp in range(S5_CHUNK // 2):
        kdim = (2 * tp + 2) * LANES
        cols = []
        for t2 in (2 * tp, 2 * tp + 1):
            cols.append(jnp.concatenate(
                [kbd_ref[:, (t2 - ti) * LANES:(t2 - ti + 1) * LANES] if ti <= t2 else zblk
                 for ti in range(2 * tp + 2)], axis=0))
        slab = jnp.concatenate(cols, axis=1).astype(BF16)
        y = _dot(lhs[:, :kdim], slab) + _dot_nt(hp, wot_ref[2 * tp * LANES:(2 * tp + 2) * LANES, :])
        for half in range(2):
            t2 = 2 * tp + half
            u_t = up_ref[:, t2 * S5_STREAMS:(t2 + 1) * S5_STREAMS, :].astype(F32)
            y_t = y[:, half * LANES:(half + 1) * LANES].reshape(n_steps, S5_STREAMS, LANES) + u_t * dsk
            y_ref[:, t2 * S5_STREAMS:(t2 + 1) * S5_STREAMS, :] = y_t.astype(y_ref.dtype)


def _s5_branch(up, B, a_re, a_im, log_dt, b_re, b_im, c_re, c_im, d_skip):
    G, C = SSM_GROUPS, SSM_GROUP
    n_steps = up.shape[0]
    n_seg = S5_STREAMS // B
    rows = n_steps * S5_STREAMS
    ng = S5_LANE_GROUPS

    def dup(a):
        return jnp.concatenate([a, a], axis=-1).astype(F32)

    are2 = dup(a_re)[:, None, :]
    aim2 = dup(a_im)[:, None, :]
    ldt = log_dt.astype(F32)[:, None, None]
    btr2 = dup(jnp.swapaxes(b_re, 1, 2))
    bti2 = dup(jnp.swapaxes(b_im, 1, 2))
    cre2 = dup(c_re)
    cim2 = dup(c_im)
    dsk = d_skip.astype(F32).reshape(G // ng, 1, ng * C)

    def per_block(shape):
        return pl.BlockSpec((ng,) + shape, lambda i: (i, 0, 0))

    act = pl.BlockSpec((n_steps, S5_CHUNK * S5_STREAMS, LANES), lambda i: (0, 0, i))
    return pl.pallas_call(
        functools.partial(_s5_kernel, n_steps=n_steps, n_seg=n_seg),
        grid=(G // ng,),
        in_specs=[act, per_block((1, LANES)), per_block((1, LANES)), per_block((1, 1)),
                  per_block((C, LANES)), per_block((C, LANES)), per_block((C, LANES)), per_block((C, LANES)),
                  pl.BlockSpec((None, 1, LANES), lambda i: (i, 0, 0))],
        out_specs=act,
        out_shape=jax.ShapeDtypeStruct(up.shape, BF16),
        scratch_shapes=[pltpu.VMEM((rows, ng * LANES), F32),
                        pltpu.VMEM((rows, ng * LANES), F32),
                        pltpu.VMEM((rows, ng * LANES), F32),
                        pltpu.VMEM((LANES, S5_CHUNK * LANES), F32),
                        pltpu.VMEM((S5_CHUNK * LANES, ng * LANES), BF16),
                        pltpu.VMEM((S5_CHUNK * LANES, ng * LANES), BF16)],
        compiler_params=pltpu.CompilerParams(dimension_semantics=("arbitrary",), vmem_limit_bytes=VMEM_LIMIT),
        name="s5",
    )(up, are2, aim2, ldt, btr2, bti2, cre2, cim2, dsk)


def _float_to_key(x):
    bits = lax.bitcast_convert_type(x, I32)
    return bits ^ (lax.shift_right_arithmetic(bits, 31) & 0x7FFFFFFF)


def _key_to_float(key):
    return lax.bitcast_convert_type(key ^ (lax.shift_right_arithmetic(key, 31) & 0x7FFFFFFF), F32)


def _probit(p):
    t = jnp.sqrt(-2.0 * jnp.log(jnp.minimum(p, 1.0 - p)))
    z = t - ((0.010328 * t + 0.802853) * t + 2.515517) / (((0.001308 * t + 0.189269) * t + 1.432788) * t + 1.0)
    return jnp.where(p < 0.5, -z, z)


def _count_ge(sc_ref, trial, n_tiles, ktile):
    def count_tile(t, acc):
        off = pl.multiple_of(t * ktile, ktile)
        for j in range(ktile // COUNT_ROWS):
            rows = sc_ref[pl.ds(off + j * COUNT_ROWS, COUNT_ROWS), :]
            acc = jnp.where(rows >= trial, acc + 1, acc)
        return acc

    acc = lax.fori_loop(0, n_tiles, count_tile, jnp.zeros((COUNT_ROWS, LANES), I32))
    return jnp.sum(acc, axis=0, keepdims=True).astype(F32)


def _select_threshold(sc_ref, s_min, s_max, n_allowed, n_tiles, k_sel, ktile):
    kf = float(k_sel)
    n_eff = jnp.maximum(n_allowed, kf + 1.0)
    z_t = _probit(1.0 - (kf - 0.5) / n_eff)

    def z_of(c):
        return _probit(jnp.clip(1.0 - c / n_eff, 0.5 / n_eff, 1.0 - 0.5 / n_eff))

    def body(st):
        it, t_lo, t_hi, c_lo, c_hi, z_lo, z_hi, w_lo, w_hi, side, done, thr = st
        g_lo = (z_lo - z_t) * w_lo
        g_hi = (z_hi - z_t) * w_hi
        t = t_lo + (t_hi - t_lo) * jnp.clip(g_lo / (g_lo - g_hi), 0.0, 1.0)
        k_lo, k_hi, k_t = _float_to_key(t_lo), _float_to_key(t_hi), _float_to_key(t)
        mid = lax.shift_right_arithmetic(k_lo, 1) + lax.shift_right_arithmetic(k_hi, 1) + (k_lo & k_hi & 1)
        use_mid = (k_t <= k_lo) | (k_t >= k_hi) | (c_lo - c_hi <= 4.0) | (it >= 20)
        t = _key_to_float(jnp.where(use_mid, mid, k_t))
        c = _count_ge(sc_ref, t, n_tiles, ktile)
        active = done == 0
        hit = active & (c == kf)
        new_lo = active & (c >= kf)
        new_hi = active & (c < kf)
        z_c = z_of(c)
        w_hi = jnp.where(new_lo, jnp.where(side > 0, 0.5 * w_hi, w_hi), 1.0)
        w_lo = jnp.where(new_hi, jnp.where(side < 0, 0.5 * w_lo, w_lo), 1.0)
        side = jnp.where(new_lo, 1.0, jnp.where(new_hi, -1.0, side))
        t_lo = jnp.where(new_lo, t, t_lo)
        c_lo = jnp.where(new_lo, c, c_lo)
        z_lo = jnp.where(new_lo, z_c, z_lo)
        t_hi = jnp.where(new_hi, t, t_hi)
        c_hi = jnp.where(new_hi, c, c_hi)
        z_hi = jnp.where(new_hi, z_c, z_hi)
        adjacent = active & (_float_to_key(t_hi) - 1 <= _float_to_key(t_lo))
        thr = jnp.where(hit, t, jnp.where(adjacent, t_lo, thr))
        done = jnp.where(hit | adjacent, 1, done)
        return it + 1, t_lo, t_hi, c_lo, c_hi, z_lo, z_hi, w_lo, w_hi, side, done, thr

    def cond(st):
        return jnp.logical_and(st[0] < 64, jnp.min(st[10]) < 1)

    few = n_allowed <= kf
    ones = jnp.ones((1, LANES), F32)
    zeros = jnp.zeros((1, LANES), F32)
    init = (jnp.int32(0), s_min, _key_to_float(_float_to_key(s_max) + 1), n_allowed, zeros,
            z_of(n_allowed), z_of(zeros), ones, ones, zeros,
            few.astype(I32), jnp.where(few, -jnp.inf, s_min))
    st = lax.fori_loop(0, SELECT_STEPS, lambda _, s: body(s), init)
    return lax.while_loop(cond, body, st)[11]


def _dsa_kernel(qi_ref, wi_ref, q_ref, ki2_ref, k_ref, v_ref, o_ref,
                sc_ref, qm_ref, qg_ref, m_ref, acc_ref, rel0_ref, rel1_ref, lg0_ref, lg1_ref, *, k_sel, qblk, ktile):
    assert qblk == LANES
    qb = pl.program_id(1)
    n_tiles = lax.div(qb * qblk + qblk + ktile - 1, ktile)
    lane = lax.broadcasted_iota(I32, (qblk, LANES), 1)
    n_lt = ktile // LANES

    for hd in range(IDX_HEADS):
        pair = qi_ref[:, (hd // 2) * LANES:(hd // 2 + 1) * LANES]
        msk = (lane < IDX_DIM) if hd % 2 == 0 else (lane >= IDX_DIM)
        qm_ref[hd * qblk:(hd + 1) * qblk, :] = jnp.where(msk, pair, jnp.zeros_like(pair))
    eye = (lax.broadcasted_iota(I32, (qblk, LANES), 0) == lane).astype(BF16)
    for hd in range(N_HEADS):
        g, r = divmod(hd, Q_PER_KV)
        qg_ref[g, r * qblk:(r + 1) * qblk, 0:HEAD_DIM] = q_ref[:, hd * HEAD_DIM:(hd + 1) * HEAD_DIM]
        qg_ref[g, r * qblk:(r + 1) * qblk, HEAD_DIM:2 * HEAD_DIM] = eye

    w_t = wi_ref[...].T
    q_chunk = lax.shift_right_logical(qb * qblk + lax.broadcasted_iota(I32, (1, LANES), 1), 6)
    n_allowed = ((q_chunk + 1) * CHUNK).astype(F32)

    rel_refs = (rel0_ref, rel1_ref)
    lg_refs = (lg0_ref, lg1_ref)

    def tile_offset(t):
        return pl.multiple_of(jnp.minimum(t, n_tiles - 1) * ktile, ktile)

    def issue_logits(slot, t):
        rel_refs[slot][...] = _dot_nt(ki2_ref[pl.ds(tile_offset(t), ktile), :], qm_ref[...])

    def score_tile(slot, t, carry):
        mx, mn = carry
        off = tile_offset(t)
        for c in range(ktile // SCORE_ROWS):
            r0 = c * SCORE_ROWS
            sc = jnp.zeros((SCORE_ROWS, LANES), F32)
            for hd in range(IDX_HEADS):
                rel = rel_refs[slot][r0:r0 + SCORE_ROWS, hd * qblk:(hd + 1) * qblk]
                sc = sc + jnp.maximum(rel, 0.0) * w_t[IDX_DIM + hd:IDX_DIM + hd + 1, :]
            k_chunk = lax.shift_right_logical(off + r0 + lax.broadcasted_iota(I32, (SCORE_ROWS, 1), 0), 6)
            ok = k_chunk <= q_chunk
            sc_ref[pl.ds(off + r0, SCORE_ROWS), :] = jnp.where(ok, sc, NEG)
            mx = jnp.maximum(mx, jnp.where(ok, sc, -jnp.inf))
            mn = jnp.minimum(mn, jnp.where(ok, sc, jnp.inf))
        return mx, mn

    def score_pair(p, carry):
        t0 = 2 * p
        issue_logits(1, t0 + 1)
        carry = score_tile(0, t0, carry)
        issue_logits(0, t0 + 2)
        return score_tile(1, t0 + 1, carry)

    issue_logits(0, 0)
    mx, mn = lax.fori_loop(0, lax.div(n_tiles + 1, 2), score_pair,
                           (jnp.full((SCORE_ROWS, LANES), -jnp.inf, F32), jnp.full((SCORE_ROWS, LANES), jnp.inf, F32)))
    s_max = jnp.max(mx, axis=0, keepdims=True)
    s_min = jnp.min(mn, axis=0, keepdims=True)

    thr = _select_threshold(sc_ref, s_min, s_max, n_allowed, n_tiles, k_sel, ktile)
    thr = jnp.maximum(thr, _key_to_float(jnp.full((1, LANES), KEY_HALF_NEG + 1, I32)))

    m_ref[...] = jnp.full(m_ref.shape, -jnp.inf, F32)
    acc_ref[...] = jnp.zeros(acc_ref.shape, F32)
    g_rows = Q_PER_KV * qblk

    def selection_bias(t):
        thr_t = jnp.where(t < n_tiles, thr, jnp.inf)
        return jnp.where(sc_ref[pl.ds(tile_offset(t), ktile), :] >= thr_t, 0.0, NEG).astype(BF16)

    def issue_qk(slot, t, g, bias_t):
        kb = jnp.concatenate([k_ref[pl.ds(tile_offset(t), ktile), g * HEAD_DIM:(g + 1) * HEAD_DIM], bias_t], axis=1)
        lg_refs[slot][g] = _dot_nt(qg_ref[g], kb)

    def attend_half(cur, t):
        off = tile_offset(t)
        bias_next = selection_bias(t + 1)
        for g in range(N_KV_HEADS):
            issue_qk(1 - cur, t + 1, g, bias_next)
            slot = cur
            ps, alphas = [], []
            for c in range(g_rows // ATT_ROWS):
                r0 = c * ATT_ROWS
                s = lg_refs[slot][g, r0:r0 + ATT_ROWS, :]
                m_cur = s[:, 0:LANES]
                for j in range(1, n_lt):
                    m_cur = jnp.maximum(m_cur, s[:, j * LANES:(j + 1) * LANES])
                m_old = m_ref[g, r0:r0 + ATT_ROWS, :]
                m_new = jnp.maximum(m_old, jnp.max(m_cur, axis=1, keepdims=True))
                alphas.append(jnp.exp2(m_old - m_new))
                m_ref[g, r0:r0 + ATT_ROWS, :] = m_new
                ps.append(jnp.exp2(s - jnp.concatenate([m_new] * n_lt, axis=1)).astype(BF16))
            alpha = jnp.concatenate(alphas, axis=0)
            pv = _dot(jnp.concatenate(ps, axis=0), v_ref[pl.ds(off, ktile), 2 * g * HEAD_DIM:(2 * g + 2) * HEAD_DIM])
            acc_ref[g] = acc_ref[g] * jnp.concatenate([alpha, alpha], axis=1) + pv

    def attend_pair(p, carry):
        attend_half(0, 2 * p)
        attend_half(1, 2 * p + 1)
        return carry

    bias_0 = selection_bias(0)
    for g in range(N_KV_HEADS):
        issue_qk(0, 0, g, bias_0)
    lax.fori_loop(0, lax.div(n_tiles + 1, 2), attend_pair, 0)
    for hd in range(N_HEADS):
        g, r = divmod(hd, Q_PER_KV)
        a = acc_ref[g, r * qblk:(r + 1) * qblk, :]
        o_ref[:, hd * HEAD_DIM:(hd + 1) * HEAD_DIM] = (a[:, :HEAD_DIM] / a[:, HEAD_DIM:]).astype(o_ref.dtype)


def _dsa_branch(q, k, v, qi, ki2, wi, B, L, qblk=128, ktile=512):
    k_sel = min(TOPK_MAX, L // 4)
    nq = L // qblk
    g_rows = Q_PER_KV * qblk
    assert L % ktile == 0

    def qrow(w):
        return pl.BlockSpec((None, qblk, w), lambda b, i: (b, i, 0))

    def whole(w):
        return pl.BlockSpec((None, L, w), lambda b, i: (b, 0, 0))

    r3 = lambda a: a.reshape(B, L, a.shape[-1])
    out = pl.pallas_call(
        functools.partial(_dsa_kernel, k_sel=k_sel, qblk=qblk, ktile=ktile),
        grid=(B, nq),
        in_specs=[qrow(IDX_HEADS * IDX_DIM), qrow(LANES), qrow(ATT_WIDTH), whole(LANES), whole(KV_WIDTH),
                  whole(2 * KV_WIDTH)],
        out_specs=qrow(ATT_WIDTH),
        out_shape=jax.ShapeDtypeStruct((B, L, ATT_WIDTH), BF16),
        scratch_shapes=[pltpu.VMEM((L, qblk), F32),
                        pltpu.VMEM((IDX_HEADS * qblk, LANES), BF16),
                        pltpu.VMEM((N_KV_HEADS, g_rows, 2 * HEAD_DIM), BF16),
                        pltpu.VMEM((N_KV_HEADS, g_rows, LANES), F32),
                        pltpu.VMEM((N_KV_HEADS, g_rows, 2 * HEAD_DIM), F32),
                        pltpu.VMEM((ktile, IDX_HEADS * qblk), F32),
                        pltpu.VMEM((ktile, IDX_HEADS * qblk), F32),
                        pltpu.VMEM((N_KV_HEADS, g_rows, ktile), F32),
                        pltpu.VMEM((N_KV_HEADS, g_rows, ktile), F32)],
        compiler_params=pltpu.CompilerParams(dimension_semantics=("arbitrary", "arbitrary"),
                                             vmem_limit_bytes=VMEM_LIMIT),
        name="dsa",
    )(r3(qi), r3(wi), r3(q), r3(ki2), r3(k), r3(v))
    return out.reshape(B * L, ATT_WIDTH)


def _gelu_tanh(x):
    return 0.5 * x * (1.0 + jnp.tanh(math.sqrt(2.0 / math.pi) * (x + 0.044715 * (x * x * x))))


def _merge_kernel(x_ref, y_ref, yb_ref, gate_ref, wglu_ref, wa_ref, wb_ref, wo_ref, o_ref, y_scr):
    n_st, rt, d = x_ref.shape
    rows = n_st * rt
    _store_lane_blocks(y_scr, y_ref[...].astype(F32))
    y = jnp.concatenate([_strided_rows(y_scr, s, rt, n_st) for s in range(n_st)], axis=0)
    y = _gelu_tanh(y)
    ya = y * jax.nn.sigmoid(_dot(y.astype(BF16), wglu_ref[...]))
    gate = gate_ref[...].reshape(rows, 2 * d)
    ga, gb = gate[:, :D_MODEL], gate[:, D_MODEL:]
    merged = (jax.nn.sigmoid(ga) * _dot(ya.astype(BF16), wa_ref[...])
              + jax.nn.sigmoid(gb) * _dot(yb_ref[...].reshape(rows, ATT_WIDTH), wb_ref[...]))
    out = x_ref[...].reshape(rows, d) + _dot(merged.astype(BF16), wo_ref[...])
    o_ref[...] = out.reshape(n_st, rt, d)


def _merge(x, y, yb, gate, w_glu, w_a, w_b, w_out):
    n_st, l_seg, _ = x.shape
    rt = ROW_TILE
    ws = [w.astype(BF16) for w in (w_glu, w_a, w_b, w_out)]

    def streams(w):
        return pl.BlockSpec((n_st, rt, w), lambda i: (0, i, 0))

    return pl.pallas_call(
        _merge_kernel,
        grid=(l_seg // rt,),
        in_specs=[streams(D_MODEL), pl.BlockSpec((None, rt * n_st, D_MODEL), lambda i: (i, 0, 0)),
                  streams(ATT_WIDTH), streams(2 * D_MODEL)]
                 + [pl.BlockSpec(w.shape, lambda i: (0, 0)) for w in ws],
        out_specs=streams(D_MODEL),
        out_shape=jax.ShapeDtypeStruct((n_st, l_seg, D_MODEL), F32),
        scratch_shapes=[pltpu.VMEM((D_MODEL // LANES, rt * n_st, LANES), F32)],
        compiler_params=pltpu.CompilerParams(dimension_semantics=("arbitrary",), vmem_limit_bytes=VMEM_LIMIT),
        name="merge",
    )(x, y, yb, gate, *ws)


def _rms(x, g):
    return x * lax.rsqrt(jnp.mean(x * x, axis=-1, keepdims=True) + EPS) * g


def _ffn_kernel(x_ref, g2_ref, win_ref, wout_ref, gf_ref, o_ref, *, final_norm):
    x = x_ref[...]
    h = _rms(x, g2_ref[...]).astype(BF16)
    gu = _dot(h, win_ref[...])
    g, up = gu[:, :FFN_HIDDEN], gu[:, FFN_HIDDEN:]
    act = (g * jax.nn.sigmoid(g)) * up
    x = x + _dot(act.astype(BF16), wout_ref[...])
    o_ref[...] = _rms(x, gf_ref[...]) if final_norm else x


def _ffn(x1, g2, w_ffn_in, w_ffn_out, gf, final_norm, tile):
    T = x1.shape[0]
    win, wout = w_ffn_in.astype(BF16), w_ffn_out.astype(BF16)

    def row(w):
        return pl.BlockSpec((tile, w), lambda i: (i, 0))

    def full(a):
        return pl.BlockSpec(a.shape, lambda i: (0, 0))

    return pl.pallas_call(
        functools.partial(_ffn_kernel, final_norm=final_norm),
        grid=(T // tile,),
        in_specs=[row(D_MODEL), full(g2), full(win), full(wout), full(gf)],
        out_specs=row(D_MODEL),
        out_shape=jax.ShapeDtypeStruct((T, D_MODEL), F32),
        compiler_params=pltpu.CompilerParams(dimension_semantics=("arbitrary",), vmem_limit_bytes=VMEM_LIMIT),
        name="ffn",
    )(x1, g2, win, wout, gf)


def kernel(x, norm1_g, w_in, a_re, a_im, log_dt, b_re, b_im, c_re, c_im, d_skip, w_glu,
           w_branch_a, w_branch_b, w_out, norm2_g, w_ffn_in, w_ffn_out, norm_f_g):
    B, L, D = x.shape
    depth = norm1_g.shape[0]
    n_seg = S5_STREAMS // B
    assert D == D_MODEL and S5_STREAMS % B == 0 and L % 512 == 0 and L % (n_seg * ROW_TILE) == 0
    x = x.astype(F32)
    for i in range(depth):
        up, q, k, v, qi, ki2, wi, gate = _project(x, norm1_g[i][None, :].astype(F32), w_in[i])
        y = _s5_branch(up, B, a_re[i], a_im[i], log_dt[i], b_re[i], b_im[i], c_re[i], c_im[i], d_skip[i])
        yb = _dsa_branch(q, k, v, qi, ki2, wi, B, L)
        x1 = _merge(x.reshape(S5_STREAMS, L // n_seg, D), y, yb.reshape(S5_STREAMS, L // n_seg, ATT_WIDTH), gate,
                    w_glu[i], w_branch_a[i], w_branch_b[i], w_out[i])
        x = _ffn(x1.reshape(B * L, D), norm2_g[i][None, :].astype(F32), w_ffn_in[i], w_ffn_out[i],
                 norm_f_g[None, :].astype(F32), i == depth - 1, 256).reshape(B, L, D)
    return x
```

```python
import functools
import math

import jax
import jax.numpy as jnp
import numpy as np
from jax import lax
from jax.experimental import pallas as pl
from jax.experimental.pallas import tpu as pltpu

F32 = jnp.float32
BF16 = jnp.bfloat16
I32 = jnp.int32

D_MODEL = 1024
CHUNK = 64
EPS = 1e-6
NEG = -1e30

SSM_GROUP = 16
SSM_GROUPS = 64
SSM_STATE = 64
S5_CHUNK = 16
S5_STREAMS = 16
S5_LANE_GROUPS = 8
ROW_TILE = S5_CHUNK
FFN_ROWS = 256

N_HEADS = 8
HEAD_DIM = 128
N_KV_HEADS = 2
Q_PER_KV = N_HEADS // N_KV_HEADS
ATT_WIDTH = N_HEADS * HEAD_DIM
KV_WIDTH = N_KV_HEADS * HEAD_DIM
IDX_HEADS = 16
IDX_DIM = 64
TOPK_MAX = 256
ROPE_THETA = 500000.0
ATT_ROT = HEAD_DIM // 4
IDX_ROT = IDX_DIM // 4
FFN_HIDDEN = -(-8 * D_MODEL // (3 * 256)) * 256

SCORE_ROWS = 16
COUNT_ROWS = 32
SELECT_STEPS = 12
ATT_ROWS = 32

LANES = 128
INT_MIN = -(2 ** 31)

VMEM_LIMIT = 56 * 1024 * 1024


def _key_of_float(val):
    bits = int(np.float32(val).view(np.int32))
    return bits ^ ((bits >> 31) & 0x7FFFFFFF)


KEY_HALF_NEG = _key_of_float(0.5 * NEG)


def _dot(a, b):
    return jnp.dot(a, b, preferred_element_type=F32)


def _dot_nt(a, b):
    return lax.dot_general(a, b, (((1,), (1,)), ((), ())), preferred_element_type=F32)


def _split_hi_lo(a):
    hi = a.astype(BF16)
    lo = (a - hi.astype(F32)).astype(BF16)
    return hi, lo


def _rope(x, c, s1, s2, half):
    n = x.shape[-1]
    return x * c + pltpu.roll(x, half, 1) * s1 + pltpu.roll(x, n - half, 1) * s2


def _store_lane_blocks(scr, val):
    for j in range(scr.shape[0]):
        scr[j] = val[:, j * LANES:(j + 1) * LANES]


def _strided_rows(scr, start, size, stride):
    return jnp.concatenate([scr[j, pl.ds(start, size, stride=stride), :] for j in range(scr.shape[0])], axis=1)


def _proj_kernel(x_ref, g1_ref, wu_ref, wq_ref, wk_ref, wv_ref, wqi_ref, wsm_ref, wg_ref,
                 ca_ref, sa1_ref, sa2_ref, ci_ref, si1_ref, si2_ref, cs_ref, ss1_ref, ss2_ref,
                 u_ref, q_ref, k_ref, v_ref, qi_ref, ki2_ref, wi_ref, gate_ref, u_scr, *, batch):
    n_st, rt, d = x_ref.shape
    rows = n_st * rt
    x = x_ref[...].reshape(rows, d)
    h = x * lax.rsqrt(jnp.mean(x * x, axis=-1, keepdims=True) + EPS) * g1_ref[...]
    hb = h.astype(BF16)

    def put(ref, val, sl=slice(None)):
        ref[:, :, sl] = val.reshape(n_st, rt, val.shape[-1])

    def table(ref):
        t = ref[...].reshape(rows // batch, LANES)
        return jnp.concatenate([t] * batch, axis=0)

    _store_lane_blocks(u_scr, _dot(hb, wu_ref[...]))
    for t in range(rt):
        u_ref[t * n_st:(t + 1) * n_st, :] = _strided_rows(u_scr, t, n_st, rt).astype(BF16)

    v = _dot(hb, wv_ref[...]).astype(BF16)
    ones = jnp.ones((rows, HEAD_DIM), BF16)
    for hd in range(N_KV_HEADS):
        put(v_ref, v[:, hd * HEAD_DIM:(hd + 1) * HEAD_DIM], slice(2 * hd * HEAD_DIM, (2 * hd + 1) * HEAD_DIM))
        put(v_ref, ones, slice((2 * hd + 1) * HEAD_DIM, (2 * hd + 2) * HEAD_DIM))
    put(gate_ref, _dot(hb, wg_ref[...]))

    ca, sa1, sa2 = table(ca_ref), table(sa1_ref), table(sa2_ref)
    q = _dot(hb, wq_ref[...])
    scale = HEAD_DIM ** -0.5 * math.log2(math.e)
    for hd in range(N_HEADS):
        sl = slice(hd * HEAD_DIM, (hd + 1) * HEAD_DIM)
        put(q_ref, (_rope(q[:, sl], ca, sa1, sa2, ATT_ROT // 2) * scale).astype(BF16), sl)
    k = _dot(hb, wk_ref[...])
    for hd in range(N_KV_HEADS):
        sl = slice(hd * HEAD_DIM, (hd + 1) * HEAD_DIM)
        put(k_ref, _rope(k[:, sl], ca, sa1, sa2, ATT_ROT // 2).astype(BF16), sl)

    ci, si1, si2 = table(ci_ref), table(si1_ref), table(si2_ref)
    qi = _dot(hb, wqi_ref[...])
    for pr in range(IDX_HEADS * IDX_DIM // LANES):
        sl = slice(pr * LANES, (pr + 1) * LANES)
        put(qi_ref, _rope(qi[:, sl], ci, si1, si2, IDX_ROT // 2).astype(BF16), sl)

    sm = _dot(hb, wsm_ref[...])
    sm = _rope(sm, table(cs_ref), table(ss1_ref), table(ss2_ref), IDX_ROT // 2)
    lane = lax.broadcasted_iota(I32, sm.shape, 1)
    put(ki2_ref, jnp.where(lane < IDX_DIM, sm, pltpu.roll(sm, IDX_DIM, 1)).astype(BF16))
    put(wi_ref, sm * (IDX_DIM ** -0.5 * IDX_HEADS ** -0.5))


def _rope_tables(L, rot, period, active_lanes):
    half = rot // 2
    pos = jnp.arange(L, dtype=jnp.int32)
    inv_freq = ROPE_THETA ** (-jnp.arange(half, dtype=F32) / half)
    ang = pos.astype(F32)[:, None] * inv_freq[None, :]
    cos, sin = jnp.cos(ang), jnp.sin(ang)
    lane = np.arange(LANES)
    within = lane % period
    fidx = np.where(within < rot, within % half, 0)
    is_x1 = (within < half) & (lane < active_lanes)
    is_x2 = (within >= half) & (within < rot) & (lane < active_lanes)
    rot_lane = is_x1 | is_x2
    c = jnp.where(rot_lane[None, :], cos[:, fidx], 1.0)
    s1 = jnp.where(is_x2[None, :], sin[:, fidx], 0.0)
    s2 = jnp.where(is_x1[None, :], -sin[:, fidx], 0.0)
    return c.astype(F32), s1.astype(F32), s2.astype(F32)


def _project(x, g1, w_in):
    B, L, _ = x.shape
    n_seg = S5_STREAMS // B
    l_seg = L // n_seg
    rt = ROW_TILE
    o = 0
    parts = []
    for s in (D_MODEL, ATT_WIDTH, KV_WIDTH, KV_WIDTH, IDX_HEADS * IDX_DIM, IDX_DIM, IDX_HEADS, D_MODEL, D_MODEL):
        parts.append(w_in[:, o:o + s])
        o += s
    wu, wq, wk, wv, wqi, wki, wwi, wga, wgb = parts
    wsm = jnp.concatenate([wki, wwi, jnp.zeros((D_MODEL, LANES - IDX_DIM - IDX_HEADS), w_in.dtype)], axis=1)
    wg = jnp.concatenate([wga, wgb], axis=1)
    ws = [w.astype(BF16) for w in (wu, wq, wk, wv, wqi, wsm, wg)]
    tabs = (_rope_tables(L, ATT_ROT, HEAD_DIM, LANES)
            + _rope_tables(L, IDX_ROT, IDX_DIM, LANES)
            + _rope_tables(L, IDX_ROT, IDX_DIM, IDX_DIM))
    tabs = [t.reshape(n_seg, l_seg, LANES) for t in tabs]

    def streams(w):
        return pl.BlockSpec((S5_STREAMS, rt, w), lambda i: (0, i, 0))

    def full(a):
        return pl.BlockSpec(a.shape, lambda i: (0, 0))

    tab_spec = pl.BlockSpec((n_seg, rt, LANES), lambda i: (0, i, 0))
    widths = (ATT_WIDTH,
              KV_WIDTH,
              2 * KV_WIDTH,
              IDX_HEADS * IDX_DIM,
              LANES,
              LANES,
              2 * D_MODEL)
    dtypes = (BF16, BF16, BF16, BF16, BF16, F32, F32)
    out_shape = ((jax.ShapeDtypeStruct((l_seg // rt, rt * S5_STREAMS, D_MODEL), BF16),)
                 + tuple(jax.ShapeDtypeStruct((S5_STREAMS, l_seg, w), dt) for w, dt in zip(widths, dtypes)))
    out_specs = ((pl.BlockSpec((None, rt * S5_STREAMS, D_MODEL), lambda i: (i, 0, 0)),)
                 + tuple(streams(w) for w in widths))
    return pl.pallas_call(
        functools.partial(_proj_kernel, batch=B),
        grid=(l_seg // rt,),
        in_specs=[streams(D_MODEL), full(g1)] + [full(w) for w in ws] + [tab_spec] * 9,
        out_specs=out_specs,
        out_shape=out_shape,
        scratch_shapes=[pltpu.VMEM((D_MODEL // LANES, rt * S5_STREAMS, LANES), F32)],
        compiler_params=pltpu.CompilerParams(dimension_semantics=("arbitrary",), vmem_limit_bytes=VMEM_LIMIT),
        name="proj",
    )(x.reshape(S5_STREAMS, l_seg, D_MODEL), g1, *ws, *tabs)


def _s5_group_weights(ldt, ar, ai, btr, bti, cre, cim):
    lane = lax.broadcasted_iota(I32, (1, LANES), 1)
    lo = lane < SSM_STATE
    dt = jnp.exp(ldt)
    rho, th = ar * dt, ai * dt
    npow = S5_CHUNK + 1
    nn = lax.broadcasted_iota(I32, (24, LANES), 0).astype(F32)
    mag = jnp.exp(nn * rho)
    lc = mag * jnp.cos(nn * th)
    ls = mag * jnp.sin(nn * th)
    lam_pk = jnp.where(lo, lc, ls)
    lam_sw = jnp.where(lo, -ls, lc)
    num_re, num_im = lc[1:2] - 1.0, ls[1:2]
    den = ar * ar + ai * ai
    f_re = (num_re * ar + num_im * ai) / den
    f_im = (num_im * ar - num_re * ai) / den
    g_re = btr * f_re - bti * f_im
    g_im = btr * f_im + bti * f_re
    g_neg = jnp.where(lo, g_re, -g_im)
    w_pk = [cre * lam_pk[n:n + 1] + cim * lam_sw[n:n + 1] for n in range(npow)]
    w_state = [g_re * lam_pk[S5_CHUNK - 1 - ti:S5_CHUNK - ti] + g_im * lam_sw[S5_CHUNK - 1 - ti:S5_CHUNK - ti]
               for ti in range(S5_CHUNK)]
    a_p = jnp.where(lo, lam_pk[S5_CHUNK:S5_CHUNK + 1], lam_sw[S5_CHUNK:S5_CHUNK + 1])
    a_q = jnp.where(lo, lam_sw[S5_CHUNK:S5_CHUNK + 1], lam_pk[S5_CHUNK:S5_CHUNK + 1])
    return g_neg, w_pk, w_state, a_p, a_q


def _place(block, g):
    z = jnp.zeros_like(block)
    return jnp.concatenate([block if j == g else z for j in range(S5_LANE_GROUPS)], axis=1)


def _swap_halves(x):
    return jnp.concatenate([pltpu.roll(x[:, g * LANES:(g + 1) * LANES], SSM_STATE, 1)
                            for g in range(x.shape[1] // LANES)], axis=1)


def _s5_kernel(up_ref, are_ref, aim_ref, ldt_ref, btr_ref, bti_ref, cre_ref, cim_ref, dsk_ref,
               y_ref, s_ref, ssw_ref, hprev_ref, kbd_ref, wst_ref, wot_ref, *, n_steps, n_seg):
    ng = S5_LANE_GROUPS
    lane = lax.broadcasted_iota(I32, (1, LANES), 1)
    lo = lane < SSM_STATE
    weights = [_s5_group_weights(ldt_ref[g], are_ref[g], aim_ref[g], btr_ref[g], bti_ref[g], cre_ref[g], cim_ref[g])
               for g in range(ng)]
    gneg_blk = jnp.concatenate([_place(weights[g][0], g) for g in range(ng)], axis=0)
    for tau in range(S5_CHUNK):
        wt = jnp.concatenate([_place(weights[g][1][tau], g) for g in range(ng)], axis=0)
        kbd_ref[:, tau * LANES:(tau + 1) * LANES] = lax.dot_general(
            gneg_blk, wt, (((1,), (1,)), ((), ())), preferred_element_type=F32, precision=lax.Precision.HIGHEST)
    for ti in range(S5_CHUNK):
        wst_ref[ti * LANES:(ti + 1) * LANES, :] = jnp.concatenate(
            [_place(weights[g][2][ti], g) for g in range(ng)], axis=0).astype(BF16)
        wot_ref[ti * LANES:(ti + 1) * LANES, :] = jnp.concatenate(
            [_place(jnp.where(lo, weights[g][1][ti + 1], -weights[g][1][ti + 1]), g) for g in range(ng)],
            axis=0).astype(BF16)
    a_p = jnp.concatenate([weights[g][3] for g in range(ng)], axis=1)
    a_q = jnp.concatenate([weights[g][4] for g in range(ng)], axis=1)

    rows = n_steps * S5_STREAMS

    def chunk_rows(ti):
        return up_ref[:, ti * S5_STREAMS:(ti + 1) * S5_STREAMS, :].reshape(rows, LANES)

    lhs = jnp.concatenate([chunk_rows(ti) for ti in range(S5_CHUNK)], axis=1)
    s_all = _dot(lhs, wst_ref[...])
    s_ref[...] = s_all
    ssw_ref[...] = _swap_halves(s_all)

    def cmul(x, p, q):
        return x * p + _swap_halves(x) * q

    def step(h, hs, i):
        rows_i = pl.ds(pl.multiple_of(i * S5_STREAMS, S5_STREAMS), S5_STREAMS)
        return h * a_p + hs * a_q + s_ref[rows_i, :], hs * a_p - h * a_q + ssw_ref[rows_i, :]

    zero = jnp.zeros((S5_STREAMS, ng * LANES), F32)
    z, _ = lax.fori_loop(0, n_steps, lambda i, c: step(c[0], c[1], i), (zero, zero))

    lo_all = jnp.concatenate([lo] * ng, axis=1)

    def factors(zpk):
        sw = _swap_halves(zpk)
        return jnp.where(lo_all, zpk, sw), jnp.where(lo_all, -sw, zpk)

    base, seg, n = jnp.where(lo_all, a_p, a_q), None, n_steps
    while n:
        if n & 1:
            seg = base if seg is None else cmul(seg, *factors(base))
        n >>= 1
        if n:
            base = cmul(base, *factors(base))
    seg_p, seg_q = factors(seg)

    row_id = lax.broadcasted_iota(I32, (S5_STREAMS, ng * LANES), 0)
    init = zero
    prev = None
    for s in range(S5_STREAMS):
        if s % n_seg == 0:
            cur = jnp.zeros((1, ng * LANES), F32)
        else:
            cur = cmul(prev, seg_p, seg_q) + z[s - 1:s]
            init = jnp.where(row_id == s, cur, init)
        prev = cur

    def pass2(i, c):
        hprev_ref[pl.ds(pl.multiple_of(i * S5_STREAMS, S5_STREAMS), S5_STREAMS), :] = c[0]
        return step(c[0], c[1], i)

    lax.fori_loop(0, n_steps, pass2, (init, _swap_halves(init)))

    hp = hprev_ref[...].astype(BF16)
    dsk = dsk_ref[...]
    zblk = jnp.zeros((LANES, LANES), F32)
    for tp in range(S5_CHUNK // 2):
        kdim = (2 * tp + 2) * LANES
        cols = []
        for t2 in (2 * tp, 2 * tp + 1):
            cols.append(jnp.concatenate(
                [kbd_ref[:, (t2 - ti) * LANES:(t2 - ti + 1) * LANES] if ti <= t2 else zblk
                 for ti in range(2 * tp + 2)], axis=0))
        slab = jnp.concatenate(cols, axis=1).astype(BF16)
        y = _dot(lhs[:, :kdim], slab) + _dot_nt(hp, wot_ref[2 * tp * LANES:(2 * tp + 2) * LANES, :])
        for half in range(2):
            t2 = 2 * tp + half
            u_t = up_ref[:, t2 * S5_STREAMS:(t2 + 1) * S5_STREAMS, :].astype(F32)
            y_t = y[:, half * LANES:(half + 1) * LANES].reshape(n_steps, S5_STREAMS, LANES) + u_t * dsk
            y_ref[:, t2 * S5_STREAMS:(t2 + 1) * S5_STREAMS, :] = y_t.astype(y_ref.dtype)


def _s5_branch(up, B, a_re, a_im, log_dt, b_re, b_im, c_re, c_im, d_skip):
    G, C = SSM_GROUPS, SSM_GROUP
    n_steps = up.shape[0]
    n_seg = S5_STREAMS // B
    rows = n_steps * S5_STREAMS
    ng = S5_LANE_GROUPS

    def dup(a):
        return jnp.concatenate([a, a], axis=-1).astype(F32)

    are2 = dup(a_re)[:, None, :]
    aim2 = dup(a_im)[:, None, :]
    ldt = log_dt.astype(F32)[:, None, None]
    btr2 = dup(jnp.swapaxes(b_re, 1, 2))
    bti2 = dup(jnp.swapaxes(b_im, 1, 2))
    cre2 = dup(c_re)
    cim2 = dup(c_im)
    dsk = d_skip.astype(F32).reshape(G // ng, 1, ng * C)

    def per_block(shape):
        return pl.BlockSpec((ng,) + shape, lambda i: (i, 0, 0))

    act = pl.BlockSpec((n_steps, S5_CHUNK * S5_STREAMS, LANES), lambda i: (0, 0, i))
    return pl.pallas_call(
        functools.partial(_s5_kernel, n_steps=n_steps, n_seg=n_seg),
        grid=(G // ng,),
        in_specs=[act, per_block((1, LANES)), per_block((1, LANES)), per_block((1, 1)),
                  per_block((C, LANES)), per_block((C, LANES)), per_block((C, LANES)), per_block((C, LANES)),
                  pl.BlockSpec((None, 1, LANES), lambda i: (i, 0, 0))],
        out_specs=act,
        out_shape=jax.ShapeDtypeStruct(up.shape, BF16),
        scratch_shapes=[pltpu.VMEM((rows, ng * LANES), F32),
                        pltpu.VMEM((rows, ng * LANES), F32),
                        pltpu.VMEM((rows, ng * LANES), F32),
                        pltpu.VMEM((LANES, S5_CHUNK * LANES), F32),
                        pltpu.VMEM((S5_CHUNK * LANES, ng * LANES), BF16),
                        pltpu.VMEM((S5_CHUNK * LANES, ng * LANES), BF16)],
        compiler_params=pltpu.CompilerParams(dimension_semantics=("arbitrary",), vmem_limit_bytes=VMEM_LIMIT),
        name="s5",
    )(up, are2, aim2, ldt, btr2, bti2, cre2, cim2, dsk)


def _float_to_key(x):
    bits = lax.bitcast_convert_type(x, I32)
    return bits ^ (lax.shift_right_arithmetic(bits, 31) & 0x7FFFFFFF)


def _key_to_float(key):
    return lax.bitcast_convert_type(key ^ (lax.shift_right_arithmetic(key, 31) & 0x7FFFFFFF), F32)


def _probit(p):
    t = jnp.sqrt(-2.0 * jnp.log(jnp.minimum(p, 1.0 - p)))
    z = t - ((0.010328 * t + 0.802853) * t + 2.515517) / (((0.001308 * t + 0.189269) * t + 1.432788) * t + 1.0)
    return jnp.where(p < 0.5, -z, z)


def _count_ge(sc_ref, trial, n_tiles, ktile):
    def count_tile(t, acc):
        off = pl.multiple_of(t * ktile, ktile)
        for j in range(ktile // COUNT_ROWS):
            rows = sc_ref[pl.ds(off + j * COUNT_ROWS, COUNT_ROWS), :]
            acc = jnp.where(rows >= trial, acc + 1, acc)
        return acc

    acc = lax.fori_loop(0, n_tiles, count_tile, jnp.zeros((COUNT_ROWS, LANES), I32))
    return jnp.sum(acc, axis=0, keepdims=True).astype(F32)


def _select_threshold(sc_ref, s_min, s_max, n_allowed, n_tiles, k_sel, ktile):
    kf = float(k_sel)
    n_eff = jnp.maximum(n_allowed, kf + 1.0)
    z_t = _probit(1.0 - (kf - 0.5) / n_eff)

    def z_of(c):
        return _probit(jnp.clip(1.0 - c / n_eff, 0.5 / n_eff, 1.0 - 0.5 / n_eff))

    def body(st):
        it, t_lo, t_hi, c_lo, c_hi, z_lo, z_hi, w_lo, w_hi, side, done, thr = st
        g_lo = (z_lo - z_t) * w_lo
        g_hi = (z_hi - z_t) * w_hi
        t = t_lo + (t_hi - t_lo) * jnp.clip(g_lo / (g_lo - g_hi), 0.0, 1.0)
        k_lo, k_hi, k_t = _float_to_key(t_lo), _float_to_key(t_hi), _float_to_key(t)
        mid = lax.shift_right_arithmetic(k_lo, 1) + lax.shift_right_arithmetic(k_hi, 1) + (k_lo & k_hi & 1)
        use_mid = (k_t <= k_lo) | (k_t >= k_hi) | (c_lo - c_hi <= 4.0) | (it >= 20)
        t = _key_to_float(jnp.where(use_mid, mid, k_t))
        c = _count_ge(sc_ref, t, n_tiles, ktile)
        active = done == 0
        hit = active & (c == kf)
        new_lo = active & (c >= kf)
        new_hi = active & (c < kf)
        z_c = z_of(c)
        w_hi = jnp.where(new_lo, jnp.where(side > 0, 0.5 * w_hi, w_hi), 1.0)
        w_lo = jnp.where(new_hi, jnp.where(side < 0, 0.5 * w_lo, w_lo), 1.0)
        side = jnp.where(new_lo, 1.0, jnp.where(new_hi, -1.0, side))
        t_lo = jnp.where(new_lo, t, t_lo)
        c_lo = jnp.where(new_lo, c, c_lo)
        z_lo = jnp.where(new_lo, z_c, z_lo)
        t_hi = jnp.where(new_hi, t, t_hi)
        c_hi = jnp.where(new_hi, c, c_hi)
        z_hi = jnp.where(new_hi, z_c, z_hi)
        adjacent = active & (_float_to_key(t_hi) - 1 <= _float_to_key(t_lo))
        thr = jnp.where(hit, t, jnp.where(adjacent, t_lo, thr))
        done = jnp.where(hit | adjacent, 1, done)
        return it + 1, t_lo, t_hi, c_lo, c_hi, z_lo, z_hi, w_lo, w_hi, side, done, thr

    def cond(st):
        return jnp.logical_and(st[0] < 64, jnp.min(st[10]) < 1)

    few = n_allowed <= kf
    ones = jnp.ones((1, LANES), F32)
    zeros = jnp.zeros((1, LANES), F32)
    init = (jnp.int32(0), s_min, _key_to_float(_float_to_key(s_max) + 1), n_allowed, zeros,
            z_of(n_allowed), z_of(zeros), ones, ones, zeros,
            few.astype(I32), jnp.where(few, -jnp.inf, s_min))
    st = lax.fori_loop(0, SELECT_STEPS, lambda _, s: body(s), init)
    return lax.while_loop(cond, body, st)[11]


def _dsa_kernel(qi_ref, wi_ref, q_ref, ki2_ref, k_ref, v_ref, o_ref,
                sc_ref, qm_ref, qg_ref, m_ref, acc_ref, rel0_ref, rel1_ref, lg0_ref, lg1_ref, *, k_sel, qblk, ktile):
    assert qblk == LANES
    qb = pl.program_id(1)
    n_tiles = lax.div(qb * qblk + qblk + ktile - 1, ktile)
    lane = lax.broadcasted_iota(I32, (qblk, LANES), 1)
    n_lt = ktile // LANES

    for hd in range(IDX_HEADS):
        pair = qi_ref[:, (hd // 2) * LANES:(hd // 2 + 1) * LANES]
        msk = (lane < IDX_DIM) if hd % 2 == 0 else (lane >= IDX_DIM)
        qm_ref[hd * qblk:(hd + 1) * qblk, :] = jnp.where(msk, pair, jnp.zeros_like(pair))
    eye = (lax.broadcasted_iota(I32, (qblk, LANES), 0) == lane).astype(BF16)
    for hd in range(N_HEADS):
        g, r = divmod(hd, Q_PER_KV)
        qg_ref[g, r * qblk:(r + 1) * qblk, 0:HEAD_DIM] = q_ref[:, hd * HEAD_DIM:(hd + 1) * HEAD_DIM]
        qg_ref[g, r * qblk:(r + 1) * qblk, HEAD_DIM:2 * HEAD_DIM] = eye

    w_t = wi_ref[...].T
    q_chunk = lax.shift_right_logical(qb * qblk + lax.broadcasted_iota(I32, (1, LANES), 1), 6)
    n_allowed = ((q_chunk + 1) * CHUNK).astype(F32)

    rel_refs = (rel0_ref, rel1_ref)
    lg_refs = (lg0_ref, lg1_ref)

    def tile_offset(t):
        return pl.multiple_of(jnp.minimum(t, n_tiles - 1) * ktile, ktile)

    def issue_logits(slot, t):
        rel_refs[slot][...] = _dot_nt(ki2_ref[pl.ds(tile_offset(t), ktile), :], qm_ref[...])

    def score_tile(slot, t, carry):
        mx, mn = carry
        off = tile_offset(t)
        for c in range(ktile // SCORE_ROWS):
            r0 = c * SCORE_ROWS
            sc = jnp.zeros((SCORE_ROWS, LANES), F32)
            for hd in range(IDX_HEADS):
                rel = rel_refs[slot][r0:r0 + SCORE_ROWS, hd * qblk:(hd + 1) * qblk]
                sc = sc + jnp.maximum(rel, 0.0) * w_t[IDX_DIM + hd:IDX_DIM + hd + 1, :]
            k_chunk = lax.shift_right_logical(off + r0 + lax.broadcasted_iota(I32, (SCORE_ROWS, 1), 0), 6)
            ok = k_chunk <= q_chunk
            sc_ref[pl.ds(off + r0, SCORE_ROWS), :] = jnp.where(ok, sc, NEG)
            mx = jnp.maximum(mx, jnp.where(ok, sc, -jnp.inf))
            mn = jnp.minimum(mn, jnp.where(ok, sc, jnp.inf))
        return mx, mn

    def score_pair(p, carry):
        t0 = 2 * p
        issue_logits(1, t0 + 1)
        carry = score_tile(0, t0, carry)
        issue_logits(0, t0 + 2)
        return score_tile(1, t0 + 1, carry)

    n_pairs = lax.shift_right_logical(n_tiles, 1)
    odd_tail = (n_tiles & 1) == 1
    issue_logits(0, 0)
    carry = lax.fori_loop(0, n_pairs, score_pair,
                          (jnp.full((SCORE_ROWS, LANES), -jnp.inf, F32), jnp.full((SCORE_ROWS, LANES), jnp.inf, F32)))
    mx, mn = lax.cond(odd_tail, lambda: score_tile(0, n_tiles - 1, carry), lambda: carry)
    s_max = jnp.max(mx, axis=0, keepdims=True)
    s_min = jnp.min(mn, axis=0, keepdims=True)

    thr = _select_threshold(sc_ref, s_min, s_max, n_allowed, n_tiles, k_sel, ktile)
    thr = jnp.maximum(thr, _key_to_float(jnp.full((1, LANES), KEY_HALF_NEG + 1, I32)))

    m_ref[...] = jnp.full(m_ref.shape, -jnp.inf, F32)
    acc_ref[...] = jnp.zeros(acc_ref.shape, F32)
    g_rows = Q_PER_KV * qblk

    def selection_bias(t):
        thr_t = jnp.where(t < n_tiles, thr, jnp.inf)
        return jnp.where(sc_ref[pl.ds(tile_offset(t), ktile), :] >= thr_t, 0.0, NEG).astype(BF16)

    def issue_qk(slot, t, g, bias_t):
        kb = jnp.concatenate([k_ref[pl.ds(tile_offset(t), ktile), g * HEAD_DIM:(g + 1) * HEAD_DIM], bias_t], axis=1)
        lg_refs[slot][g] = _dot_nt(qg_ref[g], kb)

    def attend_half(cur, t, issue_next=True):
        off = tile_offset(t)
        bias_next = selection_bias(t + 1) if issue_next else None
        for g in range(N_KV_HEADS):
            if issue_next:
                issue_qk(1 - cur, t + 1, g, bias_next)
            slot = cur
            ps, alphas = [], []
            for c in range(g_rows // ATT_ROWS):
                r0 = c * ATT_ROWS
                s = lg_refs[slot][g, r0:r0 + ATT_ROWS, :]
                m_cur = s[:, 0:LANES]
                for j in range(1, n_lt):
                    m_cur = jnp.maximum(m_cur, s[:, j * LANES:(j + 1) * LANES])
                m_old = m_ref[g, r0:r0 + ATT_ROWS, :]
                m_new = jnp.maximum(m_old, jnp.max(m_cur, axis=1, keepdims=True))
                alphas.append(jnp.exp2(m_old - m_new))
                m_ref[g, r0:r0 + ATT_ROWS, :] = m_new
                ps.append(jnp.exp2(s - jnp.concatenate([m_new] * n_lt, axis=1)).astype(BF16))
            alpha = jnp.concatenate(alphas, axis=0)
            pv = _dot(jnp.concatenate(ps, axis=0), v_ref[pl.ds(off, ktile), 2 * g * HEAD_DIM:(2 * g + 2) * HEAD_DIM])
            acc_ref[g] = acc_ref[g] * jnp.concatenate([alpha, alpha], axis=1) + pv

    def attend_pair(p, carry):
        attend_half(0, 2 * p)
        attend_half(1, 2 * p + 1)
        return carry

    bias_0 = selection_bias(0)
    for g in range(N_KV_HEADS):
        issue_qk(0, 0, g, bias_0)
    lax.fori_loop(0, n_pairs, attend_pair, 0)

    @pl.when(odd_tail)
    def _():
        attend_half(0, n_tiles - 1, issue_next=False)

    for hd in range(N_HEADS):
        g, r = divmod(hd, Q_PER_KV)
        a = acc_ref[g, r * qblk:(r + 1) * qblk, :]
        o_ref[:, hd * HEAD_DIM:(hd + 1) * HEAD_DIM] = (a[:, :HEAD_DIM] / a[:, HEAD_DIM:]).astype(o_ref.dtype)


def _dsa_branch(q, k, v, qi, ki2, wi, B, L, qblk=128, ktile=512):
    k_sel = min(TOPK_MAX, L // 4)
    nq = L // qblk
    g_rows = Q_PER_KV * qblk
    assert L % ktile == 0

    def qrow(w):
        return pl.BlockSpec((None, qblk, w), lambda b, i: (b, i, 0))

    def whole(w):
        return pl.BlockSpec((None, L, w), lambda b, i: (b, 0, 0))

    r3 = lambda a: a.reshape(B, L, a.shape[-1])
    out = pl.pallas_call(
        functools.partial(_dsa_kernel, k_sel=k_sel, qblk=qblk, ktile=ktile),
        grid=(B, nq),
        in_specs=[qrow(IDX_HEADS * IDX_DIM), qrow(LANES), qrow(ATT_WIDTH), whole(LANES), whole(KV_WIDTH),
                  whole(2 * KV_WIDTH)],
        out_specs=qrow(ATT_WIDTH),
        out_shape=jax.ShapeDtypeStruct((B, L, ATT_WIDTH), BF16),
        scratch_shapes=[pltpu.VMEM((L, qblk), F32),
                        pltpu.VMEM((IDX_HEADS * qblk, LANES), BF16),
                        pltpu.VMEM((N_KV_HEADS, g_rows, 2 * HEAD_DIM), BF16),
                        pltpu.VMEM((N_KV_HEADS, g_rows, LANES), F32),
                        pltpu.VMEM((N_KV_HEADS, g_rows, 2 * HEAD_DIM), F32),
                        pltpu.VMEM((ktile, IDX_HEADS * qblk), F32),
                        pltpu.VMEM((ktile, IDX_HEADS * qblk), F32),
                        pltpu.VMEM((N_KV_HEADS, g_rows, ktile), F32),
                        pltpu.VMEM((N_KV_HEADS, g_rows, ktile), F32)],
        compiler_params=pltpu.CompilerParams(dimension_semantics=("arbitrary", "arbitrary"),
                                             vmem_limit_bytes=VMEM_LIMIT),
        name="dsa",
    )(r3(qi), r3(wi), r3(q), r3(ki2), r3(k), r3(v))
    return out.reshape(B * L, ATT_WIDTH)


def _gelu_tanh(x):
    return 0.5 * x * (1.0 + jnp.tanh(math.sqrt(2.0 / math.pi) * (x + 0.044715 * (x * x * x))))


def _merge_kernel(x_ref, y_ref, yb_ref, gate_ref, wglu_ref, wa_ref, wb_ref, wo_ref, o_ref, y_scr):
    n_st, rt, d = x_ref.shape
    rows = n_st * rt
    _store_lane_blocks(y_scr, y_ref[...].astype(F32))
    y = jnp.concatenate([_strided_rows(y_scr, s, rt, n_st) for s in range(n_st)], axis=0)
    y = _gelu_tanh(y)
    ya = y * jax.nn.sigmoid(_dot(y.astype(BF16), wglu_ref[...]))
    gate = gate_ref[...].reshape(rows, 2 * d)
    ga, gb = gate[:, :D_MODEL], gate[:, D_MODEL:]
    merged = (jax.nn.sigmoid(ga) * _dot(ya.astype(BF16), wa_ref[...])
              + jax.nn.sigmoid(gb) * _dot(yb_ref[...].reshape(rows, ATT_WIDTH), wb_ref[...]))
    out = x_ref[...].reshape(rows, d) + _dot(merged.astype(BF16), wo_ref[...])
    o_ref[...] = out.reshape(n_st, rt, d)


def _merge(x, y, yb, gate, w_glu, w_a, w_b, w_out):
    n_st, l_seg, _ = x.shape
    rt = ROW_TILE
    ws = [w.astype(BF16) for w in (w_glu, w_a, w_b, w_out)]

    def streams(w):
        return pl.BlockSpec((n_st, rt, w), lambda i: (0, i, 0))

    return pl.pallas_call(
        _merge_kernel,
        grid=(l_seg // rt,),
        in_specs=[streams(D_MODEL), pl.BlockSpec((None, rt * n_st, D_MODEL), lambda i: (i, 0, 0)),
                  streams(ATT_WIDTH), streams(2 * D_MODEL)]
                 + [pl.BlockSpec(w.shape, lambda i: (0, 0)) for w in ws],
        out_specs=streams(D_MODEL),
        out_shape=jax.ShapeDtypeStruct((n_st, l_seg, D_MODEL), F32),
        scratch_shapes=[pltpu.VMEM((D_MODEL // LANES, rt * n_st, LANES), F32)],
        compiler_params=pltpu.CompilerParams(dimension_semantics=("arbitrary",), vmem_limit_bytes=VMEM_LIMIT),
        name="merge",
    )(x, y, yb, gate, *ws)


def _rms(x, g):
    return x * lax.rsqrt(jnp.mean(x * x, axis=-1, keepdims=True) + EPS) * g


def _ffn_kernel(x_ref, g2_ref, win_ref, wout_ref, gf_ref, o_ref, *, final_norm):
    x = x_ref[...]
    h = _rms(x, g2_ref[...]).astype(BF16)
    gu = _dot(h, win_ref[...])
    g, up = gu[:, :FFN_HIDDEN], gu[:, FFN_HIDDEN:]
    act = (g * jax.nn.sigmoid(g)) * up
    x = x + _dot(act.astype(BF16), wout_ref[...])
    o_ref[...] = _rms(x, gf_ref[...]) if final_norm else x


def _ffn(x1, g2, w_ffn_in, w_ffn_out, gf, final_norm, tile):
    T = x1.shape[0]
    win, wout = w_ffn_in.astype(BF16), w_ffn_out.astype(BF16)

    def row(w):
        return pl.BlockSpec((tile, w), lambda i: (i, 0))

    def full(a):
        return pl.BlockSpec(a.shape, lambda i: (0, 0))

    return pl.pallas_call(
        functools.partial(_ffn_kernel, final_norm=final_norm),
        grid=(T // tile,),
        in_specs=[row(D_MODEL), full(g2), full(win), full(wout), full(gf)],
        out_specs=row(D_MODEL),
        out_shape=jax.ShapeDtypeStruct((T, D_MODEL), F32),
        compiler_params=pltpu.CompilerParams(dimension_semantics=("arbitrary",), vmem_limit_bytes=VMEM_LIMIT),
        name="ffn",
    )(x1, g2, win, wout, gf)


def kernel(x, norm1_g, w_in, a_re, a_im, log_dt, b_re, b_im, c_re, c_im, d_skip, w_glu,
           w_branch_a, w_branch_b, w_out, norm2_g, w_ffn_in, w_ffn_out, norm_f_g):
    B, L, D = x.shape
    depth = norm1_g.shape[0]
    n_seg = S5_STREAMS // B
    assert D == D_MODEL and S5_STREAMS % B == 0 and L % 512 == 0 and L % (n_seg * ROW_TILE) == 0
    x = x.astype(F32)
    for i in range(depth):
        up, q, k, v, qi, ki2, wi, gate = _project(x, norm1_g[i][None, :].astype(F32), w_in[i])
        y = _s5_branch(up, B, a_re[i], a_im[i], log_dt[i], b_re[i], b_im[i], c_re[i], c_im[i], d_skip[i])
        yb = _dsa_branch(q, k, v, qi, ki2, wi, B, L)
        x1 = _merge(x.reshape(S5_STREAMS, L // n_seg, D), y, yb.reshape(S5_STREAMS, L // n_seg, ATT_WIDTH), gate,
                    w_glu[i], w_branch_a[i], w_branch_b[i], w_out[i])
        x = _ffn(x1.reshape(B * L, D), norm2_g[i][None, :].astype(F32), w_ffn_in[i], w_ffn_out[i],
                 norm_f_g[None, :].astype(F32), i == depth - 1, FFN_ROWS).reshape(B, L, D)
    return x
```

```python
import functools
import math

import jax
import jax.numpy as jnp
import numpy as np
from jax import lax
from jax.experimental import pallas as pl
from jax.experimental.pallas import tpu as pltpu

F32 = jnp.float32
BF16 = jnp.bfloat16
I32 = jnp.int32

D_MODEL = 1024
CHUNK = 64
EPS = 1e-6
NEG = -1e30

SSM_GROUP = 16
SSM_GROUPS = 64
SSM_STATE = 64
S5_CHUNK = 16
S5_STREAMS = 16
S5_LANE_GROUPS = 8
ROW_TILE = S5_CHUNK
FFN_ROWS = 256

N_HEADS = 8
HEAD_DIM = 128
N_KV_HEADS = 2
Q_PER_KV = N_HEADS // N_KV_HEADS
ATT_WIDTH = N_HEADS * HEAD_DIM
KV_WIDTH = N_KV_HEADS * HEAD_DIM
IDX_HEADS = 16
IDX_DIM = 64
TOPK_MAX = 256
ROPE_THETA = 500000.0
ATT_ROT = HEAD_DIM // 4
IDX_ROT = IDX_DIM // 4
FFN_HIDDEN = -(-8 * D_MODEL // (3 * 256)) * 256

SCORE_ROWS = 16
COUNT_ROWS = 32
SELECT_STEPS = 12
ATT_ROWS = 32

LANES = 128
INT_MIN = -(2 ** 31)

VMEM_LIMIT = 56 * 1024 * 1024


def _key_of_float(val):
    bits = int(np.float32(val).view(np.int32))
    return bits ^ ((bits >> 31) & 0x7FFFFFFF)


KEY_HALF_NEG = _key_of_float(0.5 * NEG)


def _dot(a, b):
    return jnp.dot(a, b, preferred_element_type=F32)


def _dot_nt(a, b):
    return lax.dot_general(a, b, (((1,), (1,)), ((), ())), preferred_element_type=F32)


def _split_hi_lo(a):
    hi = a.astype(BF16)
    lo = (a - hi.astype(F32)).astype(BF16)
    return hi, lo


def _rope(x, c, s1, s2, half):
    n = x.shape[-1]
    return x * c + pltpu.roll(x, half, 1) * s1 + pltpu.roll(x, n - half, 1) * s2


def _store_lane_blocks(scr, val):
    for j in range(scr.shape[0]):
        scr[j] = val[:, j * LANES:(j + 1) * LANES]


def _strided_rows(scr, start, size, stride):
    return jnp.concatenate([scr[j, pl.ds(start, size, stride=stride), :] for j in range(scr.shape[0])], axis=1)


def _proj_kernel(x_ref, g1_ref, wu_ref, wq_ref, wk_ref, wv_ref, wqi_ref, wsm_ref, wg_ref,
                 ca_ref, sa1_ref, sa2_ref, ci_ref, si1_ref, si2_ref, cs_ref, ss1_ref, ss2_ref,
                 u_ref, q_ref, k_ref, v_ref, qi_ref, ki2_ref, wi_ref, gate_ref, u_scr, *, batch):
    n_st, rt, d = x_ref.shape
    rows = n_st * rt
    x = x_ref[...].reshape(rows, d)
    h = x * lax.rsqrt(jnp.mean(x * x, axis=-1, keepdims=True) + EPS) * g1_ref[...]
    hb = h.astype(BF16)

    def put(ref, val, sl=slice(None)):
        ref[:, :, sl] = val.reshape(n_st, rt, val.shape[-1])

    def table(ref):
        t = ref[...].reshape(rows // batch, LANES)
        return jnp.concatenate([t] * batch, axis=0)

    _store_lane_blocks(u_scr, _dot(hb, wu_ref[...]))
    for t in range(rt):
        u_ref[t * n_st:(t + 1) * n_st, :] = _strided_rows(u_scr, t, n_st, rt).astype(BF16)

    v = _dot(hb, wv_ref[...]).astype(BF16)
    ones = jnp.ones((rows, HEAD_DIM), BF16)
    for hd in range(N_KV_HEADS):
        put(v_ref, v[:, hd * HEAD_DIM:(hd + 1) * HEAD_DIM], slice(2 * hd * HEAD_DIM, (2 * hd + 1) * HEAD_DIM))
        put(v_ref, ones, slice((2 * hd + 1) * HEAD_DIM, (2 * hd + 2) * HEAD_DIM))
    put(gate_ref, _dot(hb, wg_ref[...]))

    ca, sa1, sa2 = table(ca_ref), table(sa1_ref), table(sa2_ref)
    q = _dot(hb, wq_ref[...])
    scale = HEAD_DIM ** -0.5 * math.log2(math.e)
    for hd in range(N_HEADS):
        sl = slice(hd * HEAD_DIM, (hd + 1) * HEAD_DIM)
        put(q_ref, (_rope(q[:, sl], ca, sa1, sa2, ATT_ROT // 2) * scale).astype(BF16), sl)
    k = _dot(hb, wk_ref[...])
    for hd in range(N_KV_HEADS):
        sl = slice(hd * HEAD_DIM, (hd + 1) * HEAD_DIM)
        put(k_ref, _rope(k[:, sl], ca, sa1, sa2, ATT_ROT // 2).astype(BF16), sl)

    ci, si1, si2 = table(ci_ref), table(si1_ref), table(si2_ref)
    qi = _dot(hb, wqi_ref[...])
    for pr in range(IDX_HEADS * IDX_DIM // LANES):
        sl = slice(pr * LANES, (pr + 1) * LANES)
        put(qi_ref, _rope(qi[:, sl], ci, si1, si2, IDX_ROT // 2).astype(BF16), sl)

    sm = _dot(hb, wsm_ref[...])
    sm = _rope(sm, table(cs_ref), table(ss1_ref), table(ss2_ref), IDX_ROT // 2)
    lane = lax.broadcasted_iota(I32, sm.shape, 1)
    put(ki2_ref, jnp.where(lane < IDX_DIM, sm, pltpu.roll(sm, IDX_DIM, 1)).astype(BF16))
    put(wi_ref, sm * (IDX_DIM ** -0.5 * IDX_HEADS ** -0.5))


def _rope_tables(L, rot, period, active_lanes):
    half = rot // 2
    pos = jnp.arange(L, dtype=jnp.int32)
    inv_freq = ROPE_THETA ** (-jnp.arange(half, dtype=F32) / half)
    ang = pos.astype(F32)[:, None] * inv_freq[None, :]
    cos, sin = jnp.cos(ang), jnp.sin(ang)
    lane = np.arange(LANES)
    within = lane % period
    fidx = np.where(within < rot, within % half, 0)
    is_x1 = (within < half) & (lane < active_lanes)
    is_x2 = (within >= half) & (within < rot) & (lane < active_lanes)
    rot_lane = is_x1 | is_x2
    c = jnp.where(rot_lane[None, :], cos[:, fidx], 1.0)
    s1 = jnp.where(is_x2[None, :], sin[:, fidx], 0.0)
    s2 = jnp.where(is_x1[None, :], -sin[:, fidx], 0.0)
    return c.astype(F32), s1.astype(F32), s2.astype(F32)


def _project(x, g1, w_in):
    B, L, _ = x.shape
    n_seg = S5_STREAMS // B
    l_seg = L // n_seg
    rt = ROW_TILE
    o = 0
    parts = []
    for s in (D_MODEL, ATT_WIDTH, KV_WIDTH, KV_WIDTH, IDX_HEADS * IDX_DIM, IDX_DIM, IDX_HEADS, D_MODEL, D_MODEL):
        parts.append(w_in[:, o:o + s])
        o += s
    wu, wq, wk, wv, wqi, wki, wwi, wga, wgb = parts
    wsm = jnp.concatenate([wki, wwi, jnp.zeros((D_MODEL, LANES - IDX_DIM - IDX_HEADS), w_in.dtype)], axis=1)
    wg = jnp.concatenate([wga, wgb], axis=1)
    ws = [w.astype(BF16) for w in (wu, wq, wk, wv, wqi, wsm, wg)]
    tabs = (_rope_tables(L, ATT_ROT, HEAD_DIM, LANES)
            + _rope_tables(L, IDX_ROT, IDX_DIM, LANES)
            + _rope_tables(L, IDX_ROT, IDX_DIM, IDX_DIM))
    tabs = [t.reshape(n_seg, l_seg, LANES) for t in tabs]

    def streams(w):
        return pl.BlockSpec((S5_STREAMS, rt, w), lambda i: (0, i, 0))

    def full(a):
        return pl.BlockSpec(a.shape, lambda i: (0, 0))

    tab_spec = pl.BlockSpec((n_seg, rt, LANES), lambda i: (0, i, 0))
    widths = (ATT_WIDTH,
              KV_WIDTH,
              2 * KV_WIDTH,
              IDX_HEADS * IDX_DIM,
              LANES,
              LANES,
              2 * D_MODEL)
    dtypes = (BF16, BF16, BF16, BF16, BF16, F32, F32)
    out_shape = ((jax.ShapeDtypeStruct((l_seg // rt, rt * S5_STREAMS, D_MODEL), BF16),)
                 + tuple(jax.ShapeDtypeStruct((S5_STREAMS, l_seg, w), dt) for w, dt in zip(widths, dtypes)))
    out_specs = ((pl.BlockSpec((None, rt * S5_STREAMS, D_MODEL), lambda i: (i, 0, 0)),)
                 + tuple(streams(w) for w in widths))
    return pl.pallas_call(
        functools.partial(_proj_kernel, batch=B),
        grid=(l_seg // rt,),
        in_specs=[streams(D_MODEL), full(g1)] + [full(w) for w in ws] + [tab_spec] * 9,
        out_specs=out_specs,
        out_shape=out_shape,
        scratch_shapes=[pltpu.VMEM((D_MODEL // LANES, rt * S5_STREAMS, LANES), F32)],
        compiler_params=pltpu.CompilerParams(dimension_semantics=("arbitrary",), vmem_limit_bytes=VMEM_LIMIT),
        name="proj",
    )(x.reshape(S5_STREAMS, l_seg, D_MODEL), g1, *ws, *tabs)


def _s5_group_weights(ldt, ar, ai, btr, bti, cre, cim):
    lane = lax.broadcasted_iota(I32, (1, LANES), 1)
    lo = lane < SSM_STATE
    dt = jnp.exp(ldt)
    rho, th = ar * dt, ai * dt
    npow = S5_CHUNK + 1
    nn = lax.broadcasted_iota(I32, (24, LANES), 0).astype(F32)
    mag = jnp.exp(nn * rho)
    lc = mag * jnp.cos(nn * th)
    ls = mag * jnp.sin(nn * th)
    lam_pk = jnp.where(lo, lc, ls)
    lam_sw = jnp.where(lo, -ls, lc)
    num_re, num_im = lc[1:2] - 1.0, ls[1:2]
    den = ar * ar + ai * ai
    f_re = (num_re * ar + num_im * ai) / den
    f_im = (num_im * ar - num_re * ai) / den
    g_re = btr * f_re - bti * f_im
    g_im = btr * f_im + bti * f_re
    g_neg = jnp.where(lo, g_re, -g_im)
    w_pk = [cre * lam_pk[n:n + 1] + cim * lam_sw[n:n + 1] for n in range(npow)]
    w_state = [g_re * lam_pk[S5_CHUNK - 1 - ti:S5_CHUNK - ti] + g_im * lam_sw[S5_CHUNK - 1 - ti:S5_CHUNK - ti]
               for ti in range(S5_CHUNK)]
    a_p = jnp.where(lo, lam_pk[S5_CHUNK:S5_CHUNK + 1], lam_sw[S5_CHUNK:S5_CHUNK + 1])
    a_q = jnp.where(lo, lam_sw[S5_CHUNK:S5_CHUNK + 1], lam_pk[S5_CHUNK:S5_CHUNK + 1])
    return g_neg, w_pk, w_state, a_p, a_q


def _place(block, g):
    z = jnp.zeros_like(block)
    return jnp.concatenate([block if j == g else z for j in range(S5_LANE_GROUPS)], axis=1)


def _swap_halves(x):
    return jnp.concatenate([pltpu.roll(x[:, g * LANES:(g + 1) * LANES], SSM_STATE, 1)
                            for g in range(x.shape[1] // LANES)], axis=1)


def _s5_kernel(up_ref, are_ref, aim_ref, ldt_ref, btr_ref, bti_ref, cre_ref, cim_ref, dsk_ref,
               y_ref, s_ref, ssw_ref, hprev_ref, kbd_ref, wst_ref, wot_ref, *, n_steps, n_seg):
    ng = S5_LANE_GROUPS
    lane = lax.broadcasted_iota(I32, (1, LANES), 1)
    lo = lane < SSM_STATE
    weights = [_s5_group_weights(ldt_ref[g], are_ref[g], aim_ref[g], btr_ref[g], bti_ref[g], cre_ref[g], cim_ref[g])
               for g in range(ng)]
    gneg_blk = jnp.concatenate([_place(weights[g][0], g) for g in range(ng)], axis=0)
    for tau in range(S5_CHUNK):
        wt = jnp.concatenate([_place(weights[g][1][tau], g) for g in range(ng)], axis=0)
        kbd_ref[:, tau * LANES:(tau + 1) * LANES] = lax.dot_general(
            gneg_blk, wt, (((1,), (1,)), ((), ())), preferred_element_type=F32, precision=lax.Precision.HIGHEST)
    for ti in range(S5_CHUNK):
        wst_ref[ti * LANES:(ti + 1) * LANES, :] = jnp.concatenate(
            [_place(weights[g][2][ti], g) for g in range(ng)], axis=0).astype(BF16)
        wot_ref[ti * LANES:(ti + 1) * LANES, :] = jnp.concatenate(
            [_place(jnp.where(lo, weights[g][1][ti + 1], -weights[g][1][ti + 1]), g) for g in range(ng)],
            axis=0).astype(BF16)
    a_p = jnp.concatenate([weights[g][3] for g in range(ng)], axis=1)
    a_q = jnp.concatenate([weights[g][4] for g in range(ng)], axis=1)

    rows = n_steps * S5_STREAMS

    def chunk_rows(ti):
        return up_ref[:, ti * S5_STREAMS:(ti + 1) * S5_STREAMS, :].reshape(rows, LANES)

    lhs = jnp.concatenate([chunk_rows(ti) for ti in range(S5_CHUNK)], axis=1)
    s_all = _dot(lhs, wst_ref[...])
    s_ref[...] = s_all
    ssw_ref[...] = _swap_halves(s_all)

    def cmul(x, p, q):
        return x * p + _swap_halves(x) * q

    def step(h, hs, i):
        rows_i = pl.ds(pl.multiple_of(i * S5_STREAMS, S5_STREAMS), S5_STREAMS)
        return h * a_p + hs * a_q + s_ref[rows_i, :], hs * a_p - h * a_q + ssw_ref[rows_i, :]

    zero = jnp.zeros((S5_STREAMS, ng * LANES), F32)
    z, _ = lax.fori_loop(0, n_steps, lambda i, c: step(c[0], c[1], i), (zero, zero))

    lo_all = jnp.concatenate([lo] * ng, axis=1)

    def factors(zpk):
        sw = _swap_halves(zpk)
        return jnp.where(lo_all, zpk, sw), jnp.where(lo_all, -sw, zpk)

    base, seg, n = jnp.where(lo_all, a_p, a_q), None, n_steps
    while n:
        if n & 1:
            seg = base if seg is None else cmul(seg, *factors(base))
        n >>= 1
        if n:
            base = cmul(base, *factors(base))
    seg_p, seg_q = factors(seg)

    row_id = lax.broadcasted_iota(I32, (S5_STREAMS, ng * LANES), 0)
    init = zero
    prev = None
    for s in range(S5_STREAMS):
        if s % n_seg == 0:
            cur = jnp.zeros((1, ng * LANES), F32)
        else:
            cur = cmul(prev, seg_p, seg_q) + z[s - 1:s]
            init = jnp.where(row_id == s, cur, init)
        prev = cur

    def pass2(i, c):
        hprev_ref[pl.ds(pl.multiple_of(i * S5_STREAMS, S5_STREAMS), S5_STREAMS), :] = c[0]
        return step(c[0], c[1], i)

    lax.fori_loop(0, n_steps, pass2, (init, _swap_halves(init)))

    hp = hprev_ref[...].astype(BF16)
    dsk = dsk_ref[...]
    zblk = jnp.zeros((LANES, LANES), F32)
    for tp in range(S5_CHUNK // 2):
        kdim = (2 * tp + 2) * LANES
        cols = []
        for t2 in (2 * tp, 2 * tp + 1):
            cols.append(jnp.concatenate(
                [kbd_ref[:, (t2 - ti) * LANES:(t2 - ti + 1) * LANES] if ti <= t2 else zblk
                 for ti in range(2 * tp + 2)], axis=0))
        slab = jnp.concatenate(cols, axis=1).astype(BF16)
        y = _dot(lhs[:, :kdim], slab) + _dot_nt(hp, wot_ref[2 * tp * LANES:(2 * tp + 2) * LANES, :])
        for half in range(2):
            t2 = 2 * tp + half
            u_t = up_ref[:, t2 * S5_STREAMS:(t2 + 1) * S5_STREAMS, :].astype(F32)
            y_t = y[:, half * LANES:(half + 1) * LANES].reshape(n_steps, S5_STREAMS, LANES) + u_t * dsk
            y_ref[:, t2 * S5_STREAMS:(t2 + 1) * S5_STREAMS, :] = y_t.astype(y_ref.dtype)


def _s5_branch(up, B, a_re, a_im, log_dt, b_re, b_im, c_re, c_im, d_skip):
    G, C = SSM_GROUPS, SSM_GROUP
    n_steps = up.shape[0]
    n_seg = S5_STREAMS // B
    rows = n_steps * S5_STREAMS
    ng = S5_LANE_GROUPS

    def dup(a):
        return jnp.concatenate([a, a], axis=-1).astype(F32)

    are2 = dup(a_re)[:, None, :]
    aim2 = dup(a_im)[:, None, :]
    ldt = log_dt.astype(F32)[:, None, None]
    btr2 = dup(jnp.swapaxes(b_re, 1, 2))
    bti2 = dup(jnp.swapaxes(b_im, 1, 2))
    cre2 = dup(c_re)
    cim2 = dup(c_im)
    dsk = d_skip.astype(F32).reshape(G // ng, 1, ng * C)

    def per_block(shape):
        return pl.BlockSpec((ng,) + shape, lambda i: (i, 0, 0))

    act = pl.BlockSpec((n_steps, S5_CHUNK * S5_STREAMS, LANES), lambda i: (0, 0, i))
    return pl.pallas_call(
        functools.partial(_s5_kernel, n_steps=n_steps, n_seg=n_seg),
        grid=(G // ng,),
        in_specs=[act, per_block((1, LANES)), per_block((1, LANES)), per_block((1, 1)),
                  per_block((C, LANES)), per_block((C, LANES)), per_block((C, LANES)), per_block((C, LANES)),
                  pl.BlockSpec((None, 1, LANES), lambda i: (i, 0, 0))],
        out_specs=act,
        out_shape=jax.ShapeDtypeStruct(up.shape, BF16),
        scratch_shapes=[pltpu.VMEM((rows, ng * LANES), F32),
                        pltpu.VMEM((rows, ng * LANES), F32),
                        pltpu.VMEM((rows, ng * LANES), F32),
                        pltpu.VMEM((LANES, S5_CHUNK * LANES), F32),
                        pltpu.VMEM((S5_CHUNK * LANES, ng * LANES), BF16),
                        pltpu.VMEM((S5_CHUNK * LANES, ng * LANES), BF16)],
        compiler_params=pltpu.CompilerParams(dimension_semantics=("arbitrary",), vmem_limit_bytes=VMEM_LIMIT),
        name="s5",
    )(up, are2, aim2, ldt, btr2, bti2, cre2, cim2, dsk)


def _float_to_key(x):
    bits = lax.bitcast_convert_type(x, I32)
    return bits ^ (lax.shift_right_arithmetic(bits, 31) & 0x7FFFFFFF)


def _key_to_float(key):
    return lax.bitcast_convert_type(key ^ (lax.shift_right_arithmetic(key, 31) & 0x7FFFFFFF), F32)


def _probit(p):
    t = jnp.sqrt(-2.0 * jnp.log(jnp.minimum(p, 1.0 - p)))
    z = t - ((0.010328 * t + 0.802853) * t + 2.515517) / (((0.001308 * t + 0.189269) * t + 1.432788) * t + 1.0)
    return jnp.where(p < 0.5, -z, z)


def _count_ge(sc_ref, trial, n_tiles, ktile):
    def count_tile(t, acc):
        off = pl.multiple_of(t * ktile, ktile)
        for j in range(ktile // COUNT_ROWS):
            rows = sc_ref[pl.ds(off + j * COUNT_ROWS, COUNT_ROWS), :]
            acc = jnp.where(rows >= trial, acc + 1, acc)
        return acc

    acc = lax.fori_loop(0, n_tiles, count_tile, jnp.zeros((COUNT_ROWS, trial.shape[1]), I32))
    return jnp.sum(acc, axis=0, keepdims=True).astype(F32)


def _select_threshold(sc_ref, s_min, s_max, n_allowed, n_tiles, k_sel, ktile):
    kf = float(k_sel)
    n_eff = jnp.maximum(n_allowed, kf + 1.0)
    z_t = _probit(1.0 - (kf - 0.5) / n_eff)

    def z_of(c):
        return _probit(jnp.clip(1.0 - c / n_eff, 0.5 / n_eff, 1.0 - 0.5 / n_eff))

    def body(st):
        it, t_lo, t_hi, c_lo, c_hi, z_lo, z_hi, w_lo, w_hi, side, done, thr = st
        g_lo = (z_lo - z_t) * w_lo
        g_hi = (z_hi - z_t) * w_hi
        t = t_lo + (t_hi - t_lo) * jnp.clip(g_lo / (g_lo - g_hi), 0.0, 1.0)
        k_lo, k_hi, k_t = _float_to_key(t_lo), _float_to_key(t_hi), _float_to_key(t)
        mid = lax.shift_right_arithmetic(k_lo, 1) + lax.shift_right_arithmetic(k_hi, 1) + (k_lo & k_hi & 1)
        use_mid = (k_t <= k_lo) | (k_t >= k_hi) | (c_lo - c_hi <= 4.0) | (it >= 20)
        t = _key_to_float(jnp.where(use_mid, mid, k_t))
        c = _count_ge(sc_ref, t, n_tiles, ktile)
        active = done == 0
        hit = active & (c == kf)
        new_lo = active & (c >= kf)
        new_hi = active & (c < kf)
        z_c = z_of(c)
        w_hi = jnp.where(new_lo, jnp.where(side > 0, 0.5 * w_hi, w_hi), 1.0)
        w_lo = jnp.where(new_hi, jnp.where(side < 0, 0.5 * w_lo, w_lo), 1.0)
        side = jnp.where(new_lo, 1.0, jnp.where(new_hi, -1.0, side))
        t_lo = jnp.where(new_lo, t, t_lo)
        c_lo = jnp.where(new_lo, c, c_lo)
        z_lo = jnp.where(new_lo, z_c, z_lo)
        t_hi = jnp.where(new_hi, t, t_hi)
        c_hi = jnp.where(new_hi, c, c_hi)
        z_hi = jnp.where(new_hi, z_c, z_hi)
        adjacent = active & (_float_to_key(t_hi) - 1 <= _float_to_key(t_lo))
        thr = jnp.where(hit, t, jnp.where(adjacent, t_lo, thr))
        done = jnp.where(hit | adjacent, 1, done)
        return it + 1, t_lo, t_hi, c_lo, c_hi, z_lo, z_hi, w_lo, w_hi, side, done, thr

    def cond(st):
        return jnp.logical_and(st[0] < 64, jnp.min(st[10]) < 1)

    few = n_allowed <= kf
    ones = jnp.ones(n_allowed.shape, F32)
    zeros = jnp.zeros(n_allowed.shape, F32)
    init = (jnp.int32(0), s_min, _key_to_float(_float_to_key(s_max) + 1), n_allowed, zeros,
            z_of(n_allowed), z_of(zeros), ones, ones, zeros,
            few.astype(I32), jnp.where(few, -jnp.inf, s_min))
    st = lax.fori_loop(0, SELECT_STEPS, lambda _, s: body(s), init)
    return lax.while_loop(cond, body, st)[11]


def _dsa_kernel(qi_ref, wi_ref, q_ref, ki2_ref, k_ref, v_ref, o_ref,
                sc_ref, qm_ref, qg_ref, m_ref, acc_ref, rel0_ref, rel1_ref, lg0_ref, lg1_ref, *, k_sel, qblk, ktile):
    assert qblk == 2 * LANES
    hq = LANES
    qb = pl.program_id(1)
    n_tiles = lax.div(qb * qblk + qblk + ktile - 1, ktile)
    lane = lax.broadcasted_iota(I32, (hq, LANES), 1)
    n_lt = ktile // LANES
    halves = (0, 1)

    eye = (lax.broadcasted_iota(I32, (hq, LANES), 0) == lane).astype(BF16)
    for h in halves:
        rows_h = slice(h * hq, (h + 1) * hq)
        for hd in range(IDX_HEADS):
            pair = qi_ref[rows_h, (hd // 2) * LANES:(hd // 2 + 1) * LANES]
            msk = (lane < IDX_DIM) if hd % 2 == 0 else (lane >= IDX_DIM)
            qm_ref[h, hd * hq:(hd + 1) * hq, :] = jnp.where(msk, pair, jnp.zeros_like(pair))
        for hd in range(N_HEADS):
            g, r = divmod(hd, Q_PER_KV)
            qg_ref[h, g, r * hq:(r + 1) * hq, 0:HEAD_DIM] = q_ref[rows_h, hd * HEAD_DIM:(hd + 1) * HEAD_DIM]
            qg_ref[h, g, r * hq:(r + 1) * hq, HEAD_DIM:2 * HEAD_DIM] = eye

    w_t = [wi_ref[h * hq:(h + 1) * hq, :].T for h in halves]
    q_chunk = lax.shift_right_logical(qb * qblk + lax.broadcasted_iota(I32, (1, qblk), 1), 6)
    n_allowed = ((q_chunk + 1) * CHUNK).astype(F32)

    rel_refs = (rel0_ref, rel1_ref)
    lg_refs = (lg0_ref, lg1_ref)

    def tile_offset(t):
        return pl.multiple_of(jnp.minimum(t, n_tiles - 1) * ktile, ktile)

    def issue_logits(h, t):
        rel_refs[h][...] = _dot_nt(ki2_ref[pl.ds(tile_offset(t), ktile), :], qm_ref[h])

    def score_half(h, t, carry):
        mx, mn = carry
        off = pl.multiple_of(t * ktile, ktile)
        q_chunk_h = q_chunk[:, h * hq:(h + 1) * hq]
        for c in range(ktile // SCORE_ROWS):
            r0 = c * SCORE_ROWS
            sc = jnp.zeros((SCORE_ROWS, LANES), F32)
            for hd in range(IDX_HEADS):
                rel = rel_refs[h][r0:r0 + SCORE_ROWS, hd * hq:(hd + 1) * hq]
                sc = sc + jnp.maximum(rel, 0.0) * w_t[h][IDX_DIM + hd:IDX_DIM + hd + 1, :]
            k_chunk = lax.shift_right_logical(off + r0 + lax.broadcasted_iota(I32, (SCORE_ROWS, 1), 0), 6)
            ok = k_chunk <= q_chunk_h
            sc_ref[pl.ds(off + r0, SCORE_ROWS), h * hq:(h + 1) * hq] = jnp.where(ok, sc, NEG)
            mx = jnp.maximum(mx, jnp.where(ok, sc, -jnp.inf))
            mn = jnp.minimum(mn, jnp.where(ok, sc, jnp.inf))
        return mx, mn

    def score_tile(t, carry):
        issue_logits(1, t)
        c0 = score_half(0, t, carry[0])
        issue_logits(0, t + 1)
        return c0, score_half(1, t, carry[1])

    issue_logits(0, 0)
    init = (jnp.full((SCORE_ROWS, LANES), -jnp.inf, F32), jnp.full((SCORE_ROWS, LANES), jnp.inf, F32))
    (mx0, mn0), (mx1, mn1) = lax.fori_loop(0, n_tiles, score_tile, (init, init))
    mx = jnp.concatenate([mx0, mx1], axis=1)
    mn = jnp.concatenate([mn0, mn1], axis=1)
    s_max = jnp.max(mx, axis=0, keepdims=True)
    s_min = jnp.min(mn, axis=0, keepdims=True)

    thr = _select_threshold(sc_ref, s_min, s_max, n_allowed, n_tiles, k_sel, ktile)
    thr = jnp.maximum(thr, _key_to_float(jnp.full((1, qblk), KEY_HALF_NEG + 1, I32)))

    m_ref[...] = jnp.full(m_ref.shape, -jnp.inf, F32)
    acc_ref[...] = jnp.zeros(acc_ref.shape, F32)
    g_rows = Q_PER_KV * hq

    def selection_bias(h, t):
        thr_t = jnp.where(t < n_tiles, thr[:, h * hq:(h + 1) * hq], jnp.inf)
        return jnp.where(sc_ref[pl.ds(tile_offset(t), ktile), h * hq:(h + 1) * hq] >= thr_t, 0.0, NEG).astype(BF16)

    def issue_qk(h, t, g, bias_t):
        kb = jnp.concatenate([k_ref[pl.ds(tile_offset(t), ktile), g * HEAD_DIM:(g + 1) * HEAD_DIM], bias_t], axis=1)
        lg_refs[h][g] = _dot_nt(qg_ref[h, g], kb)

    def attend_half(h, t):
        off = pl.multiple_of(t * ktile, ktile)
        nh, nt = (1, t) if h == 0 else (0, t + 1)
        bias_next = selection_bias(nh, nt)
        for g in range(N_KV_HEADS):
            issue_qk(nh, nt, g, bias_next)
            ps, alphas = [], []
            for c in range(g_rows // ATT_ROWS):
                r0 = c * ATT_ROWS
                s = lg_refs[h][g, r0:r0 + ATT_ROWS, :]
                m_cur = s[:, 0:LANES]
                for j in range(1, n_lt):
                    m_cur = jnp.maximum(m_cur, s[:, j * LANES:(j + 1) * LANES])
                m_old = m_ref[h, g, r0:r0 + ATT_ROWS, :]
                m_new = jnp.maximum(m_old, jnp.max(m_cur, axis=1, keepdims=True))
                alphas.append(jnp.exp2(m_old - m_new))
                m_ref[h, g, r0:r0 + ATT_ROWS, :] = m_new
                ps.append(jnp.exp2(s - jnp.concatenate([m_new] * n_lt, axis=1)).astype(BF16))
            alpha = jnp.concatenate(alphas, axis=0)
            pv = _dot(jnp.concatenate(ps, axis=0), v_ref[pl.ds(off, ktile), 2 * g * HEAD_DIM:(2 * g + 2) * HEAD_DIM])
            acc_ref[h, g] = acc_ref[h, g] * jnp.concatenate([alpha, alpha], axis=1) + pv

    def attend_tile(t, carry):
        attend_half(0, t)
        attend_half(1, t)
        return carry

    bias_0 = selection_bias(0, 0)
    for g in range(N_KV_HEADS):
        issue_qk(0, 0, g, bias_0)
    lax.fori_loop(0, n_tiles, attend_tile, 0)

    for h in halves:
        for hd in range(N_HEADS):
            g, r = divmod(hd, Q_PER_KV)
            a = acc_ref[h, g, r * hq:(r + 1) * hq, :]
            o_ref[h * hq:(h + 1) * hq, hd * HEAD_DIM:(hd + 1) * HEAD_DIM] = (
                a[:, :HEAD_DIM] / a[:, HEAD_DIM:]).astype(o_ref.dtype)


def _dsa_branch(q, k, v, qi, ki2, wi, B, L, qblk=2 * LANES, ktile=512):
    k_sel = min(TOPK_MAX, L // 4)
    nq = L // qblk
    g_rows = Q_PER_KV * LANES
    assert L % ktile == 0 and ktile % qblk == 0

    def qrow(w):
        return pl.BlockSpec((None, qblk, w), lambda b, i: (b, i, 0))

    def whole(w):
        return pl.BlockSpec((None, L, w), lambda b, i: (b, 0, 0))

    r3 = lambda a: a.reshape(B, L, a.shape[-1])
    out = pl.pallas_call(
        functools.partial(_dsa_kernel, k_sel=k_sel, qblk=qblk, ktile=ktile),
        grid=(B, nq),
        in_specs=[qrow(IDX_HEADS * IDX_DIM), qrow(LANES), qrow(ATT_WIDTH), whole(LANES), whole(KV_WIDTH),
                  whole(2 * KV_WIDTH)],
        out_specs=qrow(ATT_WIDTH),
        out_shape=jax.ShapeDtypeStruct((B, L, ATT_WIDTH), BF16),
        scratch_shapes=[pltpu.VMEM((L, qblk), F32),
                        pltpu.VMEM((2, IDX_HEADS * LANES, LANES), BF16),
                        pltpu.VMEM((2, N_KV_HEADS, g_rows, 2 * HEAD_DIM), BF16),
                        pltpu.VMEM((2, N_KV_HEADS, g_rows, LANES), F32),
                        pltpu.VMEM((2, N_KV_HEADS, g_rows, 2 * HEAD_DIM), F32),
                        pltpu.VMEM((ktile, IDX_HEADS * LANES), F32),
                        pltpu.VMEM((ktile, IDX_HEADS * LANES), F32),
                        pltpu.VMEM((N_KV_HEADS, g_rows, ktile), F32),
                        pltpu.VMEM((N_KV_HEADS, g_rows, ktile), F32)],
        compiler_params=pltpu.CompilerParams(dimension_semantics=("arbitrary", "arbitrary"),
                                             vmem_limit_bytes=VMEM_LIMIT),
        name="dsa",
    )(r3(qi), r3(wi), r3(q), r3(ki2), r3(k), r3(v))
    return out.reshape(B * L, ATT_WIDTH)


def _gelu_tanh(x):
    return 0.5 * x * (1.0 + jnp.tanh(math.sqrt(2.0 / math.pi) * (x + 0.044715 * (x * x * x))))


def _merge_kernel(x_ref, y_ref, yb_ref, gate_ref, wglu_ref, wa_ref, wb_ref, wo_ref, o_ref, y_scr):
    n_st, rt, d = x_ref.shape
    rows = n_st * rt
    _store_lane_blocks(y_scr, y_ref[...].astype(F32))
    y = jnp.concatenate([_strided_rows(y_scr, s, rt, n_st) for s in range(n_st)], axis=0)
    y = _gelu_tanh(y)
    ya = y * jax.nn.sigmoid(_dot(y.astype(BF16), wglu_ref[...]))
    gate = gate_ref[...].reshape(rows, 2 * d)
    ga, gb = gate[:, :D_MODEL], gate[:, D_MODEL:]
    merged = (jax.nn.sigmoid(ga) * _dot(ya.astype(BF16), wa_ref[...])
              + jax.nn.sigmoid(gb) * _dot(yb_ref[...].reshape(rows, ATT_WIDTH), wb_ref[...]))
    out = x_ref[...].reshape(rows, d) + _dot(merged.astype(BF16), wo_ref[...])
    o_ref[...] = out.reshape(n_st, rt, d)


def _merge(x, y, yb, gate, w_glu, w_a, w_b, w_out):
    n_st, l_seg, _ = x.shape
    rt = ROW_TILE
    ws = [w.astype(BF16) for w in (w_glu, w_a, w_b, w_out)]

    def streams(w):
        return pl.BlockSpec((n_st, rt, w), lambda i: (0, i, 0))

    return pl.pallas_call(
        _merge_kernel,
        grid=(l_seg // rt,),
        in_specs=[streams(D_MODEL), pl.BlockSpec((None, rt * n_st, D_MODEL), lambda i: (i, 0, 0)),
                  streams(ATT_WIDTH), streams(2 * D_MODEL)]
                 + [pl.BlockSpec(w.shape, lambda i: (0, 0)) for w in ws],
        out_specs=streams(D_MODEL),
        out_shape=jax.ShapeDtypeStruct((n_st, l_seg, D_MODEL), F32),
        scratch_shapes=[pltpu.VMEM((D_MODEL // LANES, rt * n_st, LANES), F32)],
        compiler_params=pltpu.CompilerParams(dimension_semantics=("arbitrary",), vmem_limit_bytes=VMEM_LIMIT),
        name="merge",
    )(x, y, yb, gate, *ws)


def _rms(x, g):
    return x * lax.rsqrt(jnp.mean(x * x, axis=-1, keepdims=True) + EPS) * g


def _ffn_kernel(x_ref, g2_ref, win_ref, wout_ref, gf_ref, o_ref, *, final_norm):
    x = x_ref[...]
    h = _rms(x, g2_ref[...]).astype(BF16)
    gu = _dot(h, win_ref[...])
    g, up = gu[:, :FFN_HIDDEN], gu[:, FFN_HIDDEN:]
    act = (g * jax.nn.sigmoid(g)) * up
    x = x + _dot(act.astype(BF16), wout_ref[...])
    o_ref[...] = _rms(x, gf_ref[...]) if final_norm else x


def _ffn(x1, g2, w_ffn_in, w_ffn_out, gf, final_norm, tile):
    T = x1.shape[0]
    win, wout = w_ffn_in.astype(BF16), w_ffn_out.astype(BF16)

    def row(w):
        return pl.BlockSpec((tile, w), lambda i: (i, 0))

    def full(a):
        return pl.BlockSpec(a.shape, lambda i: (0, 0))

    return pl.pallas_call(
        functools.partial(_ffn_kernel, final_norm=final_norm),
        grid=(T // tile,),
        in_specs=[row(D_MODEL), full(g2), full(win), full(wout), full(gf)],
        out_specs=row(D_MODEL),
        out_shape=jax.ShapeDtypeStruct((T, D_MODEL), F32),
        compiler_params=pltpu.CompilerParams(dimension_semantics=("arbitrary",), vmem_limit_bytes=VMEM_LIMIT),
        name="ffn",
    )(x1, g2, win, wout, gf)


def kernel(x, norm1_g, w_in, a_re, a_im, log_dt, b_re, b_im, c_re, c_im, d_skip, w_glu,
           w_branch_a, w_branch_b, w_out, norm2_g, w_ffn_in, w_ffn_out, norm_f_g):
    B, L, D = x.shape
    depth = norm1_g.shape[0]
    n_seg = S5_STREAMS // B
    assert D == D_MODEL and S5_STREAMS % B == 0 and L % 512 == 0 and L % (n_seg * ROW_TILE) == 0
    x = x.astype(F32)
    for i in range(depth):
        up, q, k, v, qi, ki2, wi, gate = _project(x, norm1_g[i][None, :].astype(F32), w_in[i])
        y = _s5_branch(up, B, a_re[i], a_im[i], log_dt[i], b_re[i], b_im[i], c_re[i], c_im[i], d_skip[i])
        yb = _dsa_branch(q, k, v, qi, ki2, wi, B, L)
        x1 = _merge(x.reshape(S5_STREAMS, L // n_seg, D), y, yb.reshape(S5_STREAMS, L // n_seg, ATT_WIDTH), gate,
                    w_glu[i], w_branch_a[i], w_branch_b[i], w_out[i])
        x = _ffn(x1.reshape(B * L, D), norm2_g[i][None, :].astype(F32), w_ffn_in[i], w_ffn_out[i],
                 norm_f_g[None, :].astype(F32), i == depth - 1, FFN_ROWS).reshape(B, L, D)
    return x
```

```python
import functools
import math

import jax
import jax.numpy as jnp
import numpy as np
from jax import lax
from jax.experimental import pallas as pl
from jax.experimental.pallas import tpu as pltpu

F32 = jnp.float32
BF16 = jnp.bfloat16
I32 = jnp.int32

D_MODEL = 1024
CHUNK = 64
EPS = 1e-6
NEG = -1e30

SSM_GROUP = 16
SSM_GROUPS = 64
SSM_STATE = 64
S5_CHUNK = 16
S5_STREAMS = 16
S5_LANE_GROUPS = 8
ROW_TILE = S5_CHUNK
FFN_ROWS = 256

N_HEADS = 8
HEAD_DIM = 128
N_KV_HEADS = 2
Q_PER_KV = N_HEADS // N_KV_HEADS
ATT_WIDTH = N_HEADS * HEAD_DIM
KV_WIDTH = N_KV_HEADS * HEAD_DIM
IDX_HEADS = 16
IDX_DIM = 64
TOPK_MAX = 256
ROPE_THETA = 500000.0
ATT_ROT = HEAD_DIM // 4
IDX_ROT = IDX_DIM // 4
FFN_HIDDEN = -(-8 * D_MODEL // (3 * 256)) * 256

SCORE_ROWS = 16
COUNT_ROWS = 32
SELECT_STEPS = 8
EXTRACT_RANKS = 4
ATT_ROWS = 32

LANES = 128
INT_MIN = -(2 ** 31)

VMEM_LIMIT = 56 * 1024 * 1024


def _key_of_float(val):
    bits = int(np.float32(val).view(np.int32))
    return bits ^ ((bits >> 31) & 0x7FFFFFFF)


KEY_HALF_NEG = _key_of_float(0.5 * NEG)


def _dot(a, b):
    return jnp.dot(a, b, preferred_element_type=F32)


def _dot_nt(a, b):
    return lax.dot_general(a, b, (((1,), (1,)), ((), ())), preferred_element_type=F32)


def _split_hi_lo(a):
    hi = a.astype(BF16)
    lo = (a - hi.astype(F32)).astype(BF16)
    return hi, lo


def _rope(x, c, s1, s2, half):
    n = x.shape[-1]
    return x * c + pltpu.roll(x, half, 1) * s1 + pltpu.roll(x, n - half, 1) * s2


def _store_lane_blocks(scr, val):
    for j in range(scr.shape[0]):
        scr[j] = val[:, j * LANES:(j + 1) * LANES]


def _strided_rows(scr, start, size, stride):
    return jnp.concatenate([scr[j, pl.ds(start, size, stride=stride), :] for j in range(scr.shape[0])], axis=1)


def _proj_kernel(x_ref, g1_ref, wu_ref, wq_ref, wk_ref, wv_ref, wqi_ref, wsm_ref, wg_ref,
                 ca_ref, sa1_ref, sa2_ref, ci_ref, si1_ref, si2_ref, cs_ref, ss1_ref, ss2_ref,
                 u_ref, q_ref, k_ref, v_ref, qi_ref, ki2_ref, wi_ref, gate_ref, u_scr, *, batch):
    n_st, rt, d = x_ref.shape
    rows = n_st * rt
    x = x_ref[...].reshape(rows, d)
    h = x * lax.rsqrt(jnp.mean(x * x, axis=-1, keepdims=True) + EPS) * g1_ref[...]
    hb = h.astype(BF16)

    def put(ref, val, sl=slice(None)):
        ref[:, :, sl] = val.reshape(n_st, rt, val.shape[-1])

    def table(ref):
        t = ref[...].reshape(rows // batch, LANES)
        return jnp.concatenate([t] * batch, axis=0)

    _store_lane_blocks(u_scr, _dot(hb, wu_ref[...]))
    for t in range(rt):
        u_ref[t * n_st:(t + 1) * n_st, :] = _strided_rows(u_scr, t, n_st, rt).astype(BF16)

    v = _dot(hb, wv_ref[...]).astype(BF16)
    ones = jnp.ones((rows, HEAD_DIM), BF16)
    for hd in range(N_KV_HEADS):
        put(v_ref, v[:, hd * HEAD_DIM:(hd + 1) * HEAD_DIM], slice(2 * hd * HEAD_DIM, (2 * hd + 1) * HEAD_DIM))
        put(v_ref, ones, slice((2 * hd + 1) * HEAD_DIM, (2 * hd + 2) * HEAD_DIM))
    put(gate_ref, _dot(hb, wg_ref[...]))

    ca, sa1, sa2 = table(ca_ref), table(sa1_ref), table(sa2_ref)
    q = _dot(hb, wq_ref[...])
    scale = HEAD_DIM ** -0.5 * math.log2(math.e)
    for hd in range(N_HEADS):
        sl = slice(hd * HEAD_DIM, (hd + 1) * HEAD_DIM)
        put(q_ref, (_rope(q[:, sl], ca, sa1, sa2, ATT_ROT // 2) * scale).astype(BF16), sl)
    k = _dot(hb, wk_ref[...])
    for hd in range(N_KV_HEADS):
        sl = slice(hd * HEAD_DIM, (hd + 1) * HEAD_DIM)
        put(k_ref, _rope(k[:, sl], ca, sa1, sa2, ATT_ROT // 2).astype(BF16), sl)

    ci, si1, si2 = table(ci_ref), table(si1_ref), table(si2_ref)
    qi = _dot(hb, wqi_ref[...])
    for pr in range(IDX_HEADS * IDX_DIM // LANES):
        sl = slice(pr * LANES, (pr + 1) * LANES)
        put(qi_ref, _rope(qi[:, sl], ci, si1, si2, IDX_ROT // 2).astype(BF16), sl)

    sm = _dot(hb, wsm_ref[...])
    sm = _rope(sm, table(cs_ref), table(ss1_ref), table(ss2_ref), IDX_ROT // 2)
    lane = lax.broadcasted_iota(I32, sm.shape, 1)
    put(ki2_ref, jnp.where(lane < IDX_DIM, sm, pltpu.roll(sm, IDX_DIM, 1)).astype(BF16))
    put(wi_ref, sm * (IDX_DIM ** -0.5 * IDX_HEADS ** -0.5))


def _rope_tables(L, rot, period, active_lanes):
    half = rot // 2
    pos = jnp.arange(L, dtype=jnp.int32)
    inv_freq = ROPE_THETA ** (-jnp.arange(half, dtype=F32) / half)
    ang = pos.astype(F32)[:, None] * inv_freq[None, :]
    cos, sin = jnp.cos(ang), jnp.sin(ang)
    lane = np.arange(LANES)
    within = lane % period
    fidx = np.where(within < rot, within % half, 0)
    is_x1 = (within < half) & (lane < active_lanes)
    is_x2 = (within >= half) & (within < rot) & (lane < active_lanes)
    rot_lane = is_x1 | is_x2
    c = jnp.where(rot_lane[None, :], cos[:, fidx], 1.0)
    s1 = jnp.where(is_x2[None, :], sin[:, fidx], 0.0)
    s2 = jnp.where(is_x1[None, :], -sin[:, fidx], 0.0)
    return c.astype(F32), s1.astype(F32), s2.astype(F32)


def _project(x, g1, w_in):
    B, L, _ = x.shape
    n_seg = S5_STREAMS // B
    l_seg = L // n_seg
    rt = ROW_TILE
    o = 0
    parts = []
    for s in (D_MODEL, ATT_WIDTH, KV_WIDTH, KV_WIDTH, IDX_HEADS * IDX_DIM, IDX_DIM, IDX_HEADS, D_MODEL, D_MODEL):
        parts.append(w_in[:, o:o + s])
        o += s
    wu, wq, wk, wv, wqi, wki, wwi, wga, wgb = parts
    wsm = jnp.concatenate([wki, wwi, jnp.zeros((D_MODEL, LANES - IDX_DIM - IDX_HEADS), w_in.dtype)], axis=1)
    wg = jnp.concatenate([wga, wgb], axis=1)
    ws = [w.astype(BF16) for w in (wu, wq, wk, wv, wqi, wsm, wg)]
    tabs = (_rope_tables(L, ATT_ROT, HEAD_DIM, LANES)
            + _rope_tables(L, IDX_ROT, IDX_DIM, LANES)
            + _rope_tables(L, IDX_ROT, IDX_DIM, IDX_DIM))
    tabs = [t.reshape(n_seg, l_seg, LANES) for t in tabs]

    def streams(w):
        return pl.BlockSpec((S5_STREAMS, rt, w), lambda i: (0, i, 0))

    def full(a):
        return pl.BlockSpec(a.shape, lambda i: (0, 0))

    tab_spec = pl.BlockSpec((n_seg, rt, LANES), lambda i: (0, i, 0))
    widths = (ATT_WIDTH,
              KV_WIDTH,
              2 * KV_WIDTH,
              IDX_HEADS * IDX_DIM,
              LANES,
              LANES,
              2 * D_MODEL)
    dtypes = (BF16, BF16, BF16, BF16, BF16, F32, F32)
    out_shape = ((jax.ShapeDtypeStruct((l_seg // rt, rt * S5_STREAMS, D_MODEL), BF16),)
                 + tuple(jax.ShapeDtypeStruct((S5_STREAMS, l_seg, w), dt) for w, dt in zip(widths, dtypes)))
    out_specs = ((pl.BlockSpec((None, rt * S5_STREAMS, D_MODEL), lambda i: (i, 0, 0)),)
                 + tuple(streams(w) for w in widths))
    return pl.pallas_call(
        functools.partial(_proj_kernel, batch=B),
        grid=(l_seg // rt,),
        in_specs=[streams(D_MODEL), full(g1)] + [full(w) for w in ws] + [tab_spec] * 9,
        out_specs=out_specs,
        out_shape=out_shape,
        scratch_shapes=[pltpu.VMEM((D_MODEL // LANES, rt * S5_STREAMS, LANES), F32)],
        compiler_params=pltpu.CompilerParams(dimension_semantics=("arbitrary",), vmem_limit_bytes=VMEM_LIMIT),
        name="proj",
    )(x.reshape(S5_STREAMS, l_seg, D_MODEL), g1, *ws, *tabs)


def _s5_group_weights(ldt, ar, ai, btr, bti, cre, cim):
    lane = lax.broadcasted_iota(I32, (1, LANES), 1)
    lo = lane < SSM_STATE
    dt = jnp.exp(ldt)
    rho, th = ar * dt, ai * dt
    npow = S5_CHUNK + 1
    nn = lax.broadcasted_iota(I32, (24, LANES), 0).astype(F32)
    mag = jnp.exp(nn * rho)
    lc = mag * jnp.cos(nn * th)
    ls = mag * jnp.sin(nn * th)
    lam_pk = jnp.where(lo, lc, ls)
    lam_sw = jnp.where(lo, -ls, lc)
    num_re, num_im = lc[1:2] - 1.0, ls[1:2]
    den = ar * ar + ai * ai
    f_re = (num_re * ar + num_im * ai) / den
    f_im = (num_im * ar - num_re * ai) / den
    g_re = btr * f_re - bti * f_im
    g_im = btr * f_im + bti * f_re
    g_neg = jnp.where(lo, g_re, -g_im)
    w_pk = [cre * lam_pk[n:n + 1] + cim * lam_sw[n:n + 1] for n in range(npow)]
    w_state = [g_re * lam_pk[S5_CHUNK - 1 - ti:S5_CHUNK - ti] + g_im * lam_sw[S5_CHUNK - 1 - ti:S5_CHUNK - ti]
               for ti in range(S5_CHUNK)]
    a_p = jnp.where(lo, lam_pk[S5_CHUNK:S5_CHUNK + 1], lam_sw[S5_CHUNK:S5_CHUNK + 1])
    a_q = jnp.where(lo, lam_sw[S5_CHUNK:S5_CHUNK + 1], lam_pk[S5_CHUNK:S5_CHUNK + 1])
    return g_neg, w_pk, w_state, a_p, a_q


def _place(block, g):
    z = jnp.zeros_like(block)
    return jnp.concatenate([block if j == g else z for j in range(S5_LANE_GROUPS)], axis=1)


def _swap_halves(x):
    return jnp.concatenate([pltpu.roll(x[:, g * LANES:(g + 1) * LANES], SSM_STATE, 1)
                            for g in range(x.shape[1] // LANES)], axis=1)


def _s5_kernel(up_ref, are_ref, aim_ref, ldt_ref, btr_ref, bti_ref, cre_ref, cim_ref, dsk_ref,
               y_ref, s_ref, ssw_ref, hprev_ref, kbd_ref, wst_ref, wot_ref, *, n_steps, n_seg):
    ng = S5_LANE_GROUPS
    lane = lax.broadcasted_iota(I32, (1, LANES), 1)
    lo = lane < SSM_STATE
    weights = [_s5_group_weights(ldt_ref[g], are_ref[g], aim_ref[g], btr_ref[g], bti_ref[g], cre_ref[g], cim_ref[g])
               for g in range(ng)]
    gneg_blk = jnp.concatenate([_place(weights[g][0], g) for g in range(ng)], axis=0)
    for tau in range(S5_CHUNK):
        wt = jnp.concatenate([_place(weights[g][1][tau], g) for g in range(ng)], axis=0)
        kbd_ref[:, tau * LANES:(tau + 1) * LANES] = lax.dot_general(
            gneg_blk, wt, (((1,), (1,)), ((), ())), preferred_element_type=F32, precision=lax.Precision.HIGHEST)
    for ti in range(S5_CHUNK):
        wst_ref[ti * LANES:(ti + 1) * LANES, :] = jnp.concatenate(
            [_place(weights[g][2][ti], g) for g in range(ng)], axis=0).astype(BF16)
        wot_ref[ti * LANES:(ti + 1) * LANES, :] = jnp.concatenate(
            [_place(jnp.where(lo, weights[g][1][ti + 1], -weights[g][1][ti + 1]), g) for g in range(ng)],
            axis=0).astype(BF16)
    a_p = jnp.concatenate([weights[g][3] for g in range(ng)], axis=1)
    a_q = jnp.concatenate([weights[g][4] for g in range(ng)], axis=1)

    rows = n_steps * S5_STREAMS

    def chunk_rows(ti):
        return up_ref[:, ti * S5_STREAMS:(ti + 1) * S5_STREAMS, :].reshape(rows, LANES)

    lhs = jnp.concatenate([chunk_rows(ti) for ti in range(S5_CHUNK)], axis=1)
    s_all = _dot(lhs, wst_ref[...])
    s_ref[...] = s_all
    ssw_ref[...] = _swap_halves(s_all)

    def cmul(x, p, q):
        return x * p + _swap_halves(x) * q

    def step(h, hs, i):
        rows_i = pl.ds(pl.multiple_of(i * S5_STREAMS, S5_STREAMS), S5_STREAMS)
        return h * a_p + hs * a_q + s_ref[rows_i, :], hs * a_p - h * a_q + ssw_ref[rows_i, :]

    zero = jnp.zeros((S5_STREAMS, ng * LANES), F32)
    z, _ = lax.fori_loop(0, n_steps, lambda i, c: step(c[0], c[1], i), (zero, zero))

    lo_all = jnp.concatenate([lo] * ng, axis=1)

    def factors(zpk):
        sw = _swap_halves(zpk)
        return jnp.where(lo_all, zpk, sw), jnp.where(lo_all, -sw, zpk)

    base, seg, n = jnp.where(lo_all, a_p, a_q), None, n_steps
    while n:
        if n & 1:
            seg = base if seg is None else cmul(seg, *factors(base))
        n >>= 1
        if n:
            base = cmul(base, *factors(base))
    seg_p, seg_q = factors(seg)

    row_id = lax.broadcasted_iota(I32, (S5_STREAMS, ng * LANES), 0)
    init = zero
    prev = None
    for s in range(S5_STREAMS):
        if s % n_seg == 0:
            cur = jnp.zeros((1, ng * LANES), F32)
        else:
            cur = cmul(prev, seg_p, seg_q) + z[s - 1:s]
            init = jnp.where(row_id == s, cur, init)
        prev = cur

    def pass2(i, c):
        hprev_ref[pl.ds(pl.multiple_of(i * S5_STREAMS, S5_STREAMS), S5_STREAMS), :] = c[0]
        return step(c[0], c[1], i)

    lax.fori_loop(0, n_steps, pass2, (init, _swap_halves(init)))

    hp = hprev_ref[...].astype(BF16)
    dsk = dsk_ref[...]
    zblk = jnp.zeros((LANES, LANES), F32)
    for tp in range(S5_CHUNK // 2):
        kdim = (2 * tp + 2) * LANES
        cols = []
        for t2 in (2 * tp, 2 * tp + 1):
            cols.append(jnp.concatenate(
                [kbd_ref[:, (t2 - ti) * LANES:(t2 - ti + 1) * LANES] if ti <= t2 else zblk
                 for ti in range(2 * tp + 2)], axis=0))
        slab = jnp.concatenate(cols, axis=1).astype(BF16)
        y = _dot(lhs[:, :kdim], slab) + _dot_nt(hp, wot_ref[2 * tp * LANES:(2 * tp + 2) * LANES, :])
        for half in range(2):
            t2 = 2 * tp + half
            u_t = up_ref[:, t2 * S5_STREAMS:(t2 + 1) * S5_STREAMS, :].astype(F32)
            y_t = y[:, half * LANES:(half + 1) * LANES].reshape(n_steps, S5_STREAMS, LANES) + u_t * dsk
            y_ref[:, t2 * S5_STREAMS:(t2 + 1) * S5_STREAMS, :] = y_t.astype(y_ref.dtype)


def _s5_branch(up, B, a_re, a_im, log_dt, b_re, b_im, c_re, c_im, d_skip):
    G, C = SSM_GROUPS, SSM_GROUP
    n_steps = up.shape[0]
    n_seg = S5_STREAMS // B
    rows = n_steps * S5_STREAMS
    ng = S5_LANE_GROUPS

    def dup(a):
        return jnp.concatenate([a, a], axis=-1).astype(F32)

    are2 = dup(a_re)[:, None, :]
    aim2 = dup(a_im)[:, None, :]
    ldt = log_dt.astype(F32)[:, None, None]
    btr2 = dup(jnp.swapaxes(b_re, 1, 2))
    bti2 = dup(jnp.swapaxes(b_im, 1, 2))
    cre2 = dup(c_re)
    cim2 = dup(c_im)
    dsk = d_skip.astype(F32).reshape(G // ng, 1, ng * C)

    def per_block(shape):
        return pl.BlockSpec((ng,) + shape, lambda i: (i, 0, 0))

    act = pl.BlockSpec((n_steps, S5_CHUNK * S5_STREAMS, LANES), lambda i: (0, 0, i))
    return pl.pallas_call(
        functools.partial(_s5_kernel, n_steps=n_steps, n_seg=n_seg),
        grid=(G // ng,),
        in_specs=[act, per_block((1, LANES)), per_block((1, LANES)), per_block((1, 1)),
                  per_block((C, LANES)), per_block((C, LANES)), per_block((C, LANES)), per_block((C, LANES)),
                  pl.BlockSpec((None, 1, LANES), lambda i: (i, 0, 0))],
        out_specs=act,
        out_shape=jax.ShapeDtypeStruct(up.shape, BF16),
        scratch_shapes=[pltpu.VMEM((rows, ng * LANES), F32),
                        pltpu.VMEM((rows, ng * LANES), F32),
                        pltpu.VMEM((rows, ng * LANES), F32),
                        pltpu.VMEM((LANES, S5_CHUNK * LANES), F32),
                        pltpu.VMEM((S5_CHUNK * LANES, ng * LANES), BF16),
                        pltpu.VMEM((S5_CHUNK * LANES, ng * LANES), BF16)],
        compiler_params=pltpu.CompilerParams(dimension_semantics=("arbitrary",), vmem_limit_bytes=VMEM_LIMIT),
        name="s5",
    )(up, are2, aim2, ldt, btr2, bti2, cre2, cim2, dsk)


def _float_to_key(x):
    bits = lax.bitcast_convert_type(x, I32)
    return bits ^ (lax.shift_right_arithmetic(bits, 31) & 0x7FFFFFFF)


def _key_to_float(key):
    return lax.bitcast_convert_type(key ^ (lax.shift_right_arithmetic(key, 31) & 0x7FFFFFFF), F32)


def _probit(p):
    t = jnp.sqrt(-2.0 * jnp.log(jnp.minimum(p, 1.0 - p)))
    z = t - ((0.010328 * t + 0.802853) * t + 2.515517) / (((0.001308 * t + 0.189269) * t + 1.432788) * t + 1.0)
    return jnp.where(p < 0.5, -z, z)


def _count_ge(sc_ref, trial, n_tiles, ktile):
    def count_tile(t, acc):
        off = pl.multiple_of(t * ktile, ktile)
        for j in range(ktile // COUNT_ROWS):
            rows = sc_ref[pl.ds(off + j * COUNT_ROWS, COUNT_ROWS), :]
            acc = jnp.where(rows >= trial, acc + 1, acc)
        return acc

    acc = lax.fori_loop(0, n_tiles, count_tile, jnp.zeros((COUNT_ROWS, trial.shape[1]), I32))
    return jnp.sum(acc, axis=0, keepdims=True).astype(F32)


def _select_threshold(sc_ref, s_min, s_max, n_allowed, n_tiles, k_sel, ktile):
    kf = float(k_sel)
    n_eff = jnp.maximum(n_allowed, kf + 1.0)
    z_t = _probit(1.0 - (kf - 0.5) / n_eff)

    def z_of(c):
        return _probit(jnp.clip(1.0 - c / n_eff, 0.5 / n_eff, 1.0 - 0.5 / n_eff))

    def body(st):
        it, t_lo, t_hi, c_lo, c_hi, z_lo, z_hi, w_lo, w_hi, side, done, thr = st
        g_lo = (z_lo - z_t) * w_lo
        g_hi = (z_hi - z_t) * w_hi
        t = t_lo + (t_hi - t_lo) * jnp.clip(g_lo / (g_lo - g_hi), 0.0, 1.0)
        k_lo, k_hi, k_t = _float_to_key(t_lo), _float_to_key(t_hi), _float_to_key(t)
        mid = lax.shift_right_arithmetic(k_lo, 1) + lax.shift_right_arithmetic(k_hi, 1) + (k_lo & k_hi & 1)
        use_mid = (k_t <= k_lo) | (k_t >= k_hi) | (c_lo - c_hi <= 4.0) | (it >= 20)
        t = _key_to_float(jnp.where(use_mid, mid, k_t))
        c = _count_ge(sc_ref, t, n_tiles, ktile)
        active = done == 0
        hit = active & (c == kf)
        new_lo = active & (c >= kf)
        new_hi = active & (c < kf)
        z_c = z_of(c)
        w_hi = jnp.where(new_lo, jnp.where(side > 0, 0.5 * w_hi, w_hi), 1.0)
        w_lo = jnp.where(new_hi, jnp.where(side < 0, 0.5 * w_lo, w_lo), 1.0)
        side = jnp.where(new_lo, 1.0, jnp.where(new_hi, -1.0, side))
        t_lo = jnp.where(new_lo, t, t_lo)
        c_lo = jnp.where(new_lo, c, c_lo)
        z_lo = jnp.where(new_lo, z_c, z_lo)
        t_hi = jnp.where(new_hi, t, t_hi)
        c_hi = jnp.where(new_hi, c, c_hi)
        z_hi = jnp.where(new_hi, z_c, z_hi)
        adjacent = active & (_float_to_key(t_hi) - 1 <= _float_to_key(t_lo))
        thr = jnp.where(hit, t, jnp.where(adjacent, t_lo, thr))
        done = jnp.where(hit | adjacent, 1, done)
        return it + 1, t_lo, t_hi, c_lo, c_hi, z_lo, z_hi, w_lo, w_hi, side, done, thr

    def cond(st):
        return jnp.logical_and(st[0] < 64, jnp.min(st[10]) < 1)

    few = n_allowed <= kf
    ones = jnp.ones(n_allowed.shape, F32)
    zeros = jnp.zeros(n_allowed.shape, F32)
    init = (jnp.int32(0), s_min, _key_to_float(_float_to_key(s_max) + 1), n_allowed, zeros,
            z_of(n_allowed), z_of(zeros), ones, ones, zeros,
            few.astype(I32), jnp.where(few, -jnp.inf, s_min))
    st = lax.fori_loop(0, SELECT_STEPS, lambda _, s: body(s), init)
    it, t_lo, t_hi, c_lo, c_hi, z_lo, z_hi, w_lo, w_hi, side, done, thr = st
    rank_hi = kf - c_hi
    rank_lo = c_lo - kf + 1.0
    use_hi = rank_hi <= EXTRACT_RANKS
    resolved = (done == 0) & (use_hi | (rank_lo <= EXTRACT_RANKS))
    top = _top_ranked(sc_ref, use_hi, t_lo, t_hi, n_tiles, ktile)
    rank = jnp.where(use_hi, rank_hi, rank_lo)
    picked = top[EXTRACT_RANKS - 1]
    for j in range(EXTRACT_RANKS - 1, 0, -1):
        picked = jnp.where(rank <= j, top[j - 1], picked)
    thr = jnp.where(resolved, jnp.where(use_hi, picked, -picked), thr)
    done = jnp.where(resolved, 1, done)
    st = (it, t_lo, t_hi, c_lo, c_hi, z_lo, z_hi, w_lo, w_hi, side, done, thr)
    return lax.while_loop(cond, body, st)[11]


def _insert_sorted(tops, x):
    out = []
    for a in tops:
        out.append(jnp.maximum(a, x))
        x = jnp.minimum(a, x)
    return out


def _top_ranked(sc_ref, use_hi, t_lo, t_hi, n_tiles, ktile):
    width = t_lo.shape[1]

    def candidates(x):
        return jnp.where(use_hi, jnp.where(x < t_hi, x, -jnp.inf), jnp.where(x >= t_lo, -x, -jnp.inf))

    def sweep_tile(t, tops):
        off = pl.multiple_of(t * ktile, ktile)
        tops = list(tops)
        for j in range(ktile // 8):
            tops = _insert_sorted(tops, candidates(sc_ref[pl.ds(off + j * 8, 8), :]))
        return tuple(tops)

    init = tuple(jnp.full((8, width), -jnp.inf, F32) for _ in range(EXTRACT_RANKS))
    tops8 = lax.fori_loop(0, n_tiles, sweep_tile, init)
    tops = [jnp.full((1, width), -jnp.inf, F32) for _ in range(EXTRACT_RANKS)]
    for a in tops8:
        for r in range(8):
            tops = _insert_sorted(tops, a[r:r + 1, :])
    return tops


def _dsa_kernel(qi_ref, wi_ref, q_ref, ki2_ref, k_ref, v_ref, o_ref,
                sc_ref, qm_ref, qg_ref, m_ref, acc_ref, rel0_ref, rel1_ref, lg0_ref, lg1_ref, *, k_sel, qblk, ktile):
    assert qblk == 2 * LANES
    hq = LANES
    qb = pl.program_id(1)
    n_tiles = lax.div(qb * qblk + qblk + ktile - 1, ktile)
    lane = lax.broadcasted_iota(I32, (hq, LANES), 1)
    n_lt = ktile // LANES
    halves = (0, 1)

    eye = (lax.broadcasted_iota(I32, (hq, LANES), 0) == lane).astype(BF16)
    for h in halves:
        rows_h = slice(h * hq, (h + 1) * hq)
        for hd in range(IDX_HEADS):
            pair = qi_ref[rows_h, (hd // 2) * LANES:(hd // 2 + 1) * LANES]
            msk = (lane < IDX_DIM) if hd % 2 == 0 else (lane >= IDX_DIM)
            qm_ref[h, hd * hq:(hd + 1) * hq, :] = jnp.where(msk, pair, jnp.zeros_like(pair))
        for hd in range(N_HEADS):
            g, r = divmod(hd, Q_PER_KV)
            qg_ref[h, g, r * hq:(r + 1) * hq, 0:HEAD_DIM] = q_ref[rows_h, hd * HEAD_DIM:(hd + 1) * HEAD_DIM]
            qg_ref[h, g, r * hq:(r + 1) * hq, HEAD_DIM:2 * HEAD_DIM] = eye

    w_t = [wi_ref[h * hq:(h + 1) * hq, :].T for h in halves]
    q_chunk = lax.shift_right_logical(qb * qblk + lax.broadcasted_iota(I32, (1, qblk), 1), 6)
    n_allowed = ((q_chunk + 1) * CHUNK).astype(F32)

    rel_refs = (rel0_ref, rel1_ref)
    lg_refs = (lg0_ref, lg1_ref)

    def tile_offset(t):
        return pl.multiple_of(t * ktile, ktile)

    def issue_logits(h, t):
        rel_refs[h][...] = _dot_nt(ki2_ref[pl.ds(tile_offset(t), ktile), :], qm_ref[h])

    def score_half(h, t, carry):
        mx, mn = carry
        off = pl.multiple_of(t * ktile, ktile)
        q_chunk_h = q_chunk[:, h * hq:(h + 1) * hq]
        for c in range(ktile // SCORE_ROWS):
            r0 = c * SCORE_ROWS
            sc = jnp.zeros((SCORE_ROWS, LANES), F32)
            for hd in range(IDX_HEADS):
                rel = rel_refs[h][r0:r0 + SCORE_ROWS, hd * hq:(hd + 1) * hq]
                sc = sc + jnp.maximum(rel, 0.0) * w_t[h][IDX_DIM + hd:IDX_DIM + hd + 1, :]
            k_chunk = lax.shift_right_logical(off + r0 + lax.broadcasted_iota(I32, (SCORE_ROWS, 1), 0), 6)
            ok = k_chunk <= q_chunk_h
            sc_ref[pl.ds(off + r0, SCORE_ROWS), h * hq:(h + 1) * hq] = jnp.where(ok, sc, NEG)
            mx = jnp.maximum(mx, jnp.where(ok, sc, -jnp.inf))
            mn = jnp.minimum(mn, jnp.where(ok, sc, jnp.inf))
        return mx, mn

    def score_tile(t, carry, last=False):
        issue_logits(1, t)
        c0 = score_half(0, t, carry[0])
        if not last:
            issue_logits(0, t + 1)
        return c0, score_half(1, t, carry[1])

    issue_logits(0, 0)
    init = (jnp.full((SCORE_ROWS, LANES), -jnp.inf, F32), jnp.full((SCORE_ROWS, LANES), jnp.inf, F32))
    carry = lax.fori_loop(0, n_tiles - 1, score_tile, (init, init))
    (mx0, mn0), (mx1, mn1) = score_tile(n_tiles - 1, carry, last=True)
    mx = jnp.concatenate([mx0, mx1], axis=1)
    mn = jnp.concatenate([mn0, mn1], axis=1)
    s_max = jnp.max(mx, axis=0, keepdims=True)
    s_min = jnp.min(mn, axis=0, keepdims=True)

    thr = _select_threshold(sc_ref, s_min, s_max, n_allowed, n_tiles, k_sel, ktile)
    thr = jnp.maximum(thr, _key_to_float(jnp.full((1, qblk), KEY_HALF_NEG + 1, I32)))

    m_ref[...] = jnp.full(m_ref.shape, -jnp.inf, F32)
    acc_ref[...] = jnp.zeros(acc_ref.shape, F32)
    g_rows = Q_PER_KV * hq

    def selection_bias(h, t):
        keep = sc_ref[pl.ds(tile_offset(t), ktile), h * hq:(h + 1) * hq] >= thr[:, h * hq:(h + 1) * hq]
        return jnp.where(keep, 0.0, NEG).astype(BF16)

    def issue_qk(h, t, g, bias_t):
        kb = jnp.concatenate([k_ref[pl.ds(tile_offset(t), ktile), g * HEAD_DIM:(g + 1) * HEAD_DIM], bias_t], axis=1)
        lg_refs[h][g] = _dot_nt(qg_ref[h, g], kb)

    def attend_half(h, t, issue_next=True):
        off = pl.multiple_of(t * ktile, ktile)
        nh, nt = (1, t) if h == 0 else (0, t + 1)
        bias_next = selection_bias(nh, nt) if issue_next else None
        for g in range(N_KV_HEADS):
            if issue_next:
                issue_qk(nh, nt, g, bias_next)
            ps, alphas = [], []
            for c in range(g_rows // ATT_ROWS):
                r0 = c * ATT_ROWS
                s = lg_refs[h][g, r0:r0 + ATT_ROWS, :]
                m_cur = s[:, 0:LANES]
                for j in range(1, n_lt):
                    m_cur = jnp.maximum(m_cur, s[:, j * LANES:(j + 1) * LANES])
                m_old = m_ref[h, g, r0:r0 + ATT_ROWS, :]
                m_new = jnp.maximum(m_old, jnp.max(m_cur, axis=1, keepdims=True))
                alphas.append(jnp.exp2(m_old - m_new))
                m_ref[h, g, r0:r0 + ATT_ROWS, :] = m_new
                ps.append(jnp.exp2(s - jnp.concatenate([m_new] * n_lt, axis=1)).astype(BF16))
            alpha = jnp.concatenate(alphas, axis=0)
            pv = _dot(jnp.concatenate(ps, axis=0), v_ref[pl.ds(off, ktile), 2 * g * HEAD_DIM:(2 * g + 2) * HEAD_DIM])
            acc_ref[h, g] = acc_ref[h, g] * jnp.concatenate([alpha, alpha], axis=1) + pv

    def attend_tile(t, carry):
        attend_half(0, t)
        attend_half(1, t)
        return carry

    bias_0 = selection_bias(0, 0)
    for g in range(N_KV_HEADS):
        issue_qk(0, 0, g, bias_0)
    lax.fori_loop(0, n_tiles - 1, attend_tile, 0)
    attend_half(0, n_tiles - 1)
    attend_half(1, n_tiles - 1, issue_next=False)

    for h in halves:
        for hd in range(N_HEADS):
            g, r = divmod(hd, Q_PER_KV)
            a = acc_ref[h, g, r * hq:(r + 1) * hq, :]
            o_ref[h * hq:(h + 1) * hq, hd * HEAD_DIM:(hd + 1) * HEAD_DIM] = (
                a[:, :HEAD_DIM] / a[:, HEAD_DIM:]).astype(o_ref.dtype)


def _dsa_branch(q, k, v, qi, ki2, wi, B, L, qblk=2 * LANES, ktile=512):
    k_sel = min(TOPK_MAX, L // 4)
    nq = L // qblk
    g_rows = Q_PER_KV * LANES
    assert L % ktile == 0 and ktile % qblk == 0

    def qrow(w):
        return pl.BlockSpec((None, qblk, w), lambda b, i: (b, i, 0))

    def whole(w):
        return pl.BlockSpec((None, L, w), lambda b, i: (b, 0, 0), pipeline_mode=pl.Buffered(1))

    r3 = lambda a: a.reshape(B, L, a.shape[-1])
    out = pl.pallas_call(
        functools.partial(_dsa_kernel, k_sel=k_sel, qblk=qblk, ktile=ktile),
        grid=(B, nq),
        in_specs=[qrow(IDX_HEADS * IDX_DIM), qrow(LANES), qrow(ATT_WIDTH), whole(LANES), whole(KV_WIDTH),
                  whole(2 * KV_WIDTH)],
        out_specs=qrow(ATT_WIDTH),
        out_shape=jax.ShapeDtypeStruct((B, L, ATT_WIDTH), BF16),
        scratch_shapes=[pltpu.VMEM((L, qblk), F32),
                        pltpu.VMEM((2, IDX_HEADS * LANES, LANES), BF16),
                        pltpu.VMEM((2, N_KV_HEADS, g_rows, 2 * HEAD_DIM), BF16),
                        pltpu.VMEM((2, N_KV_HEADS, g_rows, LANES), F32),
                        pltpu.VMEM((2, N_KV_HEADS, g_rows, 2 * HEAD_DIM), F32),
                        pltpu.VMEM((ktile, IDX_HEADS * LANES), F32),
                        pltpu.VMEM((ktile, IDX_HEADS * LANES), F32),
                        pltpu.VMEM((N_KV_HEADS, g_rows, ktile), F32),
                        pltpu.VMEM((N_KV_HEADS, g_rows, ktile), F32)],
        compiler_params=pltpu.CompilerParams(dimension_semantics=("arbitrary", "arbitrary"),
                                             vmem_limit_bytes=VMEM_LIMIT),
        name="dsa",
    )(r3(qi), r3(wi), r3(q), r3(ki2), r3(k), r3(v))
    return out.reshape(B * L, ATT_WIDTH)


def _gelu_tanh(x):
    return 0.5 * x * (1.0 + jnp.tanh(math.sqrt(2.0 / math.pi) * (x + 0.044715 * (x * x * x))))


def _merge_kernel(x_ref, y_ref, yb_ref, gate_ref, wglu_ref, wa_ref, wb_ref, wo_ref, o_ref, y_scr):
    n_st, rt, d = x_ref.shape
    rows = n_st * rt
    _store_lane_blocks(y_scr, y_ref[...].astype(F32))
    y = jnp.concatenate([_strided_rows(y_scr, s, rt, n_st) for s in range(n_st)], axis=0)
    y = _gelu_tanh(y)
    ya = y * jax.nn.sigmoid(_dot(y.astype(BF16), wglu_ref[...]))
    gate = gate_ref[...].reshape(rows, 2 * d)
    ga, gb = gate[:, :D_MODEL], gate[:, D_MODEL:]
    merged = (jax.nn.sigmoid(ga) * _dot(ya.astype(BF16), wa_ref[...])
              + jax.nn.sigmoid(gb) * _dot(yb_ref[...].reshape(rows, ATT_WIDTH), wb_ref[...]))
    out = x_ref[...].reshape(rows, d) + _dot(merged.astype(BF16), wo_ref[...])
    o_ref[...] = out.reshape(n_st, rt, d)


def _merge(x, y, yb, gate, w_glu, w_a, w_b, w_out):
    n_st, l_seg, _ = x.shape
    rt = ROW_TILE
    ws = [w.astype(BF16) for w in (w_glu, w_a, w_b, w_out)]

    def streams(w):
        return pl.BlockSpec((n_st, rt, w), lambda i: (0, i, 0))

    return pl.pallas_call(
        _merge_kernel,
        grid=(l_seg // rt,),
        in_specs=[streams(D_MODEL), pl.BlockSpec((None, rt * n_st, D_MODEL), lambda i: (i, 0, 0)),
                  streams(ATT_WIDTH), streams(2 * D_MODEL)]
                 + [pl.BlockSpec(w.shape, lambda i: (0, 0)) for w in ws],
        out_specs=streams(D_MODEL),
        out_shape=jax.ShapeDtypeStruct((n_st, l_seg, D_MODEL), F32),
        scratch_shapes=[pltpu.VMEM((D_MODEL // LANES, rt * n_st, LANES), F32)],
        compiler_params=pltpu.CompilerParams(dimension_semantics=("arbitrary",), vmem_limit_bytes=VMEM_LIMIT),
        name="merge",
    )(x, y, yb, gate, *ws)


def _rms(x, g):
    return x * lax.rsqrt(jnp.mean(x * x, axis=-1, keepdims=True) + EPS) * g


def _ffn_kernel(x_ref, g2_ref, win_ref, wout_ref, gf_ref, o_ref, *, final_norm):
    x = x_ref[...]
    h = _rms(x, g2_ref[...]).astype(BF16)
    gu = _dot(h, win_ref[...])
    g, up = gu[:, :FFN_HIDDEN], gu[:, FFN_HIDDEN:]
    act = (g * jax.nn.sigmoid(g)) * up
    x = x + _dot(act.astype(BF16), wout_ref[...])
    o_ref[...] = _rms(x, gf_ref[...]) if final_norm else x


def _ffn(x1, g2, w_ffn_in, w_ffn_out, gf, final_norm, tile):
    T = x1.shape[0]
    win, wout = w_ffn_in.astype(BF16), w_ffn_out.astype(BF16)

    def row(w):
        return pl.BlockSpec((tile, w), lambda i: (i, 0))

    def full(a):
        return pl.BlockSpec(a.shape, lambda i: (0, 0))

    return pl.pallas_call(
        functools.partial(_ffn_kernel, final_norm=final_norm),
        grid=(T // tile,),
        in_specs=[row(D_MODEL), full(g2), full(win), full(wout), full(gf)],
        out_specs=row(D_MODEL),
        out_shape=jax.ShapeDtypeStruct((T, D_MODEL), F32),
        compiler_params=pltpu.CompilerParams(dimension_semantics=("arbitrary",), vmem_limit_bytes=VMEM_LIMIT),
        name="ffn",
    )(x1, g2, win, wout, gf)


def kernel(x, norm1_g, w_in, a_re, a_im, log_dt, b_re, b_im, c_re, c_im, d_skip, w_glu,
           w_branch_a, w_branch_b, w_out, norm2_g, w_ffn_in, w_ffn_out, norm_f_g):
    B, L, D = x.shape
    depth = norm1_g.shape[0]
    n_seg = S5_STREAMS // B
    assert D == D_MODEL and S5_STREAMS % B == 0 and L % 512 == 0 and L % (n_seg * ROW_TILE) == 0
    x = x.astype(F32)
    for i in range(depth):
        up, q, k, v, qi, ki2, wi, gate = _project(x, norm1_g[i][None, :].astype(F32), w_in[i])
        y = _s5_branch(up, B, a_re[i], a_im[i], log_dt[i], b_re[i], b_im[i], c_re[i], c_im[i], d_skip[i])
        yb = _dsa_branch(q, k, v, qi, ki2, wi, B, L)
        x1 = _merge(x.reshape(S5_STREAMS, L // n_seg, D), y, yb.reshape(S5_STREAMS, L // n_seg, ATT_WIDTH), gate,
                    w_glu[i], w_branch_a[i], w_branch_b[i], w_out[i])
        x = _ffn(x1.reshape(B * L, D), norm2_g[i][None, :].astype(F32), w_ffn_in[i], w_ffn_out[i],
                 norm_f_g[None, :].astype(F32), i == depth - 1, FFN_ROWS).reshape(B, L, D)
    return x
```

```python
import functools
import math

import jax
import jax.numpy as jnp
import numpy as np
from jax import lax
from jax.experimental import pallas as pl
from jax.experimental.pallas import tpu as pltpu

F32 = jnp.float32
BF16 = jnp.bfloat16
I32 = jnp.int32

D_MODEL = 1024
CHUNK = 64
EPS = 1e-6
NEG = -1e30

SSM_GROUP = 16
SSM_GROUPS = 64
SSM_STATE = 64
S5_CHUNK = 16
S5_STREAMS = 16
S5_LANE_GROUPS = 8
ROW_TILE = S5_CHUNK
FFN_ROWS = 256

N_HEADS = 8
HEAD_DIM = 128
N_KV_HEADS = 2
Q_PER_KV = N_HEADS // N_KV_HEADS
ATT_WIDTH = N_HEADS * HEAD_DIM
KV_WIDTH = N_KV_HEADS * HEAD_DIM
IDX_HEADS = 16
IDX_DIM = 64
TOPK_MAX = 256
ROPE_THETA = 500000.0
ATT_ROT = HEAD_DIM // 4
IDX_ROT = IDX_DIM // 4
FFN_HIDDEN = -(-8 * D_MODEL // (3 * 256)) * 256

SCORE_ROWS = 16
COUNT_ROWS = 32
SELECT_STEPS = 7
EXTRACT_RANKS = 4
ATT_ROWS = 32

LANES = 128
INT_MIN = -(2 ** 31)

VMEM_LIMIT = 56 * 1024 * 1024


def _key_of_float(val):
    bits = int(np.float32(val).view(np.int32))
    return bits ^ ((bits >> 31) & 0x7FFFFFFF)


KEY_HALF_NEG = _key_of_float(0.5 * NEG)


def _dot(a, b):
    return jnp.dot(a, b, preferred_element_type=F32)


def _dot_nt(a, b):
    return lax.dot_general(a, b, (((1,), (1,)), ((), ())), preferred_element_type=F32)


def _split_hi_lo(a):
    hi = a.astype(BF16)
    lo = (a - hi.astype(F32)).astype(BF16)
    return hi, lo


def _rope(x, c, s1, s2, half):
    n = x.shape[-1]
    return x * c + pltpu.roll(x, half, 1) * s1 + pltpu.roll(x, n - half, 1) * s2


def _store_lane_blocks(scr, val):
    for j in range(scr.shape[0]):
        scr[j] = val[:, j * LANES:(j + 1) * LANES]


def _strided_rows(scr, start, size, stride):
    return jnp.concatenate([scr[j, pl.ds(start, size, stride=stride), :] for j in range(scr.shape[0])], axis=1)


def _proj_kernel(x_ref, g1_ref, wu_ref, wq_ref, wk_ref, wv_ref, wqi_ref, wsm_ref, wg_ref,
                 ca_ref, sa1_ref, sa2_ref, ci_ref, si1_ref, si2_ref, cs_ref, ss1_ref, ss2_ref,
                 u_ref, q_ref, k_ref, v_ref, qi_ref, ki2_ref, wi_ref, gate_ref, u_scr, *, batch):
    n_st, rt, d = x_ref.shape
    rows = n_st * rt
    x = x_ref[...].reshape(rows, d)
    h = x * lax.rsqrt(jnp.mean(x * x, axis=-1, keepdims=True) + EPS) * g1_ref[...]
    hb = h.astype(BF16)

    def put(ref, val, sl=slice(None)):
        ref[:, :, sl] = val.reshape(n_st, rt, val.shape[-1])

    def table(ref):
        t = ref[...].reshape(rows // batch, LANES)
        return jnp.concatenate([t] * batch, axis=0)

    ca, sa1, sa2 = table(ca_ref), table(sa1_ref), table(sa2_ref)
    q = _dot(hb, wq_ref[...])
    scale = HEAD_DIM ** -0.5 * math.log2(math.e)
    for hd in range(N_HEADS):
        sl = slice(hd * HEAD_DIM, (hd + 1) * HEAD_DIM)
        put(q_ref, (_rope(q[:, sl], ca, sa1, sa2, ATT_ROT // 2) * scale).astype(BF16), sl)
    k = _dot(hb, wk_ref[...])
    for hd in range(N_KV_HEADS):
        sl = slice(hd * HEAD_DIM, (hd + 1) * HEAD_DIM)
        put(k_ref, _rope(k[:, sl], ca, sa1, sa2, ATT_ROT // 2).astype(BF16), sl)

    ci, si1, si2 = table(ci_ref), table(si1_ref), table(si2_ref)
    qi = _dot(hb, wqi_ref[...])
    for pr in range(IDX_HEADS * IDX_DIM // LANES):
        sl = slice(pr * LANES, (pr + 1) * LANES)
        put(qi_ref, _rope(qi[:, sl], ci, si1, si2, IDX_ROT // 2).astype(BF16), sl)

    sm = _dot(hb, wsm_ref[...])
    sm = _rope(sm, table(cs_ref), table(ss1_ref), table(ss2_ref), IDX_ROT // 2)
    lane = lax.broadcasted_iota(I32, sm.shape, 1)
    put(ki2_ref, jnp.where(lane < IDX_DIM, sm, pltpu.roll(sm, IDX_DIM, 1)).astype(BF16))
    put(wi_ref, sm * (IDX_DIM ** -0.5 * IDX_HEADS ** -0.5))

    _store_lane_blocks(u_scr, _dot(hb, wu_ref[...]))
    for t in range(rt):
        u_ref[t * n_st:(t + 1) * n_st, :] = _strided_rows(u_scr, t, n_st, rt).astype(BF16)

    v = _dot(hb, wv_ref[...]).astype(BF16)
    ones = jnp.ones((rows, HEAD_DIM), BF16)
    for hd in range(N_KV_HEADS):
        put(v_ref, v[:, hd * HEAD_DIM:(hd + 1) * HEAD_DIM], slice(2 * hd * HEAD_DIM, (2 * hd + 1) * HEAD_DIM))
        put(v_ref, ones, slice((2 * hd + 1) * HEAD_DIM, (2 * hd + 2) * HEAD_DIM))
    put(gate_ref, _dot(hb, wg_ref[...]))


def _rope_tables(L, rot, period, active_lanes):
    half = rot // 2
    pos = jnp.arange(L, dtype=jnp.int32)
    inv_freq = ROPE_THETA ** (-jnp.arange(half, dtype=F32) / half)
    ang = pos.astype(F32)[:, None] * inv_freq[None, :]
    cos, sin = jnp.cos(ang), jnp.sin(ang)
    lane = np.arange(LANES)
    within = lane % period
    fidx = np.where(within < rot, within % half, 0)
    is_x1 = (within < half) & (lane < active_lanes)
    is_x2 = (within >= half) & (within < rot) & (lane < active_lanes)
    rot_lane = is_x1 | is_x2
    c = jnp.where(rot_lane[None, :], cos[:, fidx], 1.0)
    s1 = jnp.where(is_x2[None, :], sin[:, fidx], 0.0)
    s2 = jnp.where(is_x1[None, :], -sin[:, fidx], 0.0)
    return c.astype(F32), s1.astype(F32), s2.astype(F32)


def _project(x, g1, w_in):
    B, L, _ = x.shape
    n_seg = S5_STREAMS // B
    l_seg = L // n_seg
    rt = ROW_TILE
    o = 0
    parts = []
    for s in (D_MODEL, ATT_WIDTH, KV_WIDTH, KV_WIDTH, IDX_HEADS * IDX_DIM, IDX_DIM, IDX_HEADS, D_MODEL, D_MODEL):
        parts.append(w_in[:, o:o + s])
        o += s
    wu, wq, wk, wv, wqi, wki, wwi, wga, wgb = parts
    wsm = jnp.concatenate([wki, wwi, jnp.zeros((D_MODEL, LANES - IDX_DIM - IDX_HEADS), w_in.dtype)], axis=1)
    wg = jnp.concatenate([wga, wgb], axis=1)
    ws = [w.astype(BF16) for w in (wu, wq, wk, wv, wqi, wsm, wg)]
    tabs = (_rope_tables(L, ATT_ROT, HEAD_DIM, LANES)
            + _rope_tables(L, IDX_ROT, IDX_DIM, LANES)
            + _rope_tables(L, IDX_ROT, IDX_DIM, IDX_DIM))
    tabs = [t.reshape(n_seg, l_seg, LANES) for t in tabs]

    def streams(w):
        return pl.BlockSpec((S5_STREAMS, rt, w), lambda i: (0, i, 0))

    def full(a):
        return pl.BlockSpec(a.shape, lambda i: (0, 0))

    tab_spec = pl.BlockSpec((n_seg, rt, LANES), lambda i: (0, i, 0))
    widths = (ATT_WIDTH,
              KV_WIDTH,
              2 * KV_WIDTH,
              IDX_HEADS * IDX_DIM,
              LANES,
              LANES,
              2 * D_MODEL)
    dtypes = (BF16, BF16, BF16, BF16, BF16, F32, F32)
    out_shape = ((jax.ShapeDtypeStruct((l_seg // rt, rt * S5_STREAMS, D_MODEL), BF16),)
                 + tuple(jax.ShapeDtypeStruct((S5_STREAMS, l_seg, w), dt) for w, dt in zip(widths, dtypes)))
    out_specs = ((pl.BlockSpec((None, rt * S5_STREAMS, D_MODEL), lambda i: (i, 0, 0)),)
                 + tuple(streams(w) for w in widths))
    return pl.pallas_call(
        functools.partial(_proj_kernel, batch=B),
        grid=(l_seg // rt,),
        in_specs=[streams(D_MODEL), full(g1)] + [full(w) for w in ws] + [tab_spec] * 9,
        out_specs=out_specs,
        out_shape=out_shape,
        scratch_shapes=[pltpu.VMEM((D_MODEL // LANES, rt * S5_STREAMS, LANES), F32)],
        compiler_params=pltpu.CompilerParams(dimension_semantics=("arbitrary",), vmem_limit_bytes=VMEM_LIMIT),
        name="proj",
    )(x.reshape(S5_STREAMS, l_seg, D_MODEL), g1, *ws, *tabs)


def _s5_group_weights(ldt, ar, ai, btr, bti, cre, cim):
    lane = lax.broadcasted_iota(I32, (1, LANES), 1)
    lo = lane < SSM_STATE
    dt = jnp.exp(ldt)
    rho, th = ar * dt, ai * dt
    npow = S5_CHUNK + 1
    nn = lax.broadcasted_iota(I32, (24, LANES), 0).astype(F32)
    mag = jnp.exp(nn * rho)
    lc = mag * jnp.cos(nn * th)
    ls = mag * jnp.sin(nn * th)
    lam_pk = jnp.where(lo, lc, ls)
    lam_sw = jnp.where(lo, -ls, lc)
    num_re, num_im = lc[1:2] - 1.0, ls[1:2]
    den = ar * ar + ai * ai
    f_re = (num_re * ar + num_im * ai) / den
    f_im = (num_im * ar - num_re * ai) / den
    g_re = btr * f_re - bti * f_im
    g_im = btr * f_im + bti * f_re
    g_neg = jnp.where(lo, g_re, -g_im)
    w_pk = [cre * lam_pk[n:n + 1] + cim * lam_sw[n:n + 1] for n in range(npow)]
    w_state = [g_re * lam_pk[S5_CHUNK - 1 - ti:S5_CHUNK - ti] + g_im * lam_sw[S5_CHUNK - 1 - ti:S5_CHUNK - ti]
               for ti in range(S5_CHUNK)]
    a_p = jnp.where(lo, lam_pk[S5_CHUNK:S5_CHUNK + 1], lam_sw[S5_CHUNK:S5_CHUNK + 1])
    a_q = jnp.where(lo, lam_sw[S5_CHUNK:S5_CHUNK + 1], lam_pk[S5_CHUNK:S5_CHUNK + 1])
    return g_neg, w_pk, w_state, a_p, a_q


def _place(block, g):
    z = jnp.zeros_like(block)
    return jnp.concatenate([block if j == g else z for j in range(S5_LANE_GROUPS)], axis=1)


def _swap_halves(x):
    return jnp.concatenate([pltpu.roll(x[:, g * LANES:(g + 1) * LANES], SSM_STATE, 1)
                            for g in range(x.shape[1] // LANES)], axis=1)


def _s5_kernel(up_ref, are_ref, aim_ref, ldt_ref, btr_ref, bti_ref, cre_ref, cim_ref, dsk_ref,
               y_ref, s_ref, ssw_ref, hprev_ref, kbd_ref, wst_ref, wot_ref, *, n_steps, n_seg):
    ng = S5_LANE_GROUPS
    lane = lax.broadcasted_iota(I32, (1, LANES), 1)
    lo = lane < SSM_STATE
    weights = [_s5_group_weights(ldt_ref[g], are_ref[g], aim_ref[g], btr_ref[g], bti_ref[g], cre_ref[g], cim_ref[g])
               for g in range(ng)]
    gneg_blk = jnp.concatenate([_place(weights[g][0], g) for g in range(ng)], axis=0)
    for tau in range(S5_CHUNK):
        wt = jnp.concatenate([_place(weights[g][1][tau], g) for g in range(ng)], axis=0)
        kbd_ref[:, tau * LANES:(tau + 1) * LANES] = lax.dot_general(
            gneg_blk, wt, (((1,), (1,)), ((), ())), preferred_element_type=F32, precision=lax.Precision.HIGHEST)
    for ti in range(S5_CHUNK):
        wst_ref[ti * LANES:(ti + 1) * LANES, :] = jnp.concatenate(
            [_place(weights[g][2][ti], g) for g in range(ng)], axis=0).astype(BF16)
        wot_ref[ti * LANES:(ti + 1) * LANES, :] = jnp.concatenate(
            [_place(jnp.where(lo, weights[g][1][ti + 1], -weights[g][1][ti + 1]), g) for g in range(ng)],
            axis=0).astype(BF16)
    a_p = jnp.concatenate([weights[g][3] for g in range(ng)], axis=1)
    a_q = jnp.concatenate([weights[g][4] for g in range(ng)], axis=1)

    rows = n_steps * S5_STREAMS

    def chunk_rows(ti):
        return up_ref[:, ti * S5_STREAMS:(ti + 1) * S5_STREAMS, :].reshape(rows, LANES)

    lhs = jnp.concatenate([chunk_rows(ti) for ti in range(S5_CHUNK)], axis=1)
    s_all = _dot(lhs, wst_ref[...])
    s_ref[...] = s_all
    ssw_ref[...] = _swap_halves(s_all)

    def cmul(x, p, q):
        return x * p + _swap_halves(x) * q

    def step(h, hs, i):
        rows_i = pl.ds(pl.multiple_of(i * S5_STREAMS, S5_STREAMS), S5_STREAMS)
        return h * a_p + hs * a_q + s_ref[rows_i, :], hs * a_p - h * a_q + ssw_ref[rows_i, :]

    zero = jnp.zeros((S5_STREAMS, ng * LANES), F32)
    z, _ = lax.fori_loop(0, n_steps, lambda i, c: step(c[0], c[1], i), (zero, zero))

    lo_all = jnp.concatenate([lo] * ng, axis=1)

    def factors(zpk):
        sw = _swap_halves(zpk)
        return jnp.where(lo_all, zpk, sw), jnp.where(lo_all, -sw, zpk)

    base, seg, n = jnp.where(lo_all, a_p, a_q), None, n_steps
    while n:
        if n & 1:
            seg = base if seg is None else cmul(seg, *factors(base))
        n >>= 1
        if n:
            base = cmul(base, *factors(base))
    seg_p, seg_q = factors(seg)

    row_id = lax.broadcasted_iota(I32, (S5_STREAMS, ng * LANES), 0)
    init = zero
    prev = None
    for s in range(S5_STREAMS):
        if s % n_seg == 0:
            cur = jnp.zeros((1, ng * LANES), F32)
        else:
            cur = cmul(prev, seg_p, seg_q) + z[s - 1:s]
            init = jnp.where(row_id == s, cur, init)
        prev = cur

    def pass2(i, c):
        hprev_ref[pl.ds(pl.multiple_of(i * S5_STREAMS, S5_STREAMS), S5_STREAMS), :] = c[0]
        return step(c[0], c[1], i)

    lax.fori_loop(0, n_steps, pass2, (init, _swap_halves(init)))

    hp = hprev_ref[...].astype(BF16)
    dsk = dsk_ref[...]
    zblk = jnp.zeros((LANES, LANES), F32)
    for tp in range(S5_CHUNK // 2):
        kdim = (2 * tp + 2) * LANES
        cols = []
        for t2 in (2 * tp, 2 * tp + 1):
            cols.append(jnp.concatenate(
                [kbd_ref[:, (t2 - ti) * LANES:(t2 - ti + 1) * LANES] if ti <= t2 else zblk
                 for ti in range(2 * tp + 2)], axis=0))
        slab = jnp.concatenate(cols, axis=1).astype(BF16)
        y = _dot(lhs[:, :kdim], slab) + _dot_nt(hp, wot_ref[2 * tp * LANES:(2 * tp + 2) * LANES, :])
        for half in range(2):
            t2 = 2 * tp + half
            u_t = up_ref[:, t2 * S5_STREAMS:(t2 + 1) * S5_STREAMS, :].astype(F32)
            y_t = y[:, half * LANES:(half + 1) * LANES].reshape(n_steps, S5_STREAMS, LANES) + u_t * dsk
            y_ref[:, t2 * S5_STREAMS:(t2 + 1) * S5_STREAMS, :] = y_t.astype(y_ref.dtype)


def _s5_branch(up, B, a_re, a_im, log_dt, b_re, b_im, c_re, c_im, d_skip):
    G, C = SSM_GROUPS, SSM_GROUP
    n_steps = up.shape[0]
    n_seg = S5_STREAMS // B
    rows = n_steps * S5_STREAMS
    ng = S5_LANE_GROUPS

    def dup(a):
        return jnp.concatenate([a, a], axis=-1).astype(F32)

    are2 = dup(a_re)[:, None, :]
    aim2 = dup(a_im)[:, None, :]
    ldt = log_dt.astype(F32)[:, None, None]
    btr2 = dup(jnp.swapaxes(b_re, 1, 2))
    bti2 = dup(jnp.swapaxes(b_im, 1, 2))
    cre2 = dup(c_re)
    cim2 = dup(c_im)
    dsk = d_skip.astype(F32).reshape(G // ng, 1, ng * C)

    def per_block(shape):
        return pl.BlockSpec((ng,) + shape, lambda i: (i, 0, 0))

    act = pl.BlockSpec((n_steps, S5_CHUNK * S5_STREAMS, LANES), lambda i: (0, 0, i))
    return pl.pallas_call(
        functools.partial(_s5_kernel, n_steps=n_steps, n_seg=n_seg),
        grid=(G // ng,),
        in_specs=[act, per_block((1, LANES)), per_block((1, LANES)), per_block((1, 1)),
                  per_block((C, LANES)), per_block((C, LANES)), per_block((C, LANES)), per_block((C, LANES)),
                  pl.BlockSpec((None, 1, LANES), lambda i: (i, 0, 0))],
        out_specs=act,
        out_shape=jax.ShapeDtypeStruct(up.shape, BF16),
        scratch_shapes=[pltpu.VMEM((rows, ng * LANES), F32),
                        pltpu.VMEM((rows, ng * LANES), F32),
                        pltpu.VMEM((rows, ng * LANES), F32),
                        pltpu.VMEM((LANES, S5_CHUNK * LANES), F32),
                        pltpu.VMEM((S5_CHUNK * LANES, ng * LANES), BF16),
                        pltpu.VMEM((S5_CHUNK * LANES, ng * LANES), BF16)],
        compiler_params=pltpu.CompilerParams(dimension_semantics=("arbitrary",), vmem_limit_bytes=VMEM_LIMIT),
        name="s5",
    )(up, are2, aim2, ldt, btr2, bti2, cre2, cim2, dsk)


def _float_to_key(x):
    bits = lax.bitcast_convert_type(x, I32)
    return bits ^ (lax.shift_right_arithmetic(bits, 31) & 0x7FFFFFFF)


def _key_to_float(key):
    return lax.bitcast_convert_type(key ^ (lax.shift_right_arithmetic(key, 31) & 0x7FFFFFFF), F32)


def _probit(p):
    t = jnp.sqrt(-2.0 * jnp.log(jnp.minimum(p, 1.0 - p)))
    z = t - ((0.010328 * t + 0.802853) * t + 2.515517) / (((0.001308 * t + 0.189269) * t + 1.432788) * t + 1.0)
    return jnp.where(p < 0.5, -z, z)


def _count_ge(sc_ref, trial, n_tiles, ktile):
    def count_tile(t, acc):
        off = pl.multiple_of(t * ktile, ktile)
        for j in range(ktile // COUNT_ROWS):
            rows = sc_ref[pl.ds(off + j * COUNT_ROWS, COUNT_ROWS), :]
            acc = jnp.where(rows >= trial, acc + 1, acc)
        return acc

    acc = lax.fori_loop(0, n_tiles, count_tile, jnp.zeros((COUNT_ROWS, trial.shape[1]), I32))
    return jnp.sum(acc, axis=0, keepdims=True).astype(F32)


def _select_threshold(sc_ref, s_min, s_max, n_allowed, n_tiles, k_sel, ktile):
    kf = float(k_sel)
    n_eff = jnp.maximum(n_allowed, kf + 1.0)
    z_t = _probit(1.0 - (kf - 0.5) / n_eff)

    def z_of(c):
        return _probit(jnp.clip(1.0 - c / n_eff, 0.5 / n_eff, 1.0 - 0.5 / n_eff))

    def body(st):
        it, t_lo, t_hi, c_lo, c_hi, z_lo, z_hi, w_lo, w_hi, side, done, thr = st
        g_lo = (z_lo - z_t) * w_lo
        g_hi = (z_hi - z_t) * w_hi
        t = t_lo + (t_hi - t_lo) * jnp.clip(g_lo / (g_lo - g_hi), 0.0, 1.0)
        k_lo, k_hi, k_t = _float_to_key(t_lo), _float_to_key(t_hi), _float_to_key(t)
        mid = lax.shift_right_arithmetic(k_lo, 1) + lax.shift_right_arithmetic(k_hi, 1) + (k_lo & k_hi & 1)
        use_mid = (k_t <= k_lo) | (k_t >= k_hi) | (c_lo - c_hi <= 4.0) | (it >= 20)
        t = _key_to_float(jnp.where(use_mid, mid, k_t))
        c = _count_ge(sc_ref, t, n_tiles, ktile)
        active = done == 0
        hit = active & (c == kf)
        new_lo = active & (c >= kf)
        new_hi = active & (c < kf)
        z_c = z_of(c)
        w_hi = jnp.where(new_lo, jnp.where(side > 0, 0.5 * w_hi, w_hi), 1.0)
        w_lo = jnp.where(new_hi, jnp.where(side < 0, 0.5 * w_lo, w_lo), 1.0)
        side = jnp.where(new_lo, 1.0, jnp.where(new_hi, -1.0, side))
        t_lo = jnp.where(new_lo, t, t_lo)
        c_lo = jnp.where(new_lo, c, c_lo)
        z_lo = jnp.where(new_lo, z_c, z_lo)
        t_hi = jnp.where(new_hi, t, t_hi)
        c_hi = jnp.where(new_hi, c, c_hi)
        z_hi = jnp.where(new_hi, z_c, z_hi)
        adjacent = active & (_float_to_key(t_hi) - 1 <= _float_to_key(t_lo))
        thr = jnp.where(hit, t, jnp.where(adjacent, t_lo, thr))
        done = jnp.where(hit | adjacent, 1, done)
        return it + 1, t_lo, t_hi, c_lo, c_hi, z_lo, z_hi, w_lo, w_hi, side, done, thr

    def cond(st):
        return jnp.logical_and(st[0] < 64, jnp.min(st[10]) < 1)

    few = n_allowed <= kf
    ones = jnp.ones(n_allowed.shape, F32)
    zeros = jnp.zeros(n_allowed.shape, F32)
    init = (jnp.int32(0), s_min, _key_to_float(_float_to_key(s_max) + 1), n_allowed, zeros,
            z_of(n_allowed), z_of(zeros), ones, ones, zeros,
            few.astype(I32), jnp.where(few, -jnp.inf, s_min))
    st = lax.fori_loop(0, SELECT_STEPS, lambda _, s: body(s), init)
    it, t_lo, t_hi, c_lo, c_hi, z_lo, z_hi, w_lo, w_hi, side, done, thr = st
    rank_hi = kf - c_hi
    rank_lo = c_lo - kf + 1.0
    use_hi = rank_hi <= EXTRACT_RANKS
    resolved = (done == 0) & (use_hi | (rank_lo <= EXTRACT_RANKS))
    top = _top_ranked(sc_ref, use_hi, t_lo, t_hi, n_tiles, ktile)
    rank = jnp.where(use_hi, rank_hi, rank_lo)
    picked = top[EXTRACT_RANKS - 1]
    for j in range(EXTRACT_RANKS - 1, 0, -1):
        picked = jnp.where(rank <= j, top[j - 1], picked)
    thr = jnp.where(resolved, jnp.where(use_hi, picked, -picked), thr)
    done = jnp.where(resolved, 1, done)
    st = (it, t_lo, t_hi, c_lo, c_hi, z_lo, z_hi, w_lo, w_hi, side, done, thr)
    return lax.while_loop(cond, body, st)[11]


def _insert_sorted(tops, x):
    out = []
    for a in tops:
        out.append(jnp.maximum(a, x))
        x = jnp.minimum(a, x)
    return out


def _top_ranked(sc_ref, use_hi, t_lo, t_hi, n_tiles, ktile):
    width = t_lo.shape[1]
    sign = jnp.where(use_hi, 1.0, -1.0)
    bound = jnp.where(use_hi, t_hi, _key_to_float(_float_to_key(-t_lo) + 1))

    def candidates(x):
        y = x * sign
        return jnp.where(y < bound, y, -jnp.inf)

    def sweep_tile(t, tops):
        off = pl.multiple_of(t * ktile, ktile)
        tops = list(tops)
        for j in range(ktile // 8):
            tops = _insert_sorted(tops, candidates(sc_ref[pl.ds(off + j * 8, 8), :]))
        return tuple(tops)

    init = tuple(jnp.full((8, width), -jnp.inf, F32) for _ in range(EXTRACT_RANKS))
    tops8 = lax.fori_loop(0, n_tiles, sweep_tile, init)
    tops = [jnp.full((1, width), -jnp.inf, F32) for _ in range(EXTRACT_RANKS)]
    for a in tops8:
        for r in range(8):
            tops = _insert_sorted(tops, a[r:r + 1, :])
    return tops


def _dsa_kernel(qi_ref, wi_ref, q_ref, ki2_ref, k_ref, v_ref, o_ref,
                sc_ref, qm_ref, qg_ref, m_ref, acc_ref, rel0_ref, rel1_ref, lg0_ref, lg1_ref, *, k_sel, qblk, ktile):
    assert qblk == 2 * LANES
    hq = LANES
    qb = pl.program_id(1)
    n_tiles = lax.div(qb * qblk + qblk + ktile - 1, ktile)
    lane = lax.broadcasted_iota(I32, (hq, LANES), 1)
    n_lt = ktile // LANES
    halves = (0, 1)

    eye = (lax.broadcasted_iota(I32, (hq, LANES), 0) == lane).astype(BF16)
    for h in halves:
        rows_h = slice(h * hq, (h + 1) * hq)
        for hd in range(IDX_HEADS):
            pair = qi_ref[rows_h, (hd // 2) * LANES:(hd // 2 + 1) * LANES]
            msk = (lane < IDX_DIM) if hd % 2 == 0 else (lane >= IDX_DIM)
            qm_ref[h, hd * hq:(hd + 1) * hq, :] = jnp.where(msk, pair, jnp.zeros_like(pair))
        for hd in range(N_HEADS):
            g, r = divmod(hd, Q_PER_KV)
            qg_ref[h, g, r * hq:(r + 1) * hq, 0:HEAD_DIM] = q_ref[rows_h, hd * HEAD_DIM:(hd + 1) * HEAD_DIM]
            qg_ref[h, g, r * hq:(r + 1) * hq, HEAD_DIM:2 * HEAD_DIM] = eye

    w_t = [wi_ref[h * hq:(h + 1) * hq, :].T for h in halves]
    q_chunk = lax.shift_right_logical(qb * qblk + lax.broadcasted_iota(I32, (1, qblk), 1), 6)
    n_allowed = ((q_chunk + 1) * CHUNK).astype(F32)

    rel_refs = (rel0_ref, rel1_ref)
    lg_refs = (lg0_ref, lg1_ref)

    def tile_offset(t):
        return pl.multiple_of(t * ktile, ktile)

    def issue_logits(h, t):
        rel_refs[h][...] = _dot_nt(ki2_ref[pl.ds(tile_offset(t), ktile), :], qm_ref[h])

    def score_half(h, t, carry):
        mx, mn = carry
        off = pl.multiple_of(t * ktile, ktile)
        q_chunk_h = q_chunk[:, h * hq:(h + 1) * hq]
        for c in range(ktile // SCORE_ROWS):
            r0 = c * SCORE_ROWS
            sc = jnp.zeros((SCORE_ROWS, LANES), F32)
            for hd in range(IDX_HEADS):
                rel = rel_refs[h][r0:r0 + SCORE_ROWS, hd * hq:(hd + 1) * hq]
                sc = sc + jnp.maximum(rel, 0.0) * w_t[h][IDX_DIM + hd:IDX_DIM + hd + 1, :]
            k_chunk = lax.shift_right_logical(off + r0 + lax.broadcasted_iota(I32, (SCORE_ROWS, 1), 0), 6)
            ok = k_chunk <= q_chunk_h
            sc_ref[pl.ds(off + r0, SCORE_ROWS), h * hq:(h + 1) * hq] = jnp.where(ok, sc, NEG)
            mx = jnp.maximum(mx, jnp.where(ok, sc, -jnp.inf))
            mn = jnp.minimum(mn, jnp.where(ok, sc, jnp.inf))
        return mx, mn

    def score_tile(t, carry, last=False):
        issue_logits(1, t)
        c0 = score_half(0, t, carry[0])
        if not last:
            issue_logits(0, t + 1)
        return c0, score_half(1, t, carry[1])

    issue_logits(0, 0)
    init = (jnp.full((SCORE_ROWS, LANES), -jnp.inf, F32), jnp.full((SCORE_ROWS, LANES), jnp.inf, F32))
    carry = lax.fori_loop(0, n_tiles - 1, score_tile, (init, init))
    (mx0, mn0), (mx1, mn1) = score_tile(n_tiles - 1, carry, last=True)
    mx = jnp.concatenate([mx0, mx1], axis=1)
    mn = jnp.concatenate([mn0, mn1], axis=1)
    s_max = jnp.max(mx, axis=0, keepdims=True)
    s_min = jnp.min(mn, axis=0, keepdims=True)

    thr = _select_threshold(sc_ref, s_min, s_max, n_allowed, n_tiles, k_sel, ktile)
    thr = jnp.maximum(thr, _key_to_float(jnp.full((1, qblk), KEY_HALF_NEG + 1, I32)))

    m_ref[...] = jnp.full(m_ref.shape, -jnp.inf, F32)
    acc_ref[...] = jnp.zeros(acc_ref.shape, F32)
    g_rows = Q_PER_KV * hq

    def selection_bias(h, t):
        keep = sc_ref[pl.ds(tile_offset(t), ktile), h * hq:(h + 1) * hq] >= thr[:, h * hq:(h + 1) * hq]
        return jnp.where(keep, 0.0, NEG).astype(BF16)

    def issue_qk(h, t, g, bias_t):
        kb = jnp.concatenate([k_ref[pl.ds(tile_offset(t), ktile), g * HEAD_DIM:(g + 1) * HEAD_DIM], bias_t], axis=1)
        lg_refs[h][g] = _dot_nt(qg_ref[h, g], kb)

    def attend_half(h, t, issue_next=True):
        off = pl.multiple_of(t * ktile, ktile)
        nh, nt = (1, t) if h == 0 else (0, t + 1)
        bias_next = selection_bias(nh, nt) if issue_next else None
        for g in range(N_KV_HEADS):
            if issue_next:
                issue_qk(nh, nt, g, bias_next)
            ps, alphas = [], []
            for c in range(g_rows // ATT_ROWS):
                r0 = c * ATT_ROWS
                s = lg_refs[h][g, r0:r0 + ATT_ROWS, :]
                m_cur = s[:, 0:LANES]
                for j in range(1, n_lt):
                    m_cur = jnp.maximum(m_cur, s[:, j * LANES:(j + 1) * LANES])
                m_old = m_ref[h, g, r0:r0 + ATT_ROWS, :]
                m_new = jnp.maximum(m_old, jnp.max(m_cur, axis=1, keepdims=True))
                alphas.append(jnp.exp2(m_old - m_new))
                m_ref[h, g, r0:r0 + ATT_ROWS, :] = m_new
                ps.append(jnp.exp2(s - jnp.concatenate([m_new] * n_lt, axis=1)).astype(BF16))
            alpha = jnp.concatenate(alphas, axis=0)
            pv = _dot(jnp.concatenate(ps, axis=0), v_ref[pl.ds(off, ktile), 2 * g * HEAD_DIM:(2 * g + 2) * HEAD_DIM])
            acc_ref[h, g] = acc_ref[h, g] * jnp.concatenate([alpha, alpha], axis=1) + pv

    def attend_tile(t, carry):
        attend_half(0, t)
        attend_half(1, t)
        return carry

    bias_0 = selection_bias(0, 0)
    for g in range(N_KV_HEADS):
        issue_qk(0, 0, g, bias_0)
    lax.fori_loop(0, n_tiles - 1, attend_tile, 0)
    attend_half(0, n_tiles - 1)
    attend_half(1, n_tiles - 1, issue_next=False)

    for h in halves:
        for hd in range(N_HEADS):
            g, r = divmod(hd, Q_PER_KV)
            a = acc_ref[h, g, r * hq:(r + 1) * hq, :]
            o_ref[h * hq:(h + 1) * hq, hd * HEAD_DIM:(hd + 1) * HEAD_DIM] = (
                a[:, :HEAD_DIM] / a[:, HEAD_DIM:]).astype(o_ref.dtype)


def _dsa_branch(q, k, v, qi, ki2, wi, B, L, qblk=2 * LANES, ktile=512):
    k_sel = min(TOPK_MAX, L // 4)
    nq = L // qblk
    g_rows = Q_PER_KV * LANES
    assert L % ktile == 0 and ktile % qblk == 0

    def qrow(w):
        return pl.BlockSpec((None, qblk, w), lambda b, i: (b, i, 0))

    def whole(w):
        return pl.BlockSpec((None, L, w), lambda b, i: (b, 0, 0), pipeline_mode=pl.Buffered(1))

    r3 = lambda a: a.reshape(B, L, a.shape[-1])
    out = pl.pallas_call(
        functools.partial(_dsa_kernel, k_sel=k_sel, qblk=qblk, ktile=ktile),
        grid=(B, nq),
        in_specs=[qrow(IDX_HEADS * IDX_DIM), qrow(LANES), qrow(ATT_WIDTH), whole(LANES), whole(KV_WIDTH),
                  whole(2 * KV_WIDTH)],
        out_specs=qrow(ATT_WIDTH),
        out_shape=jax.ShapeDtypeStruct((B, L, ATT_WIDTH), BF16),
        scratch_shapes=[pltpu.VMEM((L, qblk), F32),
                        pltpu.VMEM((2, IDX_HEADS * LANES, LANES), BF16),
                        pltpu.VMEM((2, N_KV_HEADS, g_rows, 2 * HEAD_DIM), BF16),
                        pltpu.VMEM((2, N_KV_HEADS, g_rows, LANES), F32),
                        pltpu.VMEM((2, N_KV_HEADS, g_rows, 2 * HEAD_DIM), F32),
                        pltpu.VMEM((ktile, IDX_HEADS * LANES), F32),
                        pltpu.VMEM((ktile, IDX_HEADS * LANES), F32),
                        pltpu.VMEM((N_KV_HEADS, g_rows, ktile), F32),
                        pltpu.VMEM((N_KV_HEADS, g_rows, ktile), F32)],
        compiler_params=pltpu.CompilerParams(dimension_semantics=("arbitrary", "arbitrary"),
                                             vmem_limit_bytes=VMEM_LIMIT),
        name="dsa",
    )(r3(qi), r3(wi), r3(q), r3(ki2), r3(k), r3(v))
    return out.reshape(B * L, ATT_WIDTH)


def _gelu_tanh(x):
    return 0.5 * x * (1.0 + jnp.tanh(math.sqrt(2.0 / math.pi) * (x + 0.044715 * (x * x * x))))


def _merge_kernel(x_ref, y_ref, yb_ref, gate_ref, wglu_ref, wa_ref, wb_ref, wo_ref, o_ref, y_scr):
    n_st, rt, d = x_ref.shape
    rows = n_st * rt
    _store_lane_blocks(y_scr, y_ref[...].astype(F32))
    y = jnp.concatenate([_strided_rows(y_scr, s, rt, n_st) for s in range(n_st)], axis=0)
    y = _gelu_tanh(y)
    ya = y * jax.nn.sigmoid(_dot(y.astype(BF16), wglu_ref[...]))
    gate = gate_ref[...].reshape(rows, 2 * d)
    ga, gb = gate[:, :D_MODEL], gate[:, D_MODEL:]
    merged = (jax.nn.sigmoid(ga) * _dot(ya.astype(BF16), wa_ref[...])
              + jax.nn.sigmoid(gb) * _dot(yb_ref[...].reshape(rows, ATT_WIDTH), wb_ref[...]))
    out = x_ref[...].reshape(rows, d) + _dot(merged.astype(BF16), wo_ref[...])
    o_ref[...] = out.reshape(n_st, rt, d)


def _merge(x, y, yb, gate, w_glu, w_a, w_b, w_out):
    n_st, l_seg, _ = x.shape
    rt = ROW_TILE
    ws = [w.astype(BF16) for w in (w_glu, w_a, w_b, w_out)]

    def streams(w):
        return pl.BlockSpec((n_st, rt, w), lambda i: (0, i, 0))

    return pl.pallas_call(
        _merge_kernel,
        grid=(l_seg // rt,),
        in_specs=[streams(D_MODEL), pl.BlockSpec((None, rt * n_st, D_MODEL), lambda i: (i, 0, 0)),
                  streams(ATT_WIDTH), streams(2 * D_MODEL)]
                 + [pl.BlockSpec(w.shape, lambda i: (0, 0)) for w in ws],
        out_specs=streams(D_MODEL),
        out_shape=jax.ShapeDtypeStruct((n_st, l_seg, D_MODEL), F32),
        scratch_shapes=[pltpu.VMEM((D_MODEL // LANES, rt * n_st, LANES), F32)],
        compiler_params=pltpu.CompilerParams(dimension_semantics=("arbitrary",), vmem_limit_bytes=VMEM_LIMIT),
        name="merge",
    )(x, y, yb, gate, *ws)


def _rms(x, g):
    return x * lax.rsqrt(jnp.mean(x * x, axis=-1, keepdims=True) + EPS) * g


def _ffn_kernel(x_ref, g2_ref, win_ref, wout_ref, gf_ref, o_ref, *, final_norm):
    x = x_ref[...]
    h = _rms(x, g2_ref[...]).astype(BF16)
    gu = _dot(h, win_ref[...])
    g, up = gu[:, :FFN_HIDDEN], gu[:, FFN_HIDDEN:]
    act = (g * jax.nn.sigmoid(g)) * up
    x = x + _dot(act.astype(BF16), wout_ref[...])
    o_ref[...] = _rms(x, gf_ref[...]) if final_norm else x


def _ffn(x1, g2, w_ffn_in, w_ffn_out, gf, final_norm, tile):
    T = x1.shape[0]
    win, wout = w_ffn_in.astype(BF16), w_ffn_out.astype(BF16)

    def row(w):
        return pl.BlockSpec((tile, w), lambda i: (i, 0))

    def full(a):
        return pl.BlockSpec(a.shape, lambda i: (0, 0))

    return pl.pallas_call(
        functools.partial(_ffn_kernel, final_norm=final_norm),
        grid=(T // tile,),
        in_specs=[row(D_MODEL), full(g2), full(win), full(wout), full(gf)],
        out_specs=row(D_MODEL),
        out_shape=jax.ShapeDtypeStruct((T, D_MODEL), F32),
        compiler_params=pltpu.CompilerParams(dimension_semantics=("arbitrary",), vmem_limit_bytes=VMEM_LIMIT),
        name="ffn",
    )(x1, g2, win, wout, gf)


def kernel(x, norm1_g, w_in, a_re, a_im, log_dt, b_re, b_im, c_re, c_im, d_skip, w_glu,
           w_branch_a, w_branch_b, w_out, norm2_g, w_ffn_in, w_ffn_out, norm_f_g):
    B, L, D = x.shape
    depth = norm1_g.shape[0]
    n_seg = S5_STREAMS // B
    assert D == D_MODEL and S5_STREAMS % B == 0 and L % 512 == 0 and L % (n_seg * ROW_TILE) == 0
    x = x.astype(F32)
    for i in range(depth):
        up, q, k, v, qi, ki2, wi, gate = _project(x, norm1_g[i][None, :].astype(F32), w_in[i])
        y = _s5_branch(up, B, a_re[i], a_im[i], log_dt[i], b_re[i], b_im[i], c_re[i], c_im[i], d_skip[i])
        yb = _dsa_branch(q, k, v, qi, ki2, wi, B, L)
        x1 = _merge(x.reshape(S5_STREAMS, L // n_seg, D), y, yb.reshape(S5_STREAMS, L // n_seg, ATT_WIDTH), gate,
                    w_glu[i], w_branch_a[i], w_branch_b[i], w_out[i])
        x = _ffn(x1.reshape(B * L, D), norm2_g[i][None, :].astype(F32), w_ffn_in[i], w_ffn_out[i],
                 norm_f_g[None, :].astype(F32), i == depth - 1, FFN_ROWS).reshape(B, L, D)
    return x
```

```python
import functools
import math

import jax
import jax.numpy as jnp
import numpy as np
from jax import lax
from jax.experimental import pallas as pl
from jax.experimental.pallas import tpu as pltpu

F32 = jnp.float32
BF16 = jnp.bfloat16
I32 = jnp.int32

D_MODEL = 1024
CHUNK = 64
EPS = 1e-6
NEG = -1e30

SSM_GROUP = 16
SSM_GROUPS = 64
SSM_STATE = 64
S5_CHUNK = 16
S5_STREAMS = 16
S5_LANE_GROUPS = 8
ROW_TILE = S5_CHUNK
FFN_ROWS = 256

N_HEADS = 8
HEAD_DIM = 128
N_KV_HEADS = 2
Q_PER_KV = N_HEADS // N_KV_HEADS
ATT_WIDTH = N_HEADS * HEAD_DIM
KV_WIDTH = N_KV_HEADS * HEAD_DIM
IDX_HEADS = 16
IDX_DIM = 64
TOPK_MAX = 256
ROPE_THETA = 500000.0
ATT_ROT = HEAD_DIM // 4
IDX_ROT = IDX_DIM // 4
FFN_HIDDEN = -(-8 * D_MODEL // (3 * 256)) * 256

SCORE_ROWS = 16
COUNT_ROWS = 32
SELECT_STEPS = 7
EXTRACT_RANKS = 4
ATT_ROWS = 32

LANES = 128
INT_MIN = -(2 ** 31)

VMEM_LIMIT = 56 * 1024 * 1024


def _key_of_float(val):
    bits = int(np.float32(val).view(np.int32))
    return bits ^ ((bits >> 31) & 0x7FFFFFFF)


KEY_HALF_NEG = _key_of_float(0.5 * NEG)


def _dot(a, b):
    return jnp.dot(a, b, preferred_element_type=F32)


def _dot_nt(a, b):
    return lax.dot_general(a, b, (((1,), (1,)), ((), ())), preferred_element_type=F32)


def _rope(x, c, s1, s2, half):
    n = x.shape[-1]
    return x * c + pltpu.roll(x, half, 1) * s1 + pltpu.roll(x, n - half, 1) * s2


def _store_lane_blocks(scr, val):
    for j in range(scr.shape[0]):
        scr[j] = val[:, j * LANES:(j + 1) * LANES]


def _strided_rows(scr, start, size, stride):
    return jnp.concatenate([scr[j, pl.ds(start, size, stride=stride), :] for j in range(scr.shape[0])], axis=1)


def _proj_kernel(x_ref, g1_ref, wu_ref, wq_ref, wk_ref, wv_ref, wqi_ref, wsm_ref, wg_ref,
                 ca_ref, sa1_ref, sa2_ref, ci_ref, si1_ref, si2_ref, cs_ref, ss1_ref, ss2_ref,
                 u_ref, q_ref, k_ref, v_ref, qi_ref, ki2_ref, wi_ref, gate_ref, u_scr, *, batch):
    n_st, rt, d = x_ref.shape
    rows = n_st * rt
    x = x_ref[...].reshape(rows, d)
    h = x * lax.rsqrt(jnp.mean(x * x, axis=-1, keepdims=True) + EPS) * g1_ref[...]
    hb = h.astype(BF16)

    def put(ref, val, sl=slice(None)):
        ref[:, :, sl] = val.reshape(n_st, rt, val.shape[-1])

    def table(ref):
        t = ref[...].reshape(rows // batch, LANES)
        return jnp.concatenate([t] * batch, axis=0)

    ca, sa1, sa2 = table(ca_ref), table(sa1_ref), table(sa2_ref)
    q = _dot(hb, wq_ref[...])
    scale = HEAD_DIM ** -0.5 * math.log2(math.e)
    for hd in range(N_HEADS):
        sl = slice(hd * HEAD_DIM, (hd + 1) * HEAD_DIM)
        put(q_ref, (_rope(q[:, sl], ca, sa1, sa2, ATT_ROT // 2) * scale).astype(BF16), sl)
    k = _dot(hb, wk_ref[...])
    for hd in range(N_KV_HEADS):
        sl = slice(hd * HEAD_DIM, (hd + 1) * HEAD_DIM)
        put(k_ref, _rope(k[:, sl], ca, sa1, sa2, ATT_ROT // 2).astype(BF16), sl)

    ci, si1, si2 = table(ci_ref), table(si1_ref), table(si2_ref)
    qi = _dot(hb, wqi_ref[...])
    for pr in range(IDX_HEADS * IDX_DIM // LANES):
        sl = slice(pr * LANES, (pr + 1) * LANES)
        put(qi_ref, _rope(qi[:, sl], ci, si1, si2, IDX_ROT // 2).astype(BF16), sl)

    sm = _dot(hb, wsm_ref[...])
    sm = _rope(sm, table(cs_ref), table(ss1_ref), table(ss2_ref), IDX_ROT // 2)
    lane = lax.broadcasted_iota(I32, sm.shape, 1)
    put(ki2_ref, jnp.where(lane < IDX_DIM, sm, pltpu.roll(sm, IDX_DIM, 1)).astype(BF16))
    put(wi_ref, sm * (IDX_DIM ** -0.5 * IDX_HEADS ** -0.5))

    _store_lane_blocks(u_scr, _dot(hb, wu_ref[...]))
    for t in range(rt):
        u_ref[t * n_st:(t + 1) * n_st, :] = _strided_rows(u_scr, t, n_st, rt).astype(BF16)

    v = _dot(hb, wv_ref[...]).astype(BF16)
    ones = jnp.ones((rows, HEAD_DIM), BF16)
    for hd in range(N_KV_HEADS):
        put(v_ref, v[:, hd * HEAD_DIM:(hd + 1) * HEAD_DIM], slice(2 * hd * HEAD_DIM, (2 * hd + 1) * HEAD_DIM))
        put(v_ref, ones, slice((2 * hd + 1) * HEAD_DIM, (2 * hd + 2) * HEAD_DIM))
    put(gate_ref, _dot(hb, wg_ref[...]))


def _rope_tables(L, rot, period, active_lanes):
    half = rot // 2
    pos = jnp.arange(L, dtype=jnp.int32)
    inv_freq = ROPE_THETA ** (-jnp.arange(half, dtype=F32) / half)
    ang = pos.astype(F32)[:, None] * inv_freq[None, :]
    cos, sin = jnp.cos(ang), jnp.sin(ang)
    lane = np.arange(LANES)
    within = lane % period
    fidx = np.where(within < rot, within % half, 0)
    is_x1 = (within < half) & (lane < active_lanes)
    is_x2 = (within >= half) & (within < rot) & (lane < active_lanes)
    rot_lane = is_x1 | is_x2
    c = jnp.where(rot_lane[None, :], cos[:, fidx], 1.0)
    s1 = jnp.where(is_x2[None, :], sin[:, fidx], 0.0)
    s2 = jnp.where(is_x1[None, :], -sin[:, fidx], 0.0)
    return c.astype(F32), s1.astype(F32), s2.astype(F32)


def _project(x, g1, w_in):
    B, L, _ = x.shape
    n_seg = S5_STREAMS // B
    l_seg = L // n_seg
    rt = ROW_TILE
    o = 0
    parts = []
    for s in (D_MODEL, ATT_WIDTH, KV_WIDTH, KV_WIDTH, IDX_HEADS * IDX_DIM, IDX_DIM, IDX_HEADS, D_MODEL, D_MODEL):
        parts.append(w_in[:, o:o + s])
        o += s
    wu, wq, wk, wv, wqi, wki, wwi, wga, wgb = parts
    wsm = jnp.concatenate([wki, wwi, jnp.zeros((D_MODEL, LANES - IDX_DIM - IDX_HEADS), w_in.dtype)], axis=1)
    wg = jnp.concatenate([wga, wgb], axis=1)
    ws = [w.astype(BF16) for w in (wu, wq, wk, wv, wqi, wsm, wg)]
    tabs = (_rope_tables(L, ATT_ROT, HEAD_DIM, LANES)
            + _rope_tables(L, IDX_ROT, IDX_DIM, LANES)
            + _rope_tables(L, IDX_ROT, IDX_DIM, IDX_DIM))
    tabs = [t.reshape(n_seg, l_seg, LANES) for t in tabs]

    def streams(w):
        return pl.BlockSpec((S5_STREAMS, rt, w), lambda i: (0, i, 0))

    def full(a):
        return pl.BlockSpec(a.shape, lambda i: (0, 0))

    tab_spec = pl.BlockSpec((n_seg, rt, LANES), lambda i: (0, i, 0))
    widths = (ATT_WIDTH,
              KV_WIDTH,
              2 * KV_WIDTH,
              IDX_HEADS * IDX_DIM,
              LANES,
              LANES,
              2 * D_MODEL)
    dtypes = (BF16, BF16, BF16, BF16, BF16, F32, F32)
    out_shape = ((jax.ShapeDtypeStruct((l_seg // rt, rt * S5_STREAMS, D_MODEL), BF16),)
                 + tuple(jax.ShapeDtypeStruct((S5_STREAMS, l_seg, w), dt) for w, dt in zip(widths, dtypes)))
    out_specs = ((pl.BlockSpec((None, rt * S5_STREAMS, D_MODEL), lambda i: (i, 0, 0)),)
                 + tuple(streams(w) for w in widths))
    return pl.pallas_call(
        functools.partial(_proj_kernel, batch=B),
        grid=(l_seg // rt,),
        in_specs=[streams(D_MODEL), full(g1)] + [full(w) for w in ws] + [tab_spec] * 9,
        out_specs=out_specs,
        out_shape=out_shape,
        scratch_shapes=[pltpu.VMEM((D_MODEL // LANES, rt * S5_STREAMS, LANES), F32)],
        compiler_params=pltpu.CompilerParams(dimension_semantics=("arbitrary",), vmem_limit_bytes=VMEM_LIMIT),
        name="proj",
    )(x.reshape(S5_STREAMS, l_seg, D_MODEL), g1, *ws, *tabs)


def _s5_group_weights(ldt, ar, ai, btr, bti, cre, cim):
    lane = lax.broadcasted_iota(I32, (1, LANES), 1)
    lo = lane < SSM_STATE
    dt = jnp.exp(ldt)
    rho, th = ar * dt, ai * dt
    npow = S5_CHUNK + 1
    nn = lax.broadcasted_iota(I32, (24, LANES), 0).astype(F32)
    mag = jnp.exp(nn * rho)
    lc = mag * jnp.cos(nn * th)
    ls = mag * jnp.sin(nn * th)
    lam_pk = jnp.where(lo, lc, ls)
    lam_sw = jnp.where(lo, -ls, lc)
    num_re, num_im = lc[1:2] - 1.0, ls[1:2]
    den = ar * ar + ai * ai
    f_re = (num_re * ar + num_im * ai) / den
    f_im = (num_im * ar - num_re * ai) / den
    g_re = btr * f_re - bti * f_im
    g_im = btr * f_im + bti * f_re
    g_neg = jnp.where(lo, g_re, -g_im)
    w_pk = [cre * lam_pk[n:n + 1] + cim * lam_sw[n:n + 1] for n in range(npow)]
    w_state = [g_re * lam_pk[S5_CHUNK - 1 - ti:S5_CHUNK - ti] + g_im * lam_sw[S5_CHUNK - 1 - ti:S5_CHUNK - ti]
               for ti in range(S5_CHUNK)]
    a_p = jnp.where(lo, lam_pk[S5_CHUNK:S5_CHUNK + 1], lam_sw[S5_CHUNK:S5_CHUNK + 1])
    a_q = jnp.where(lo, lam_sw[S5_CHUNK:S5_CHUNK + 1], lam_pk[S5_CHUNK:S5_CHUNK + 1])
    return g_neg, w_pk, w_state, a_p, a_q


def _place(block, g):
    z = jnp.zeros_like(block)
    return jnp.concatenate([block if j == g else z for j in range(S5_LANE_GROUPS)], axis=1)


def _swap_halves(x):
    return jnp.concatenate([pltpu.roll(x[:, g * LANES:(g + 1) * LANES], SSM_STATE, 1)
                            for g in range(x.shape[1] // LANES)], axis=1)


def _s5_kernel(up_ref, are_ref, aim_ref, ldt_ref, btr_ref, bti_ref, cre_ref, cim_ref, dsk_ref,
               y_ref, s_ref, ssw_ref, hprev_ref, kbd_ref, wst_ref, wot_ref, *, n_steps, n_seg):
    ng = S5_LANE_GROUPS
    lane = lax.broadcasted_iota(I32, (1, LANES), 1)
    lo = lane < SSM_STATE
    weights = [_s5_group_weights(ldt_ref[g], are_ref[g], aim_ref[g], btr_ref[g], bti_ref[g], cre_ref[g], cim_ref[g])
               for g in range(ng)]
    gneg_blk = jnp.concatenate([_place(weights[g][0], g) for g in range(ng)], axis=0)
    for tau in range(S5_CHUNK):
        wt = jnp.concatenate([_place(weights[g][1][tau], g) for g in range(ng)], axis=0)
        kbd_ref[:, tau * LANES:(tau + 1) * LANES] = lax.dot_general(
            gneg_blk, wt, (((1,), (1,)), ((), ())), preferred_element_type=F32, precision=lax.Precision.HIGHEST)
    for ti in range(S5_CHUNK):
        wst_ref[ti * LANES:(ti + 1) * LANES, :] = jnp.concatenate(
            [_place(weights[g][2][ti], g) for g in range(ng)], axis=0).astype(BF16)
        wot_ref[ti * LANES:(ti + 1) * LANES, :] = jnp.concatenate(
            [_place(jnp.where(lo, weights[g][1][ti + 1], -weights[g][1][ti + 1]), g) for g in range(ng)],
            axis=0).astype(BF16)
    a_p = jnp.concatenate([weights[g][3] for g in range(ng)], axis=1)
    a_q = jnp.concatenate([weights[g][4] for g in range(ng)], axis=1)

    rows = n_steps * S5_STREAMS

    def chunk_rows(ti):
        return up_ref[:, ti * S5_STREAMS:(ti + 1) * S5_STREAMS, :].reshape(rows, LANES)

    lhs = jnp.concatenate([chunk_rows(ti) for ti in range(S5_CHUNK)], axis=1)
    s_all = _dot(lhs, wst_ref[...])
    s_ref[...] = s_all
    ssw_ref[...] = _swap_halves(s_all)

    def cmul(x, p, q):
        return x * p + _swap_halves(x) * q

    def step(h, hs, i):
        rows_i = pl.ds(pl.multiple_of(i * S5_STREAMS, S5_STREAMS), S5_STREAMS)
        return h * a_p + hs * a_q + s_ref[rows_i, :], hs * a_p - h * a_q + ssw_ref[rows_i, :]

    zero = jnp.zeros((S5_STREAMS, ng * LANES), F32)
    z, _ = lax.fori_loop(0, n_steps, lambda i, c: step(c[0], c[1], i), (zero, zero))

    lo_all = jnp.concatenate([lo] * ng, axis=1)

    def factors(zpk):
        sw = _swap_halves(zpk)
        return jnp.where(lo_all, zpk, sw), jnp.where(lo_all, -sw, zpk)

    base, seg, n = jnp.where(lo_all, a_p, a_q), None, n_steps
    while n:
        if n & 1:
            seg = base if seg is None else cmul(seg, *factors(base))
        n >>= 1
        if n:
            base = cmul(base, *factors(base))
    seg_p, seg_q = factors(seg)

    row_id = lax.broadcasted_iota(I32, (S5_STREAMS, ng * LANES), 0)
    init = zero
    prev = None
    for s in range(S5_STREAMS):
        if s % n_seg == 0:
            cur = jnp.zeros((1, ng * LANES), F32)
        else:
            cur = cmul(prev, seg_p, seg_q) + z[s - 1:s]
            init = jnp.where(row_id == s, cur, init)
        prev = cur

    def pass2(i, c):
        hprev_ref[pl.ds(pl.multiple_of(i * S5_STREAMS, S5_STREAMS), S5_STREAMS), :] = c[0]
        return step(c[0], c[1], i)

    lax.fori_loop(0, n_steps, pass2, (init, _swap_halves(init)))

    hp = hprev_ref[...].astype(BF16)
    dsk = dsk_ref[...]
    zblk = jnp.zeros((LANES, LANES), F32)
    for tp in range(S5_CHUNK // 2):
        kdim = (2 * tp + 2) * LANES
        cols = []
        for t2 in (2 * tp, 2 * tp + 1):
            cols.append(jnp.concatenate(
                [kbd_ref[:, (t2 - ti) * LANES:(t2 - ti + 1) * LANES] if ti <= t2 else zblk
                 for ti in range(2 * tp + 2)], axis=0))
        slab = jnp.concatenate(cols, axis=1).astype(BF16)
        y = _dot(lhs[:, :kdim], slab) + _dot_nt(hp, wot_ref[2 * tp * LANES:(2 * tp + 2) * LANES, :])
        for half in range(2):
            t2 = 2 * tp + half
            u_t = up_ref[:, t2 * S5_STREAMS:(t2 + 1) * S5_STREAMS, :].astype(F32)
            y_t = y[:, half * LANES:(half + 1) * LANES].reshape(n_steps, S5_STREAMS, LANES) + u_t * dsk
            y_ref[:, t2 * S5_STREAMS:(t2 + 1) * S5_STREAMS, :] = y_t.astype(y_ref.dtype)


def _s5_branch(up, B, a_re, a_im, log_dt, b_re, b_im, c_re, c_im, d_skip):
    G, C = SSM_GROUPS, SSM_GROUP
    n_steps = up.shape[0]
    n_seg = S5_STREAMS // B
    rows = n_steps * S5_STREAMS
    ng = S5_LANE_GROUPS

    def dup(a):
        return jnp.concatenate([a, a], axis=-1).astype(F32)

    are2 = dup(a_re)[:, None, :]
    aim2 = dup(a_im)[:, None, :]
    ldt = log_dt.astype(F32)[:, None, None]
    btr2 = dup(jnp.swapaxes(b_re, 1, 2))
    bti2 = dup(jnp.swapaxes(b_im, 1, 2))
    cre2 = dup(c_re)
    cim2 = dup(c_im)
    dsk = d_skip.astype(F32).reshape(G // ng, 1, ng * C)

    def per_block(shape):
        return pl.BlockSpec((ng,) + shape, lambda i: (i, 0, 0))

    act = pl.BlockSpec((n_steps, S5_CHUNK * S5_STREAMS, LANES), lambda i: (0, 0, i))
    return pl.pallas_call(
        functools.partial(_s5_kernel, n_steps=n_steps, n_seg=n_seg),
        grid=(G // ng,),
        in_specs=[act, per_block((1, LANES)), per_block((1, LANES)), per_block((1, 1)),
                  per_block((C, LANES)), per_block((C, LANES)), per_block((C, LANES)), per_block((C, LANES)),
                  pl.BlockSpec((None, 1, LANES), lambda i: (i, 0, 0))],
        out_specs=act,
        out_shape=jax.ShapeDtypeStruct(up.shape, BF16),
        scratch_shapes=[pltpu.VMEM((rows, ng * LANES), F32),
                        pltpu.VMEM((rows, ng * LANES), F32),
                        pltpu.VMEM((rows, ng * LANES), F32),
                        pltpu.VMEM((LANES, S5_CHUNK * LANES), F32),
                        pltpu.VMEM((S5_CHUNK * LANES, ng * LANES), BF16),
                        pltpu.VMEM((S5_CHUNK * LANES, ng * LANES), BF16)],
        compiler_params=pltpu.CompilerParams(dimension_semantics=("arbitrary",), vmem_limit_bytes=VMEM_LIMIT),
        name="s5",
    )(up, are2, aim2, ldt, btr2, bti2, cre2, cim2, dsk)


def _float_to_key(x):
    bits = lax.bitcast_convert_type(x, I32)
    return bits ^ (lax.shift_right_arithmetic(bits, 31) & 0x7FFFFFFF)


def _key_to_float(key):
    return lax.bitcast_convert_type(key ^ (lax.shift_right_arithmetic(key, 31) & 0x7FFFFFFF), F32)


def _probit(p):
    t = jnp.sqrt(-2.0 * jnp.log(jnp.minimum(p, 1.0 - p)))
    z = t - ((0.010328 * t + 0.802853) * t + 2.515517) / (((0.001308 * t + 0.189269) * t + 1.432788) * t + 1.0)
    return jnp.where(p < 0.5, -z, z)


def _count_ge(sc_ref, trial, n_tiles, ktile):
    def count_tile(t, acc):
        off = pl.multiple_of(t * ktile, ktile)
        for j in range(ktile // COUNT_ROWS):
            rows = sc_ref[pl.ds(off + j * COUNT_ROWS, COUNT_ROWS), :]
            acc = jnp.where(rows >= trial, acc + 1, acc)
        return acc

    acc = lax.fori_loop(0, n_tiles, count_tile, jnp.zeros((COUNT_ROWS, trial.shape[1]), I32))
    return jnp.sum(acc, axis=0, keepdims=True).astype(F32)


def _select_threshold(sc_ref, s_min, s_max, n_allowed, n_tiles, k_sel, ktile):
    kf = float(k_sel)
    n_eff = jnp.maximum(n_allowed, kf + 1.0)
    z_t = _probit(1.0 - (kf - 0.5) / n_eff)

    def z_of(c):
        return _probit(jnp.clip(1.0 - c / n_eff, 0.5 / n_eff, 1.0 - 0.5 / n_eff))

    def body(st):
        it, t_lo, t_hi, c_lo, c_hi, z_lo, z_hi, w_lo, w_hi, side, done, thr = st
        g_lo = (z_lo - z_t) * w_lo
        g_hi = (z_hi - z_t) * w_hi
        t = t_lo + (t_hi - t_lo) * jnp.clip(g_lo / (g_lo - g_hi), 0.0, 1.0)
        k_lo, k_hi, k_t = _float_to_key(t_lo), _float_to_key(t_hi), _float_to_key(t)
        mid = lax.shift_right_arithmetic(k_lo, 1) + lax.shift_right_arithmetic(k_hi, 1) + (k_lo & k_hi & 1)
        use_mid = (k_t <= k_lo) | (k_t >= k_hi) | (c_lo - c_hi <= 4.0) | (it >= 20)
        t = _key_to_float(jnp.where(use_mid, mid, k_t))
        c = _count_ge(sc_ref, t, n_tiles, ktile)
        active = done == 0
        hit = active & (c == kf)
        new_lo = active & (c >= kf)
        new_hi = active & (c < kf)
        z_c = z_of(c)
        w_hi = jnp.where(new_lo, jnp.where(side > 0, 0.5 * w_hi, w_hi), 1.0)
        w_lo = jnp.where(new_hi, jnp.where(side < 0, 0.5 * w_lo, w_lo), 1.0)
        side = jnp.where(new_lo, 1.0, jnp.where(new_hi, -1.0, side))
        t_lo = jnp.where(new_lo, t, t_lo)
        c_lo = jnp.where(new_lo, c, c_lo)
        z_lo = jnp.where(new_lo, z_c, z_lo)
        t_hi = jnp.where(new_hi, t, t_hi)
        c_hi = jnp.where(new_hi, c, c_hi)
        z_hi = jnp.where(new_hi, z_c, z_hi)
        adjacent = active & (_float_to_key(t_hi) - 1 <= _float_to_key(t_lo))
        thr = jnp.where(hit, t, jnp.where(adjacent, t_lo, thr))
        done = jnp.where(hit | adjacent, 1, done)
        return it + 1, t_lo, t_hi, c_lo, c_hi, z_lo, z_hi, w_lo, w_hi, side, done, thr

    def cond(st):
        return jnp.logical_and(st[0] < 64, jnp.min(st[10]) < 1)

    few = n_allowed <= kf
    ones = jnp.ones(n_allowed.shape, F32)
    zeros = jnp.zeros(n_allowed.shape, F32)
    init = (jnp.int32(0), s_min, _key_to_float(_float_to_key(s_max) + 1), n_allowed, zeros,
            z_of(n_allowed), z_of(zeros), ones, ones, zeros,
            few.astype(I32), jnp.where(few, -jnp.inf, s_min))
    st = lax.fori_loop(0, SELECT_STEPS, lambda _, s: body(s), init)
    it, t_lo, t_hi, c_lo, c_hi, z_lo, z_hi, w_lo, w_hi, side, done, thr = st
    rank_hi = kf - c_hi
    rank_lo = c_lo - kf + 1.0
    use_hi = rank_hi <= EXTRACT_RANKS
    resolved = (done == 0) & (use_hi | (rank_lo <= EXTRACT_RANKS))
    top = _top_ranked(sc_ref, use_hi, t_lo, t_hi, n_tiles, ktile)
    rank = jnp.where(use_hi, rank_hi, rank_lo)
    picked = top[EXTRACT_RANKS - 1]
    for j in range(EXTRACT_RANKS - 1, 0, -1):
        picked = jnp.where(rank <= j, top[j - 1], picked)
    thr = jnp.where(resolved, jnp.where(use_hi, picked, -picked), thr)
    done = jnp.where(resolved, 1, done)
    st = (it, t_lo, t_hi, c_lo, c_hi, z_lo, z_hi, w_lo, w_hi, side, done, thr)
    return lax.while_loop(cond, body, st)[11]


def _insert_sorted(tops, x):
    out = []
    for a in tops:
        out.append(jnp.maximum(a, x))
        x = jnp.minimum(a, x)
    return out


def _top_ranked(sc_ref, use_hi, t_lo, t_hi, n_tiles, ktile):
    width = t_lo.shape[1]
    sign = jnp.where(use_hi, 1.0, -1.0)
    bound = jnp.where(use_hi, t_hi, _key_to_float(_float_to_key(-t_lo) + 1))

    def candidates(x):
        y = x * sign
        return jnp.where(y < bound, y, -jnp.inf)

    def sweep_tile(t, tops):
        off = pl.multiple_of(t * ktile, ktile)
        tops = list(tops)
        for j in range(ktile // 8):
            tops = _insert_sorted(tops, candidates(sc_ref[pl.ds(off + j * 8, 8), :]))
        return tuple(tops)

    init = tuple(jnp.full((8, width), -jnp.inf, F32) for _ in range(EXTRACT_RANKS))
    tops8 = lax.fori_loop(0, n_tiles, sweep_tile, init)
    tops = [jnp.full((1, width), -jnp.inf, F32) for _ in range(EXTRACT_RANKS)]
    for a in tops8:
        for r in range(8):
            tops = _insert_sorted(tops, a[r:r + 1, :])
    return tops


def _dsa_kernel(qi_ref, wi_ref, q_ref, ki2_ref, k_ref, v_ref, o_ref,
                sc_ref, qm_ref, qg_ref, m_ref, acc_ref, rel0_ref, rel1_ref, lg0_ref, lg1_ref, *, k_sel, qblk, ktile):
    assert qblk == 2 * LANES
    hq = LANES
    qb = pl.program_id(1)
    n_tiles = lax.div(qb * qblk + qblk + ktile - 1, ktile)
    lane = lax.broadcasted_iota(I32, (hq, LANES), 1)
    n_lt = ktile // LANES
    halves = (0, 1)

    eye = (lax.broadcasted_iota(I32, (hq, LANES), 0) == lane).astype(BF16)
    for h in halves:
        rows_h = slice(h * hq, (h + 1) * hq)
        for hd in range(IDX_HEADS):
            pair = qi_ref[rows_h, (hd // 2) * LANES:(hd // 2 + 1) * LANES]
            msk = (lane < IDX_DIM) if hd % 2 == 0 else (lane >= IDX_DIM)
            qm_ref[h, hd * hq:(hd + 1) * hq, :] = jnp.where(msk, pair, jnp.zeros_like(pair))
        for hd in range(N_HEADS):
            g, r = divmod(hd, Q_PER_KV)
            qg_ref[h, g, r * hq:(r + 1) * hq, 0:HEAD_DIM] = q_ref[rows_h, hd * HEAD_DIM:(hd + 1) * HEAD_DIM]
            qg_ref[h, g, r * hq:(r + 1) * hq, HEAD_DIM:2 * HEAD_DIM] = eye

    w_t = [wi_ref[h * hq:(h + 1) * hq, :].T for h in halves]
    q_chunk = lax.shift_right_logical(qb * qblk + lax.broadcasted_iota(I32, (1, qblk), 1), 6)
    n_allowed = ((q_chunk + 1) * CHUNK).astype(F32)

    rel_refs = (rel0_ref, rel1_ref)
    lg_refs = (lg0_ref, lg1_ref)

    def tile_offset(t):
        return pl.multiple_of(t * ktile, ktile)

    def issue_logits(h, t):
        rel_refs[h][...] = _dot_nt(ki2_ref[pl.ds(tile_offset(t), ktile), :], qm_ref[h])

    def score_half(h, t, carry):
        mx, mn = carry
        off = pl.multiple_of(t * ktile, ktile)
        q_chunk_h = q_chunk[:, h * hq:(h + 1) * hq]
        for c in range(ktile // SCORE_ROWS):
            r0 = c * SCORE_ROWS
            sc = jnp.zeros((SCORE_ROWS, LANES), F32)
            for hd in range(IDX_HEADS):
                rel = rel_refs[h][r0:r0 + SCORE_ROWS, hd * hq:(hd + 1) * hq]
                sc = sc + jnp.maximum(rel, 0.0) * w_t[h][IDX_DIM + hd:IDX_DIM + hd + 1, :]
            k_chunk = lax.shift_right_logical(off + r0 + lax.broadcasted_iota(I32, (SCORE_ROWS, 1), 0), 6)
            ok = k_chunk <= q_chunk_h
            sc_ref[pl.ds(off + r0, SCORE_ROWS), h * hq:(h + 1) * hq] = jnp.where(ok, sc, NEG)
            mx = jnp.maximum(mx, jnp.where(ok, sc, -jnp.inf))
            mn = jnp.minimum(mn, jnp.where(ok, sc, jnp.inf))
        return mx, mn

    def score_tile(t, carry, last=False):
        issue_logits(1, t)
        c0 = score_half(0, t, carry[0])
        if not last:
            issue_logits(0, t + 1)
        return c0, score_half(1, t, carry[1])

    issue_logits(0, 0)
    init = (jnp.full((SCORE_ROWS, LANES), -jnp.inf, F32), jnp.full((SCORE_ROWS, LANES), jnp.inf, F32))
    n_main = n_tiles - 1
    n_main_pairs = lax.shift_right_logical(n_main, 1)
    main_odd = (n_main & 1) == 1
    carry = lax.fori_loop(0, n_main_pairs, lambda p, c: score_tile(2 * p + 1, score_tile(2 * p, c)), (init, init))
    carry = lax.cond(main_odd, lambda: score_tile(n_main - 1, carry), lambda: carry)
    (mx0, mn0), (mx1, mn1) = score_tile(n_tiles - 1, carry, last=True)
    mx = jnp.concatenate([mx0, mx1], axis=1)
    mn = jnp.concatenate([mn0, mn1], axis=1)
    s_max = jnp.max(mx, axis=0, keepdims=True)
    s_min = jnp.min(mn, axis=0, keepdims=True)

    thr = _select_threshold(sc_ref, s_min, s_max, n_allowed, n_tiles, k_sel, ktile)
    thr = jnp.maximum(thr, _key_to_float(jnp.full((1, qblk), KEY_HALF_NEG + 1, I32)))

    m_ref[...] = jnp.full(m_ref.shape, -jnp.inf, F32)
    acc_ref[...] = jnp.zeros(acc_ref.shape, F32)
    g_rows = Q_PER_KV * hq

    def selection_bias(h, t):
        keep = sc_ref[pl.ds(tile_offset(t), ktile), h * hq:(h + 1) * hq] >= thr[:, h * hq:(h + 1) * hq]
        return jnp.where(keep, 0.0, NEG).astype(BF16)

    def issue_qk(h, t, g, bias_t):
        kb = jnp.concatenate([k_ref[pl.ds(tile_offset(t), ktile), g * HEAD_DIM:(g + 1) * HEAD_DIM], bias_t], axis=1)
        lg_refs[h][g] = _dot_nt(qg_ref[h, g], kb)

    def attend_half(h, t, issue_next=True):
        off = pl.multiple_of(t * ktile, ktile)
        nh, nt = (1, t) if h == 0 else (0, t + 1)
        bias_next = selection_bias(nh, nt) if issue_next else None
        for g in range(N_KV_HEADS):
            if issue_next:
                issue_qk(nh, nt, g, bias_next)
            ps, alphas = [], []
            for c in range(g_rows // ATT_ROWS):
                r0 = c * ATT_ROWS
                s = lg_refs[h][g, r0:r0 + ATT_ROWS, :]
                m_cur = s[:, 0:LANES]
                for j in range(1, n_lt):
                    m_cur = jnp.maximum(m_cur, s[:, j * LANES:(j + 1) * LANES])
                m_old = m_ref[h, g, r0:r0 + ATT_ROWS, :]
                m_new = jnp.maximum(m_old, jnp.max(m_cur, axis=1, keepdims=True))
                alphas.append(jnp.exp2(m_old - m_new))
                m_ref[h, g, r0:r0 + ATT_ROWS, :] = m_new
                ps.append(jnp.exp2(s - jnp.concatenate([m_new] * n_lt, axis=1)).astype(BF16))
            alpha = jnp.concatenate(alphas, axis=0)
            pv = _dot(jnp.concatenate(ps, axis=0), v_ref[pl.ds(off, ktile), 2 * g * HEAD_DIM:(2 * g + 2) * HEAD_DIM])
            acc_ref[h, g] = acc_ref[h, g] * jnp.concatenate([alpha, alpha], axis=1) + pv

    def attend_tile(t, carry):
        attend_half(0, t)
        attend_half(1, t)
        return carry

    bias_0 = selection_bias(0, 0)
    for g in range(N_KV_HEADS):
        issue_qk(0, 0, g, bias_0)
    lax.fori_loop(0, n_main_pairs, lambda p, c: attend_tile(2 * p + 1, attend_tile(2 * p, c)), 0)

    @pl.when(main_odd)
    def _():
        attend_tile(n_main - 1, 0)

    attend_half(0, n_tiles - 1)
    attend_half(1, n_tiles - 1, issue_next=False)

    for h in halves:
        for hd in range(N_HEADS):
            g, r = divmod(hd, Q_PER_KV)
            a = acc_ref[h, g, r * hq:(r + 1) * hq, :]
            o_ref[h * hq:(h + 1) * hq, hd * HEAD_DIM:(hd + 1) * HEAD_DIM] = (
                a[:, :HEAD_DIM] / a[:, HEAD_DIM:]).astype(o_ref.dtype)


def _dsa_branch(q, k, v, qi, ki2, wi, B, L, qblk=2 * LANES, ktile=512):
    k_sel = min(TOPK_MAX, L // 4)
    nq = L // qblk
    g_rows = Q_PER_KV * LANES
    assert L % ktile == 0 and ktile % qblk == 0

    def qrow(w):
        return pl.BlockSpec((None, qblk, w), lambda b, i: (b, i, 0))

    def whole(w):
        return pl.BlockSpec((None, L, w), lambda b, i: (b, 0, 0), pipeline_mode=pl.Buffered(1))

    r3 = lambda a: a.reshape(B, L, a.shape[-1])
    out = pl.pallas_call(
        functools.partial(_dsa_kernel, k_sel=k_sel, qblk=qblk, ktile=ktile),
        grid=(B, nq),
        in_specs=[qrow(IDX_HEADS * IDX_DIM), qrow(LANES), qrow(ATT_WIDTH), whole(LANES), whole(KV_WIDTH),
                  whole(2 * KV_WIDTH)],
        out_specs=qrow(ATT_WIDTH),
        out_shape=jax.ShapeDtypeStruct((B, L, ATT_WIDTH), BF16),
        scratch_shapes=[pltpu.VMEM((L, qblk), F32),
                        pltpu.VMEM((2, IDX_HEADS * LANES, LANES), BF16),
                        pltpu.VMEM((2, N_KV_HEADS, g_rows, 2 * HEAD_DIM), BF16),
                        pltpu.VMEM((2, N_KV_HEADS, g_rows, LANES), F32),
                        pltpu.VMEM((2, N_KV_HEADS, g_rows, 2 * HEAD_DIM), F32),
                        pltpu.VMEM((ktile, IDX_HEADS * LANES), F32),
                        pltpu.VMEM((ktile, IDX_HEADS * LANES), F32),
                        pltpu.VMEM((N_KV_HEADS, g_rows, ktile), F32),
                        pltpu.VMEM((N_KV_HEADS, g_rows, ktile), F32)],
        compiler_params=pltpu.CompilerParams(dimension_semantics=("arbitrary", "arbitrary"),
                                             vmem_limit_bytes=VMEM_LIMIT),
        name="dsa",
    )(r3(qi), r3(wi), r3(q), r3(ki2), r3(k), r3(v))
    return out.reshape(B * L, ATT_WIDTH)


def _gelu_tanh(x):
    return 0.5 * x * (1.0 + jnp.tanh(math.sqrt(2.0 / math.pi) * (x + 0.044715 * (x * x * x))))


def _merge_kernel(x_ref, y_ref, yb_ref, gate_ref, wglu_ref, wa_ref, wb_ref, wo_ref, o_ref, y_scr):
    n_st, rt, d = x_ref.shape
    rows = n_st * rt
    gate = gate_ref[...].reshape(rows, 2 * d)
    ga, gb = gate[:, :D_MODEL], gate[:, D_MODEL:]
    branch_b = jax.nn.sigmoid(gb) * _dot(yb_ref[...].reshape(rows, ATT_WIDTH), wb_ref[...])
    _store_lane_blocks(y_scr, y_ref[...].astype(F32))
    y = jnp.concatenate([_strided_rows(y_scr, s, rt, n_st) for s in range(n_st)], axis=0)
    y = _gelu_tanh(y)
    ya = y * jax.nn.sigmoid(_dot(y.astype(BF16), wglu_ref[...]))
    merged = jax.nn.sigmoid(ga) * _dot(ya.astype(BF16), wa_ref[...]) + branch_b
    out = x_ref[...].reshape(rows, d) + _dot(merged.astype(BF16), wo_ref[...])
    o_ref[...] = out.reshape(n_st, rt, d)


def _merge(x, y, yb, gate, w_glu, w_a, w_b, w_out):
    n_st, l_seg, _ = x.shape
    rt = ROW_TILE
    ws = [w.astype(BF16) for w in (w_glu, w_a, w_b, w_out)]

    def streams(w):
        return pl.BlockSpec((n_st, rt, w), lambda i: (0, i, 0))

    return pl.pallas_call(
        _merge_kernel,
        grid=(l_seg // rt,),
        in_specs=[streams(D_MODEL), pl.BlockSpec((None, rt * n_st, D_MODEL), lambda i: (i, 0, 0)),
                  streams(ATT_WIDTH), streams(2 * D_MODEL)]
                 + [pl.BlockSpec(w.shape, lambda i: (0, 0)) for w in ws],
        out_specs=streams(D_MODEL),
        out_shape=jax.ShapeDtypeStruct((n_st, l_seg, D_MODEL), F32),
        scratch_shapes=[pltpu.VMEM((D_MODEL // LANES, rt * n_st, LANES), F32)],
        compiler_params=pltpu.CompilerParams(dimension_semantics=("arbitrary",), vmem_limit_bytes=VMEM_LIMIT),
        name="merge",
    )(x, y, yb, gate, *ws)


def _rms(x, g):
    return x * lax.rsqrt(jnp.mean(x * x, axis=-1, keepdims=True) + EPS) * g


def _ffn_kernel(x_ref, g2_ref, win_ref, wout_ref, gf_ref, o_ref, *, final_norm):
    x = x_ref[...]
    h = _rms(x, g2_ref[...]).astype(BF16)
    gu = _dot(h, win_ref[...])
    g, up = gu[:, :FFN_HIDDEN], gu[:, FFN_HIDDEN:]
    act = (g * jax.nn.sigmoid(g)) * up
    x = x + _dot(act.astype(BF16), wout_ref[...])
    o_ref[...] = _rms(x, gf_ref[...]) if final_norm else x


def _ffn(x1, g2, w_ffn_in, w_ffn_out, gf, final_norm, tile):
    T = x1.shape[0]
    win, wout = w_ffn_in.astype(BF16), w_ffn_out.astype(BF16)

    def row(w):
        return pl.BlockSpec((tile, w), lambda i: (i, 0))

    def full(a):
        return pl.BlockSpec(a.shape, lambda i: (0, 0))

    return pl.pallas_call(
        functools.partial(_ffn_kernel, final_norm=final_norm),
        grid=(T // tile,),
        in_specs=[row(D_MODEL), full(g2), full(win), full(wout), full(gf)],
        out_specs=row(D_MODEL),
        out_shape=jax.ShapeDtypeStruct((T, D_MODEL), F32),
        compiler_params=pltpu.CompilerParams(dimension_semantics=("arbitrary",), vmem_limit_bytes=VMEM_LIMIT),
        name="ffn",
    )(x1, g2, win, wout, gf)


def kernel(x, norm1_g, w_in, a_re, a_im, log_dt, b_re, b_im, c_re, c_im, d_skip, w_glu,
           w_branch_a, w_branch_b, w_out, norm2_g, w_ffn_in, w_ffn_out, norm_f_g):
    B, L, D = x.shape
    depth = norm1_g.shape[0]
    n_seg = S5_STREAMS // B
    assert D == D_MODEL and S5_STREAMS % B == 0 and L % 512 == 0 and L % (n_seg * ROW_TILE) == 0
    x = x.astype(F32)
    for i in range(depth):
        up, q, k, v, qi, ki2, wi, gate = _project(x, norm1_g[i][None, :].astype(F32), w_in[i])
        y = _s5_branch(up, B, a_re[i], a_im[i], log_dt[i], b_re[i], b_im[i], c_re[i], c_im[i], d_skip[i])
        yb = _dsa_branch(q, k, v, qi, ki2, wi, B, L)
        x1 = _merge(x.reshape(S5_STREAMS, L // n_seg, D), y, yb.reshape(S5_STREAMS, L // n_seg, ATT_WIDTH), gate,
                    w_glu[i], w_branch_a[i], w_branch_b[i], w_out[i])
        x = _ffn(x1.reshape(B * L, D), norm2_g[i][None, :].astype(F32), w_ffn_in[i], w_ffn_out[i],
                 norm_f_g[None, :].astype(F32), i == depth - 1, FFN_ROWS).reshape(B, L, D)
    return x
```

```python
import functools
import math

import jax
import jax.numpy as jnp
import numpy as np
from jax import lax
from jax.experimental import pallas as pl
from jax.experimental.pallas import tpu as pltpu

F32 = jnp.float32
BF16 = jnp.bfloat16
I32 = jnp.int32

D_MODEL = 1024
CHUNK = 64
EPS = 1e-6
NEG = -1e30

SSM_GROUP = 16
SSM_GROUPS = 64
SSM_STATE = 64
S5_CHUNK = 16
S5_STREAMS = 16
S5_LANE_GROUPS = 8
ROW_TILE = S5_CHUNK
FFN_ROWS = 256

N_HEADS = 8
HEAD_DIM = 128
N_KV_HEADS = 2
Q_PER_KV = N_HEADS // N_KV_HEADS
ATT_WIDTH = N_HEADS * HEAD_DIM
KV_WIDTH = N_KV_HEADS * HEAD_DIM
IDX_HEADS = 16
IDX_DIM = 64
TOPK_MAX = 256
ROPE_THETA = 500000.0
ATT_ROT = HEAD_DIM // 4
IDX_ROT = IDX_DIM // 4
FFN_HIDDEN = -(-8 * D_MODEL // (3 * 256)) * 256

SCORE_ROWS = 16
COUNT_ROWS = 32
SELECT_STEPS = 7
EXTRACT_RANKS = 4
ATT_ROWS = 32

LANES = 128
INT_MIN = -(2 ** 31)

VMEM_LIMIT = 56 * 1024 * 1024


def _key_of_float(val):
    bits = int(np.float32(val).view(np.int32))
    return bits ^ ((bits >> 31) & 0x7FFFFFFF)


KEY_HALF_NEG = _key_of_float(0.5 * NEG)


def _dot(a, b):
    return jnp.dot(a, b, preferred_element_type=F32)


def _dot_nt(a, b):
    return lax.dot_general(a, b, (((1,), (1,)), ((), ())), preferred_element_type=F32)


def _rope(x, c, s1, s2, half):
    n = x.shape[-1]
    return x * c + pltpu.roll(x, half, 1) * s1 + pltpu.roll(x, n - half, 1) * s2


def _store_lane_blocks(scr, val):
    for j in range(scr.shape[0]):
        scr[j] = val[:, j * LANES:(j + 1) * LANES]


def _strided_rows(scr, start, size, stride):
    return jnp.concatenate([scr[j, pl.ds(start, size, stride=stride), :] for j in range(scr.shape[0])], axis=1)


def _proj_kernel(x_ref, g1_ref, wu_ref, wq_ref, wk_ref, wv_ref, wqi_ref, wsm_ref, wg_ref,
                 ca_ref, sa1_ref, sa2_ref, ci_ref, si1_ref, si2_ref, cs_ref, ss1_ref, ss2_ref,
                 u_ref, q_ref, k_ref, v_ref, qi_ref, ki2_ref, wi_ref, gate_ref, u_scr, *, batch):
    n_st, rt, d = x_ref.shape
    rows = n_st * rt
    x = x_ref[...].reshape(rows, d)
    h = x * lax.rsqrt(jnp.mean(x * x, axis=-1, keepdims=True) + EPS) * g1_ref[...]
    hb = h.astype(BF16)

    def put(ref, val, sl=slice(None)):
        ref[:, :, sl] = val.reshape(n_st, rt, val.shape[-1])

    def table(ref):
        t = ref[...].reshape(rows // batch, LANES)
        return jnp.concatenate([t] * batch, axis=0)

    ca, sa1, sa2 = table(ca_ref), table(sa1_ref), table(sa2_ref)
    q = _dot(hb, wq_ref[...])
    scale = HEAD_DIM ** -0.5 * math.log2(math.e)
    for hd in range(N_HEADS):
        sl = slice(hd * HEAD_DIM, (hd + 1) * HEAD_DIM)
        put(q_ref, (_rope(q[:, sl], ca, sa1, sa2, ATT_ROT // 2) * scale).astype(BF16), sl)
    k = _dot(hb, wk_ref[...])
    for hd in range(N_KV_HEADS):
        sl = slice(hd * HEAD_DIM, (hd + 1) * HEAD_DIM)
        put(k_ref, _rope(k[:, sl], ca, sa1, sa2, ATT_ROT // 2).astype(BF16), sl)

    ci, si1, si2 = table(ci_ref), table(si1_ref), table(si2_ref)
    qi = _dot(hb, wqi_ref[...])
    for pr in range(IDX_HEADS * IDX_DIM // LANES):
        sl = slice(pr * LANES, (pr + 1) * LANES)
        put(qi_ref, _rope(qi[:, sl], ci, si1, si2, IDX_ROT // 2).astype(BF16), sl)

    sm = _dot(hb, wsm_ref[...])
    sm = _rope(sm, table(cs_ref), table(ss1_ref), table(ss2_ref), IDX_ROT // 2)
    lane = lax.broadcasted_iota(I32, sm.shape, 1)
    put(ki2_ref, jnp.where(lane < IDX_DIM, sm, pltpu.roll(sm, IDX_DIM, 1)).astype(BF16))
    put(wi_ref, sm * (IDX_DIM ** -0.5 * IDX_HEADS ** -0.5))

    _store_lane_blocks(u_scr, _dot(hb, wu_ref[...]))
    for t in range(rt):
        u_ref[t * n_st:(t + 1) * n_st, :] = _strided_rows(u_scr, t, n_st, rt).astype(BF16)

    v = _dot(hb, wv_ref[...]).astype(BF16)
    ones = jnp.ones((rows, HEAD_DIM), BF16)
    for hd in range(N_KV_HEADS):
        put(v_ref, v[:, hd * HEAD_DIM:(hd + 1) * HEAD_DIM], slice(2 * hd * HEAD_DIM, (2 * hd + 1) * HEAD_DIM))
        put(v_ref, ones, slice((2 * hd + 1) * HEAD_DIM, (2 * hd + 2) * HEAD_DIM))
    put(gate_ref, _dot(hb, wg_ref[...]))


def _rope_tables(L, rot, period, active_lanes):
    half = rot // 2
    pos = jnp.arange(L, dtype=jnp.int32)
    inv_freq = ROPE_THETA ** (-jnp.arange(half, dtype=F32) / half)
    ang = pos.astype(F32)[:, None] * inv_freq[None, :]
    cos, sin = jnp.cos(ang), jnp.sin(ang)
    lane = np.arange(LANES)
    within = lane % period
    fidx = np.where(within < rot, within % half, 0)
    is_x1 = (within < half) & (lane < active_lanes)
    is_x2 = (within >= half) & (within < rot) & (lane < active_lanes)
    rot_lane = is_x1 | is_x2
    c = jnp.where(rot_lane[None, :], cos[:, fidx], 1.0)
    s1 = jnp.where(is_x2[None, :], sin[:, fidx], 0.0)
    s2 = jnp.where(is_x1[None, :], -sin[:, fidx], 0.0)
    return c.astype(F32), s1.astype(F32), s2.astype(F32)


def _project(x, g1, w_in):
    B, L, _ = x.shape
    n_seg = S5_STREAMS // B
    l_seg = L // n_seg
    rt = ROW_TILE
    o = 0
    parts = []
    for s in (D_MODEL, ATT_WIDTH, KV_WIDTH, KV_WIDTH, IDX_HEADS * IDX_DIM, IDX_DIM, IDX_HEADS, D_MODEL, D_MODEL):
        parts.append(w_in[:, o:o + s])
        o += s
    wu, wq, wk, wv, wqi, wki, wwi, wga, wgb = parts
    wsm = jnp.concatenate([wki, wwi, jnp.zeros((D_MODEL, LANES - IDX_DIM - IDX_HEADS), w_in.dtype)], axis=1)
    wg = jnp.concatenate([wga, wgb], axis=1)
    ws = [w.astype(BF16) for w in (wu, wq, wk, wv, wqi, wsm, wg)]
    tabs = (_rope_tables(L, ATT_ROT, HEAD_DIM, LANES)
            + _rope_tables(L, IDX_ROT, IDX_DIM, LANES)
            + _rope_tables(L, IDX_ROT, IDX_DIM, IDX_DIM))
    tabs = [t.reshape(n_seg, l_seg, LANES) for t in tabs]

    def streams(w):
        return pl.BlockSpec((S5_STREAMS, rt, w), lambda i: (0, i, 0))

    def full(a):
        return pl.BlockSpec(a.shape, lambda i: (0, 0))

    tab_spec = pl.BlockSpec((n_seg, rt, LANES), lambda i: (0, i, 0))
    widths = (ATT_WIDTH,
              KV_WIDTH,
              2 * KV_WIDTH,
              IDX_HEADS * IDX_DIM,
              LANES,
              LANES,
              2 * D_MODEL)
    dtypes = (BF16, BF16, BF16, BF16, BF16, F32, F32)
    out_shape = ((jax.ShapeDtypeStruct((l_seg // rt, rt * S5_STREAMS, D_MODEL), BF16),)
                 + tuple(jax.ShapeDtypeStruct((S5_STREAMS, l_seg, w), dt) for w, dt in zip(widths, dtypes)))
    out_specs = ((pl.BlockSpec((None, rt * S5_STREAMS, D_MODEL), lambda i: (i, 0, 0)),)
                 + tuple(streams(w) for w in widths))
    return pl.pallas_call(
        functools.partial(_proj_kernel, batch=B),
        grid=(l_seg // rt,),
        in_specs=[streams(D_MODEL), full(g1)] + [full(w) for w in ws] + [tab_spec] * 9,
        out_specs=out_specs,
        out_shape=out_shape,
        scratch_shapes=[pltpu.VMEM((D_MODEL // LANES, rt * S5_STREAMS, LANES), F32)],
        compiler_params=pltpu.CompilerParams(dimension_semantics=("arbitrary",), vmem_limit_bytes=VMEM_LIMIT),
        name="proj",
    )(x.reshape(S5_STREAMS, l_seg, D_MODEL), g1, *ws, *tabs)


def _s5_group_weights(ldt, ar, ai, btr, bti, cre, cim):
    lane = lax.broadcasted_iota(I32, (1, LANES), 1)
    lo = lane < SSM_STATE
    dt = jnp.exp(ldt)
    rho, th = ar * dt, ai * dt
    npow = S5_CHUNK + 1
    nn = lax.broadcasted_iota(I32, (24, LANES), 0).astype(F32)
    mag = jnp.exp(nn * rho)
    lc = mag * jnp.cos(nn * th)
    ls = mag * jnp.sin(nn * th)
    lam_pk = jnp.where(lo, lc, ls)
    lam_sw = jnp.where(lo, -ls, lc)
    num_re, num_im = lc[1:2] - 1.0, ls[1:2]
    den = ar * ar + ai * ai
    f_re = (num_re * ar + num_im * ai) / den
    f_im = (num_im * ar - num_re * ai) / den
    g_re = btr * f_re - bti * f_im
    g_im = btr * f_im + bti * f_re
    g_neg = jnp.where(lo, g_re, -g_im)
    w_pk = [cre * lam_pk[n:n + 1] + cim * lam_sw[n:n + 1] for n in range(npow)]
    w_state = [g_re * lam_pk[S5_CHUNK - 1 - ti:S5_CHUNK - ti] + g_im * lam_sw[S5_CHUNK - 1 - ti:S5_CHUNK - ti]
               for ti in range(S5_CHUNK)]
    a_p = jnp.where(lo, lam_pk[S5_CHUNK:S5_CHUNK + 1], lam_sw[S5_CHUNK:S5_CHUNK + 1])
    a_q = jnp.where(lo, lam_sw[S5_CHUNK:S5_CHUNK + 1], lam_pk[S5_CHUNK:S5_CHUNK + 1])
    return g_neg, w_pk, w_state, a_p, a_q


def _place(block, g):
    z = jnp.zeros_like(block)
    return jnp.concatenate([block if j == g else z for j in range(S5_LANE_GROUPS)], axis=1)


def _swap_halves(x):
    return jnp.concatenate([pltpu.roll(x[:, g * LANES:(g + 1) * LANES], SSM_STATE, 1)
                            for g in range(x.shape[1] // LANES)], axis=1)


def _s5_kernel(up_ref, are_ref, aim_ref, ldt_ref, btr_ref, bti_ref, cre_ref, cim_ref, dsk_ref,
               y_ref, s_ref, ssw_ref, hprev_ref, kbd_ref, wst_ref, wot_ref, *, n_steps, n_seg):
    ng = S5_LANE_GROUPS
    lane = lax.broadcasted_iota(I32, (1, LANES), 1)
    lo = lane < SSM_STATE
    weights = [_s5_group_weights(ldt_ref[g], are_ref[g], aim_ref[g], btr_ref[g], bti_ref[g], cre_ref[g], cim_ref[g])
               for g in range(ng)]
    gneg_blk = jnp.concatenate([_place(weights[g][0], g) for g in range(ng)], axis=0)
    for tau in range(S5_CHUNK):
        wt = jnp.concatenate([_place(weights[g][1][tau], g) for g in range(ng)], axis=0)
        kbd_ref[:, tau * LANES:(tau + 1) * LANES] = lax.dot_general(
            gneg_blk, wt, (((1,), (1,)), ((), ())), preferred_element_type=F32, precision=lax.Precision.HIGHEST)
    for ti in range(S5_CHUNK):
        wst_ref[ti * LANES:(ti + 1) * LANES, :] = jnp.concatenate(
            [_place(weights[g][2][ti], g) for g in range(ng)], axis=0).astype(BF16)
        wot_ref[ti * LANES:(ti + 1) * LANES, :] = jnp.concatenate(
            [_place(jnp.where(lo, weights[g][1][ti + 1], -weights[g][1][ti + 1]), g) for g in range(ng)],
            axis=0).astype(BF16)
    a_p = jnp.concatenate([weights[g][3] for g in range(ng)], axis=1)
    a_q = jnp.concatenate([weights[g][4] for g in range(ng)], axis=1)

    rows = n_steps * S5_STREAMS

    def chunk_rows(ti):
        return up_ref[:, ti * S5_STREAMS:(ti + 1) * S5_STREAMS, :].reshape(rows, LANES)

    lhs = jnp.concatenate([chunk_rows(ti) for ti in range(S5_CHUNK)], axis=1)
    s_all = _dot(lhs, wst_ref[...])
    s_ref[...] = s_all
    ssw_ref[...] = _swap_halves(s_all)

    def cmul(x, p, q):
        return x * p + _swap_halves(x) * q

    def step(h, hs, i):
        rows_i = pl.ds(pl.multiple_of(i * S5_STREAMS, S5_STREAMS), S5_STREAMS)
        return h * a_p + hs * a_q + s_ref[rows_i, :], hs * a_p - h * a_q + ssw_ref[rows_i, :]

    zero = jnp.zeros((S5_STREAMS, ng * LANES), F32)
    z, _ = lax.fori_loop(0, n_steps, lambda i, c: step(c[0], c[1], i), (zero, zero))

    lo_all = jnp.concatenate([lo] * ng, axis=1)

    def factors(zpk):
        sw = _swap_halves(zpk)
        return jnp.where(lo_all, zpk, sw), jnp.where(lo_all, -sw, zpk)

    base, seg, n = jnp.where(lo_all, a_p, a_q), None, n_steps
    while n:
        if n & 1:
            seg = base if seg is None else cmul(seg, *factors(base))
        n >>= 1
        if n:
            base = cmul(base, *factors(base))
    seg_p, seg_q = factors(seg)

    row_id = lax.broadcasted_iota(I32, (S5_STREAMS, ng * LANES), 0)
    init = zero
    prev = None
    for s in range(S5_STREAMS):
        if s % n_seg == 0:
            cur = jnp.zeros((1, ng * LANES), F32)
        else:
            cur = cmul(prev, seg_p, seg_q) + z[s - 1:s]
            init = jnp.where(row_id == s, cur, init)
        prev = cur

    def pass2(i, c):
        hprev_ref[pl.ds(pl.multiple_of(i * S5_STREAMS, S5_STREAMS), S5_STREAMS), :] = c[0]
        return step(c[0], c[1], i)

    lax.fori_loop(0, n_steps, pass2, (init, _swap_halves(init)))

    hp = hprev_ref[...].astype(BF16)
    dsk = dsk_ref[...]
    zblk = jnp.zeros((LANES, LANES), F32)
    for tp in range(S5_CHUNK // 2):
        kdim = (2 * tp + 2) * LANES
        cols = []
        for t2 in (2 * tp, 2 * tp + 1):
            cols.append(jnp.concatenate(
                [kbd_ref[:, (t2 - ti) * LANES:(t2 - ti + 1) * LANES] if ti <= t2 else zblk
                 for ti in range(2 * tp + 2)], axis=0))
        slab = jnp.concatenate(cols, axis=1).astype(BF16)
        y = _dot(lhs[:, :kdim], slab) + _dot_nt(hp, wot_ref[2 * tp * LANES:(2 * tp + 2) * LANES, :])
        for half in range(2):
            t2 = 2 * tp + half
            u_t = up_ref[:, t2 * S5_STREAMS:(t2 + 1) * S5_STREAMS, :].astype(F32)
            y_t = y[:, half * LANES:(half + 1) * LANES].reshape(n_steps, S5_STREAMS, LANES) + u_t * dsk
            y_ref[:, t2 * S5_STREAMS:(t2 + 1) * S5_STREAMS, :] = y_t.astype(y_ref.dtype)


def _s5_branch(up, B, a_re, a_im, log_dt, b_re, b_im, c_re, c_im, d_skip):
    G, C = SSM_GROUPS, SSM_GROUP
    n_steps = up.shape[0]
    n_seg = S5_STREAMS // B
    rows = n_steps * S5_STREAMS
    ng = S5_LANE_GROUPS

    def dup(a):
        return jnp.concatenate([a, a], axis=-1).astype(F32)

    are2 = dup(a_re)[:, None, :]
    aim2 = dup(a_im)[:, None, :]
    ldt = log_dt.astype(F32)[:, None, None]
    btr2 = dup(jnp.swapaxes(b_re, 1, 2))
    bti2 = dup(jnp.swapaxes(b_im, 1, 2))
    cre2 = dup(c_re)
    cim2 = dup(c_im)
    dsk = d_skip.astype(F32).reshape(G // ng, 1, ng * C)

    def per_block(shape):
        return pl.BlockSpec((ng,) + shape, lambda i: (i, 0, 0))

    act = pl.BlockSpec((n_steps, S5_CHUNK * S5_STREAMS, LANES), lambda i: (0, 0, i))
    return pl.pallas_call(
        functools.partial(_s5_kernel, n_steps=n_steps, n_seg=n_seg),
        grid=(G // ng,),
        in_specs=[act, per_block((1, LANES)), per_block((1, LANES)), per_block((1, 1)),
                  per_block((C, LANES)), per_block((C, LANES)), per_block((C, LANES)), per_block((C, LANES)),
                  pl.BlockSpec((None, 1, LANES), lambda i: (i, 0, 0))],
        out_specs=act,
        out_shape=jax.ShapeDtypeStruct(up.shape, BF16),
        scratch_shapes=[pltpu.VMEM((rows, ng * LANES), F32),
                        pltpu.VMEM((rows, ng * LANES), F32),
                        pltpu.VMEM((rows, ng * LANES), F32),
                        pltpu.VMEM((LANES, S5_CHUNK * LANES), F32),
                        pltpu.VMEM((S5_CHUNK * LANES, ng * LANES), BF16),
                        pltpu.VMEM((S5_CHUNK * LANES, ng * LANES), BF16)],
        compiler_params=pltpu.CompilerParams(dimension_semantics=("arbitrary",), vmem_limit_bytes=VMEM_LIMIT),
        name="s5",
    )(up, are2, aim2, ldt, btr2, bti2, cre2, cim2, dsk)


def _float_to_key(x):
    bits = lax.bitcast_convert_type(x, I32)
    return bits ^ (lax.shift_right_arithmetic(bits, 31) & 0x7FFFFFFF)


def _key_to_float(key):
    return lax.bitcast_convert_type(key ^ (lax.shift_right_arithmetic(key, 31) & 0x7FFFFFFF), F32)


def _probit(p):
    t = jnp.sqrt(-2.0 * jnp.log(jnp.minimum(p, 1.0 - p)))
    z = t - ((0.010328 * t + 0.802853) * t + 2.515517) / (((0.001308 * t + 0.189269) * t + 1.432788) * t + 1.0)
    return jnp.where(p < 0.5, -z, z)


def _count_ge(sc_ref, trial, n_tiles, ktile):
    def count_tile(t, acc):
        off = pl.multiple_of(t * ktile, ktile)
        for j in range(ktile // COUNT_ROWS):
            rows = sc_ref[pl.ds(off + j * COUNT_ROWS, COUNT_ROWS), :]
            acc = jnp.where(rows >= trial, acc + 1, acc)
        return acc

    acc = lax.fori_loop(0, n_tiles, count_tile, jnp.zeros((COUNT_ROWS, trial.shape[1]), I32))
    return jnp.sum(acc, axis=0, keepdims=True).astype(F32)


def _select_threshold(sc_ref, s_min, s_max, n_allowed, n_tiles, k_sel, ktile):
    kf = float(k_sel)
    n_eff = jnp.maximum(n_allowed, kf + 1.0)
    z_t = _probit(1.0 - (kf - 0.5) / n_eff)

    def z_of(c):
        return _probit(jnp.clip(1.0 - c / n_eff, 0.5 / n_eff, 1.0 - 0.5 / n_eff))

    def body(st):
        it, t_lo, t_hi, c_lo, c_hi, z_lo, z_hi, w_lo, w_hi, side, done, thr = st
        g_lo = (z_lo - z_t) * w_lo
        g_hi = (z_hi - z_t) * w_hi
        t = t_lo + (t_hi - t_lo) * jnp.clip(g_lo / (g_lo - g_hi), 0.0, 1.0)
        k_lo, k_hi, k_t = _float_to_key(t_lo), _float_to_key(t_hi), _float_to_key(t)
        mid = lax.shift_right_arithmetic(k_lo, 1) + lax.shift_right_arithmetic(k_hi, 1) + (k_lo & k_hi & 1)
        use_mid = (k_t <= k_lo) | (k_t >= k_hi) | (c_lo - c_hi <= 4.0) | (it >= 20)
        t = _key_to_float(jnp.where(use_mid, mid, k_t))
        c = _count_ge(sc_ref, t, n_tiles, ktile)
        active = done == 0
        hit = active & (c == kf)
        new_lo = active & (c >= kf)
        new_hi = active & (c < kf)
        z_c = z_of(c)
        w_hi = jnp.where(new_lo, jnp.where(side > 0, 0.5 * w_hi, w_hi), 1.0)
        w_lo = jnp.where(new_hi, jnp.where(side < 0, 0.5 * w_lo, w_lo), 1.0)
        side = jnp.where(new_lo, 1.0, jnp.where(new_hi, -1.0, side))
        t_lo = jnp.where(new_lo, t, t_lo)
        c_lo = jnp.where(new_lo, c, c_lo)
        z_lo = jnp.where(new_lo, z_c, z_lo)
        t_hi = jnp.where(new_hi, t, t_hi)
        c_hi = jnp.where(new_hi, c, c_hi)
        z_hi = jnp.where(new_hi, z_c, z_hi)
        adjacent = active & (_float_to_key(t_hi) - 1 <= _float_to_key(t_lo))
        thr = jnp.where(hit, t, jnp.where(adjacent, t_lo, thr))
        done = jnp.where(hit | adjacent, 1, done)
        return it + 1, t_lo, t_hi, c_lo, c_hi, z_lo, z_hi, w_lo, w_hi, side, done, thr

    def cond(st):
        return jnp.logical_and(st[0] < 64, jnp.min(st[10]) < 1)

    few = n_allowed <= kf
    ones = jnp.ones(n_allowed.shape, F32)
    zeros = jnp.zeros(n_allowed.shape, F32)
    init = (jnp.int32(0), s_min, _key_to_float(_float_to_key(s_max) + 1), n_allowed, zeros,
            z_of(n_allowed), z_of(zeros), ones, ones, zeros,
            few.astype(I32), jnp.where(few, -jnp.inf, s_min))
    st = lax.fori_loop(0, SELECT_STEPS, lambda _, s: body(s), init)
    it, t_lo, t_hi, c_lo, c_hi, z_lo, z_hi, w_lo, w_hi, side, done, thr = st
    rank_hi = kf - c_hi
    rank_lo = c_lo - kf + 1.0
    use_hi = rank_hi <= EXTRACT_RANKS
    resolved = (done == 0) & (use_hi | (rank_lo <= EXTRACT_RANKS))
    top = _top_ranked(sc_ref, use_hi, t_lo, t_hi, n_tiles, ktile)
    rank = jnp.where(use_hi, rank_hi, rank_lo)
    picked = top[EXTRACT_RANKS - 1]
    for j in range(EXTRACT_RANKS - 1, 0, -1):
        picked = jnp.where(rank <= j, top[j - 1], picked)
    thr = jnp.where(resolved, jnp.where(use_hi, picked, -picked), thr)
    done = jnp.where(resolved, 1, done)
    st = (it, t_lo, t_hi, c_lo, c_hi, z_lo, z_hi, w_lo, w_hi, side, done, thr)
    return lax.while_loop(cond, body, st)[11]


def _insert_sorted(tops, x):
    out = []
    for a in tops:
        out.append(jnp.maximum(a, x))
        x = jnp.minimum(a, x)
    return out


def _top_ranked(sc_ref, use_hi, t_lo, t_hi, n_tiles, ktile):
    width = t_lo.shape[1]
    sign = jnp.where(use_hi, 1.0, -1.0)
    bound = jnp.where(use_hi, t_hi, _key_to_float(_float_to_key(-t_lo) + 1))

    def candidates(x):
        y = x * sign
        return jnp.where(y < bound, y, -jnp.inf)

    def sweep_tile(t, tops):
        off = pl.multiple_of(t * ktile, ktile)
        tops = list(tops)
        for j in range(ktile // 8):
            tops = _insert_sorted(tops, candidates(sc_ref[pl.ds(off + j * 8, 8), :]))
        return tuple(tops)

    init = tuple(jnp.full((8, width), -jnp.inf, F32) for _ in range(EXTRACT_RANKS))
    tops8 = lax.fori_loop(0, n_tiles, sweep_tile, init)
    tops = [jnp.full((1, width), -jnp.inf, F32) for _ in range(EXTRACT_RANKS)]
    for a in tops8:
        for r in range(8):
            tops = _insert_sorted(tops, a[r:r + 1, :])
    return tops


def _dsa_kernel(qi_ref, wi_ref, q_ref, ki2_ref, k_ref, v_ref, o_ref,
                sc_ref, qm_ref, qg_ref, m_ref, acc_ref, rel0_ref, rel1_ref, lg0_ref, lg1_ref, *, k_sel, qblk, ktile):
    assert qblk == 2 * LANES
    hq = LANES
    qb = pl.program_id(1)
    n_tiles = lax.div(qb * qblk + qblk + ktile - 1, ktile)
    lane = lax.broadcasted_iota(I32, (hq, LANES), 1)
    n_lt = ktile // LANES
    halves = (0, 1)

    eye = (lax.broadcasted_iota(I32, (hq, LANES), 0) == lane).astype(BF16)
    for h in halves:
        rows_h = slice(h * hq, (h + 1) * hq)
        for hd in range(IDX_HEADS):
            pair = qi_ref[rows_h, (hd // 2) * LANES:(hd // 2 + 1) * LANES]
            msk = (lane < IDX_DIM) if hd % 2 == 0 else (lane >= IDX_DIM)
            qm_ref[h, hd * hq:(hd + 1) * hq, :] = jnp.where(msk, pair, jnp.zeros_like(pair))
        for hd in range(N_HEADS):
            g, r = divmod(hd, Q_PER_KV)
            qg_ref[h, g, r * hq:(r + 1) * hq, 0:HEAD_DIM] = q_ref[rows_h, hd * HEAD_DIM:(hd + 1) * HEAD_DIM]
            qg_ref[h, g, r * hq:(r + 1) * hq, HEAD_DIM:2 * HEAD_DIM] = eye

    w_t = [wi_ref[h * hq:(h + 1) * hq, :].T for h in halves]
    q_chunk = lax.shift_right_logical(qb * qblk + lax.broadcasted_iota(I32, (1, qblk), 1), 6)
    n_allowed = ((q_chunk + 1) * CHUNK).astype(F32)

    rel_refs = (rel0_ref, rel1_ref)
    lg_refs = (lg0_ref, lg1_ref)

    def tile_offset(t):
        return pl.multiple_of(t * ktile, ktile)

    def issue_logits(h, t):
        rel_refs[h][...] = _dot_nt(ki2_ref[pl.ds(tile_offset(t), ktile), :], qm_ref[h])

    def score_half(h, t, carry):
        mx, mn = carry
        off = pl.multiple_of(t * ktile, ktile)
        q_chunk_h = q_chunk[:, h * hq:(h + 1) * hq]
        for c in range(ktile // SCORE_ROWS):
            r0 = c * SCORE_ROWS
            sc = jnp.zeros((SCORE_ROWS, LANES), F32)
            for hd in range(IDX_HEADS):
                rel = rel_refs[h][r0:r0 + SCORE_ROWS, hd * hq:(hd + 1) * hq]
                sc = sc + jnp.maximum(rel, 0.0) * w_t[h][IDX_DIM + hd:IDX_DIM + hd + 1, :]
            k_chunk = lax.shift_right_logical(off + r0 + lax.broadcasted_iota(I32, (SCORE_ROWS, 1), 0), 6)
            ok = k_chunk <= q_chunk_h
            sc_ref[pl.ds(off + r0, SCORE_ROWS), h * hq:(h + 1) * hq] = jnp.where(ok, sc, NEG)
            mx = jnp.maximum(mx, jnp.where(ok, sc, -jnp.inf))
            mn = jnp.minimum(mn, jnp.where(ok, sc, jnp.inf))
        return mx, mn

    def score_tile(t, carry, last=False):
        issue_logits(1, t)
        c0 = score_half(0, t, carry[0])
        if not last:
            issue_logits(0, t + 1)
        return c0, score_half(1, t, carry[1])

    issue_logits(0, 0)
    init = (jnp.full((SCORE_ROWS, LANES), -jnp.inf, F32), jnp.full((SCORE_ROWS, LANES), jnp.inf, F32))
    n_main = n_tiles - 1
    n_quads = lax.shift_right_logical(n_main, 2)
    has_pair = (n_main & 2) == 2
    has_single = (n_main & 1) == 1
    pair_start = 4 * n_quads
    single_start = n_main - 1

    def score_run(t0, count, c):
        for j in range(count):
            c = score_tile(t0 + j, c)
        return c

    carry = lax.fori_loop(0, n_quads, lambda p, c: score_run(4 * p, 4, c), (init, init))
    carry = lax.cond(has_pair, lambda: score_run(pair_start, 2, carry), lambda: carry)
    carry = lax.cond(has_single, lambda: score_tile(single_start, carry), lambda: carry)
    (mx0, mn0), (mx1, mn1) = score_tile(n_tiles - 1, carry, last=True)
    mx = jnp.concatenate([mx0, mx1], axis=1)
    mn = jnp.concatenate([mn0, mn1], axis=1)
    s_max = jnp.max(mx, axis=0, keepdims=True)
    s_min = jnp.min(mn, axis=0, keepdims=True)

    thr = _select_threshold(sc_ref, s_min, s_max, n_allowed, n_tiles, k_sel, ktile)
    thr = jnp.maximum(thr, _key_to_float(jnp.full((1, qblk), KEY_HALF_NEG + 1, I32)))

    m_ref[...] = jnp.full(m_ref.shape, -jnp.inf, F32)
    acc_ref[...] = jnp.zeros(acc_ref.shape, F32)
    g_rows = Q_PER_KV * hq

    def selection_bias(h, t):
        keep = sc_ref[pl.ds(tile_offset(t), ktile), h * hq:(h + 1) * hq] >= thr[:, h * hq:(h + 1) * hq]
        return jnp.where(keep, 0.0, NEG).astype(BF16)

    def issue_qk(h, t, g, bias_t):
        kb = jnp.concatenate([k_ref[pl.ds(tile_offset(t), ktile), g * HEAD_DIM:(g + 1) * HEAD_DIM], bias_t], axis=1)
        lg_refs[h][g] = _dot_nt(qg_ref[h, g], kb)

    def attend_half(h, t, issue_next=True):
        off = pl.multiple_of(t * ktile, ktile)
        nh, nt = (1, t) if h == 0 else (0, t + 1)
        bias_next = selection_bias(nh, nt) if issue_next else None
        for g in range(N_KV_HEADS):
            if issue_next:
                issue_qk(nh, nt, g, bias_next)
            ps, alphas = [], []
            for c in range(g_rows // ATT_ROWS):
                r0 = c * ATT_ROWS
                s = lg_refs[h][g, r0:r0 + ATT_ROWS, :]
                m_cur = s[:, 0:LANES]
                for j in range(1, n_lt):
                    m_cur = jnp.maximum(m_cur, s[:, j * LANES:(j + 1) * LANES])
                m_old = m_ref[h, g, r0:r0 + ATT_ROWS, :]
                m_new = jnp.maximum(m_old, jnp.max(m_cur, axis=1, keepdims=True))
                alphas.append(jnp.exp2(m_old - m_new))
                m_ref[h, g, r0:r0 + ATT_ROWS, :] = m_new
                ps.append(jnp.exp2(s - jnp.concatenate([m_new] * n_lt, axis=1)).astype(BF16))
            alpha = jnp.concatenate(alphas, axis=0)
            pv = _dot(jnp.concatenate(ps, axis=0), v_ref[pl.ds(off, ktile), 2 * g * HEAD_DIM:(2 * g + 2) * HEAD_DIM])
            acc_ref[h, g] = acc_ref[h, g] * jnp.concatenate([alpha, alpha], axis=1) + pv

    def attend_tile(t, carry):
        attend_half(0, t)
        attend_half(1, t)
        return carry

    bias_0 = selection_bias(0, 0)
    for g in range(N_KV_HEADS):
        issue_qk(0, 0, g, bias_0)
    def attend_run(t0, count):
        for j in range(count):
            attend_tile(t0 + j, 0)

    def attend_quad(p, c):
        attend_run(4 * p, 4)
        return c

    lax.fori_loop(0, n_quads, attend_quad, 0)

    @pl.when(has_pair)
    def _():
        attend_run(pair_start, 2)

    @pl.when(has_single)
    def _():
        attend_tile(single_start, 0)

    attend_half(0, n_tiles - 1)
    attend_half(1, n_tiles - 1, issue_next=False)

    for h in halves:
        for hd in range(N_HEADS):
            g, r = divmod(hd, Q_PER_KV)
            a = acc_ref[h, g, r * hq:(r + 1) * hq, :]
            o_ref[h * hq:(h + 1) * hq, hd * HEAD_DIM:(hd + 1) * HEAD_DIM] = (
                a[:, :HEAD_DIM] / a[:, HEAD_DIM:]).astype(o_ref.dtype)


def _dsa_branch(q, k, v, qi, ki2, wi, B, L, qblk=2 * LANES, ktile=512):
    k_sel = min(TOPK_MAX, L // 4)
    nq = L // qblk
    g_rows = Q_PER_KV * LANES
    assert L % ktile == 0 and ktile % qblk == 0

    def qrow(w):
        return pl.BlockSpec((None, qblk, w), lambda b, i: (b, i, 0))

    def whole(w):
        return pl.BlockSpec((None, L, w), lambda b, i: (b, 0, 0), pipeline_mode=pl.Buffered(1))

    r3 = lambda a: a.reshape(B, L, a.shape[-1])
    out = pl.pallas_call(
        functools.partial(_dsa_kernel, k_sel=k_sel, qblk=qblk, ktile=ktile),
        grid=(B, nq),
        in_specs=[qrow(IDX_HEADS * IDX_DIM), qrow(LANES), qrow(ATT_WIDTH), whole(LANES), whole(KV_WIDTH),
                  whole(2 * KV_WIDTH)],
        out_specs=qrow(ATT_WIDTH),
        out_shape=jax.ShapeDtypeStruct((B, L, ATT_WIDTH), BF16),
        scratch_shapes=[pltpu.VMEM((L, qblk), F32),
                        pltpu.VMEM((2, IDX_HEADS * LANES, LANES), BF16),
                        pltpu.VMEM((2, N_KV_HEADS, g_rows, 2 * HEAD_DIM), BF16),
                        pltpu.VMEM((2, N_KV_HEADS, g_rows, LANES), F32),
                        pltpu.VMEM((2, N_KV_HEADS, g_rows, 2 * HEAD_DIM), F32),
                        pltpu.VMEM((ktile, IDX_HEADS * LANES), F32),
                        pltpu.VMEM((ktile, IDX_HEADS * LANES), F32),
                        pltpu.VMEM((N_KV_HEADS, g_rows, ktile), F32),
                        pltpu.VMEM((N_KV_HEADS, g_rows, ktile), F32)],
        compiler_params=pltpu.CompilerParams(dimension_semantics=("arbitrary", "arbitrary"),
                                             vmem_limit_bytes=VMEM_LIMIT),
        name="dsa",
    )(r3(qi), r3(wi), r3(q), r3(ki2), r3(k), r3(v))
    return out.reshape(B * L, ATT_WIDTH)


def _gelu_tanh(x):
    return 0.5 * x * (1.0 + jnp.tanh(math.sqrt(2.0 / math.pi) * (x + 0.044715 * (x * x * x))))


def _merge_kernel(x_ref, y_ref, yb_ref, gate_ref, wglu_ref, wa_ref, wb_ref, wo_ref, o_ref, y_scr):
    n_st, rt, d = x_ref.shape
    rows = n_st * rt
    _store_lane_blocks(y_scr, y_ref[...].astype(F32))
    y = jnp.concatenate([_strided_rows(y_scr, s, rt, n_st) for s in range(n_st)], axis=0)
    y = _gelu_tanh(y)
    ya = y * jax.nn.sigmoid(_dot(y.astype(BF16), wglu_ref[...]))
    gate = gate_ref[...].reshape(rows, 2 * d)
    ga, gb = gate[:, :D_MODEL], gate[:, D_MODEL:]
    merged = (jax.nn.sigmoid(ga) * _dot(ya.astype(BF16), wa_ref[...])
              + jax.nn.sigmoid(gb) * _dot(yb_ref[...].reshape(rows, ATT_WIDTH), wb_ref[...]))
    out = x_ref[...].reshape(rows, d) + _dot(merged.astype(BF16), wo_ref[...])
    o_ref[...] = out.reshape(n_st, rt, d)


def _merge(x, y, yb, gate, w_glu, w_a, w_b, w_out):
    n_st, l_seg, _ = x.shape
    rt = ROW_TILE
    ws = [w.astype(BF16) for w in (w_glu, w_a, w_b, w_out)]

    def streams(w):
        return pl.BlockSpec((n_st, rt, w), lambda i: (0, i, 0))

    return pl.pallas_call(
        _merge_kernel,
        grid=(l_seg // rt,),
        in_specs=[streams(D_MODEL), pl.BlockSpec((None, rt * n_st, D_MODEL), lambda i: (i, 0, 0)),
                  streams(ATT_WIDTH), streams(2 * D_MODEL)]
                 + [pl.BlockSpec(w.shape, lambda i: (0, 0)) for w in ws],
        out_specs=streams(D_MODEL),
        out_shape=jax.ShapeDtypeStruct((n_st, l_seg, D_MODEL), F32),
        scratch_shapes=[pltpu.VMEM((D_MODEL // LANES, rt * n_st, LANES), F32)],
        compiler_params=pltpu.CompilerParams(dimension_semantics=("arbitrary",), vmem_limit_bytes=VMEM_LIMIT),
        name="merge",
    )(x, y, yb, gate, *ws)


def _rms(x, g):
    return x * lax.rsqrt(jnp.mean(x * x, axis=-1, keepdims=True) + EPS) * g


def _ffn_kernel(x_ref, g2_ref, win_ref, wout_ref, gf_ref, o_ref, *, final_norm):
    x = x_ref[...]
    h = _rms(x, g2_ref[...]).astype(BF16)
    gu = _dot(h, win_ref[...])
    g, up = gu[:, :FFN_HIDDEN], gu[:, FFN_HIDDEN:]
    act = (g * jax.nn.sigmoid(g)) * up
    x = x + _dot(act.astype(BF16), wout_ref[...])
    o_ref[...] = _rms(x, gf_ref[...]) if final_norm else x


def _ffn(x1, g2, w_ffn_in, w_ffn_out, gf, final_norm, tile):
    T = x1.shape[0]
    win, wout = w_ffn_in.astype(BF16), w_ffn_out.astype(BF16)

    def row(w):
        return pl.BlockSpec((tile, w), lambda i: (i, 0))

    def full(a):
        return pl.BlockSpec(a.shape, lambda i: (0, 0))

    return pl.pallas_call(
        functools.partial(_ffn_kernel, final_norm=final_norm),
        grid=(T // tile,),
        in_specs=[row(D_MODEL), full(g2), full(win), full(wout), full(gf)],
        out_specs=row(D_MODEL),
        out_shape=jax.ShapeDtypeStruct((T, D_MODEL), F32),
        compiler_params=pltpu.CompilerParams(dimension_semantics=("arbitrary",), vmem_limit_bytes=VMEM_LIMIT),
        name="ffn",
    )(x1, g2, win, wout, gf)


def kernel(x, norm1_g, w_in, a_re, a_im, log_dt, b_re, b_im, c_re, c_im, d_skip, w_glu,
           w_branch_a, w_branch_b, w_out, norm2_g, w_ffn_in, w_ffn_out, norm_f_g):
    B, L, D = x.shape
    depth = norm1_g.shape[0]
    n_seg = S5_STREAMS // B
    assert D == D_MODEL and S5_STREAMS % B == 0 and L % 512 == 0 and L % (n_seg * ROW_TILE) == 0
    x = x.astype(F32)
    for i in range(depth):
        up, q, k, v, qi, ki2, wi, gate = _project(x, norm1_g[i][None, :].astype(F32), w_in[i])
        y = _s5_branch(up, B, a_re[i], a_im[i], log_dt[i], b_re[i], b_im[i], c_re[i], c_im[i], d_skip[i])
        yb = _dsa_branch(q, k, v, qi, ki2, wi, B, L)
        x1 = _merge(x.reshape(S5_STREAMS, L // n_seg, D), y, yb.reshape(S5_STREAMS, L // n_seg, ATT_WIDTH), gate,
                    w_glu[i], w_branch_a[i], w_branch_b[i], w_out[i])
        x = _ffn(x1.reshape(B * L, D), norm2_g[i][None, :].astype(F32), w_ffn_in[i], w_ffn_out[i],
                 norm_f_g[None, :].astype(F32), i == depth - 1, FFN_ROWS).reshape(B, L, D)
    return x
```
